```python
import math
import jax
import jax.numpy as jnp
from jax import lax
import numpy as np

D_MODEL = 2048
BATCH = 8
SEQ = 2048
DEPTH = 4

CHUNK = 64
GDN_HEADS = 8
GDN_HEAD_DIM = 128
GDN_WIDTH = GDN_HEADS * GDN_HEAD_DIM
GDN_CONV = 4
CONV_WIDTH = 1024
CONV_KERNEL = 31
ATT_HEADS = 8
ATT_KV_HEADS = 2
ATT_HEAD_DIM = 128
ATT_WIDTH = ATT_HEADS * ATT_HEAD_DIM
ATT_KV_WIDTH = ATT_KV_HEADS * ATT_HEAD_DIM
IDX_HEADS = 8
IDX_HEAD_DIM = 64
TOPK_MAX = 256
QBLOCK = 128
REL_BUCKETS = 32
REL_MAX_DIST = 128
N_BRANCH = 3
EPS = 1e-6

IN_SPLITS = (3 * GDN_WIDTH, GDN_WIDTH, GDN_HEADS, GDN_HEADS, 2 * CONV_WIDTH, CONV_WIDTH,
             ATT_WIDTH, ATT_KV_WIDTH, ATT_KV_WIDTH, ATT_WIDTH, IDX_HEADS * IDX_HEAD_DIM,
             IDX_HEAD_DIM, IDX_HEADS, N_BRANCH * D_MODEL)
N_IN = sum(IN_SPLITS)

kernel_name = 'hybrid_gdn_conformer_dsa_trunk'


def rms_norm(x, g):
    xf = x.astype(jnp.float32)
    y = xf * lax.rsqrt(jnp.mean(xf * xf, axis=-1, keepdims=True) + EPS)
    return (y * g.astype(jnp.float32)).astype(x.dtype)


def layer_norm(x, g, b):
    xf = x.astype(jnp.float32)
    mu = jnp.mean(xf, axis=-1, keepdims=True)
    var = jnp.mean(jnp.square(xf - mu), axis=-1, keepdims=True)
    y = (xf - mu) * lax.rsqrt(var + EPS)
    return (y * g.astype(jnp.float32) + b.astype(jnp.float32)).astype(x.dtype)


def l2_norm(x):
    xf = x.astype(jnp.float32)
    return xf * lax.rsqrt(jnp.sum(xf * xf, axis=-1, keepdims=True) + EPS)


def causal_depthwise_conv(x, w):
    k = w.shape[0]
    return lax.conv_general_dilated(
        x, w[:, None, :].astype(x.dtype), window_strides=(1,), padding=[(k - 1, 0)],
        dimension_numbers=('NWC', 'WIO', 'NWC'), feature_group_count=x.shape[-1])


def split_columns(p):
    outs = []
    start = 0
    for size in IN_SPLITS:
        outs.append(p[..., start:start + size])
        start += size
    return outs


def t5_bucket(rel):
    nb = REL_BUCKETS // 2
    max_exact = nb // 2
    offset = jnp.where(rel > 0, nb, 0)
    n = jnp.abs(rel)
    nf = jnp.maximum(n, 1).astype(jnp.float32)
    large = max_exact + (jnp.log(nf / max_exact) / math.log(REL_MAX_DIST / max_exact)
                         * (nb - max_exact)).astype(jnp.int32)
    large = jnp.minimum(large, nb - 1)
    return offset + jnp.where(n < max_exact, n, large)


def gated_delta_rule(q, k, v, g, beta):
    bsz, seq, nh, dk = q.shape
    dv = v.shape[-1]
    nc = seq // CHUNK
    f32 = jnp.float32

    def chunks(t):
        return t.astype(f32).reshape(bsz, nc, CHUNK, nh, -1).transpose(0, 1, 3, 2, 4)

    qc = chunks(q) * (dk ** -0.5)
    kc = chunks(k)
    vc = chunks(v)
    gc = jnp.cumsum(chunks(g[..., None])[..., 0], axis=-1)
    bc = chunks(beta[..., None])
    pos = jnp.arange(CHUNK)
    incl = pos[:, None] >= pos[None, :]
    strict = pos[:, None] > pos[None, :]
    decay = jnp.exp(jnp.where(incl, gc[..., :, None] - gc[..., None, :], -jnp.inf))
    kb = kc * bc
    a_strict = jnp.where(strict, jnp.einsum('bnhid,bnhjd->bnhij', kb, kc) * decay, 0.0)
    eye = jnp.eye(CHUNK, dtype=f32)
    t_inv = lax.linalg.triangular_solve(eye + a_strict, jnp.broadcast_to(eye, a_strict.shape),
                                        left_side=True, lower=True)
    u = t_inv @ (vc * bc)
    w = t_inv @ (kb * jnp.exp(gc)[..., None])
    intra = jnp.where(incl, jnp.einsum('bnhid,bnhjd->bnhij', qc, kc) * decay, 0.0)

    def step(state, inp):
        q_i, k_i, u_i, w_i, g_i, a_i = inp
        v_new = u_i - w_i @ state
        o_i = (q_i * jnp.exp(g_i)[..., None]) @ state + a_i @ v_new
        g_last = g_i[..., -1]
        k_dec = k_i * jnp.exp(g_last[..., None] - g_i)[..., None]
        state = state * jnp.exp(g_last)[..., None, None] + jnp.einsum('bhcd,bhce->bhde', k_dec, v_new)
        return state, o_i

    xs = tuple(jnp.moveaxis(t, 1, 0) for t in (qc, kc, u, w, gc, intra))
    s0 = jnp.zeros((bsz, nh, dk, dv), f32)
    _, o = lax.scan(step, s0, xs)
    return o.transpose(1, 0, 3, 2, 4).reshape(bsz, seq, nh, dv).astype(v.dtype)


def dsa_sparse_attention(q, k, v, q_idx, k_idx, w_idx, rel_bias):
    bsz, seq = q.shape[:2]
    topk = min(TOPK_MAX, seq // 4)
    nblk = seq // QBLOCK
    grp = ATT_HEADS // ATT_KV_HEADS
    key_pos = jnp.arange(seq)
    f32 = jnp.float32
    k_idx_f = k_idx.astype(f32)

    def block(i):
        start = i * QBLOCK
        qb = lax.dynamic_slice_in_dim(q, start, QBLOCK, axis=1)
        qib = lax.dynamic_slice_in_dim(q_idx, start, QBLOCK, axis=1).astype(f32)
        wib = lax.dynamic_slice_in_dim(w_idx, start, QBLOCK, axis=1).astype(f32)
        qpos = start + jnp.arange(QBLOCK)
        s_idx = jnp.einsum('bqhd,bsd->bqhs', qib, k_idx_f) * (IDX_HEAD_DIM ** -0.5)
        score = jnp.einsum('bqh,bqhs->bqs', wib * (IDX_HEADS ** -0.5), jax.nn.relu(s_idx))
        admissible = (key_pos[None, :] // CHUNK) <= (qpos[:, None] // CHUNK)
        score = jnp.where(admissible[None], score, -jnp.inf)
        _, sel = lax.top_k(score, topk)
        valid = (sel // CHUNK) <= (qpos[None, :, None] // CHUNK)
        kg = jax.vmap(lambda kk, ii: kk[ii])(k, sel).astype(f32)
        vg = jax.vmap(lambda vv, ii: vv[ii])(v, sel).astype(f32)
        qg = qb.reshape(bsz, QBLOCK, ATT_KV_HEADS, grp, ATT_HEAD_DIM).astype(f32)
        logits = jnp.einsum('bqhgd,bqnhd->bqhgn', qg, kg) * (ATT_HEAD_DIM ** -0.5)
        bias = rel_bias[t5_bucket(sel - qpos[None, :, None])].astype(f32)
        bias = bias.reshape(bsz, QBLOCK, topk, ATT_KV_HEADS, grp).transpose(0, 1, 3, 4, 2)
        logits = jnp.where(valid[:, :, None, None, :], logits + bias, -jnp.inf)
        p = jax.nn.softmax(logits, axis=-1)
        o = jnp.einsum('bqhgn,bqnhd->bqhgd', p, vg)
        return o.reshape(bsz, QBLOCK, ATT_WIDTH).astype(q.dtype)

    out = lax.map(block, jnp.arange(nblk))
    return out.transpose(1, 0, 2, 3).reshape(bsz, seq, ATT_WIDTH)


def setup_inputs(seed: int = 0) -> dict:
    key = jax.random.key(seed)
    ks = jax.random.split(key, 24)
    d = D_MODEL

    def nrm(k, shape, scale):
        return jax.random.normal(k, shape, jnp.float32) * scale

    return {
        'x': nrm(ks[0], (BATCH, SEQ, d), 1.0),
        'c': nrm(ks[1], (BATCH, d), 1.0),
        'rel_bias': nrm(ks[2], (REL_BUCKETS, ATT_HEADS), 0.5),
        'w_ada': nrm(ks[3], (DEPTH, d, 3 * d), 0.5 * d ** -0.5),
        'b_ada': nrm(ks[4], (DEPTH, 3 * d), 0.02),
        'norm_g': 1.0 + nrm(ks[5], (DEPTH, d), 0.02),
        'w_in': nrm(ks[6], (DEPTH, d, N_IN), d ** -0.5),
        'b_gate': nrm(ks[7], (DEPTH, N_BRANCH * d), 0.02),
        'conv_a_w': nrm(ks[8], (DEPTH, GDN_CONV, 3 * GDN_WIDTH), GDN_CONV ** -0.5),
        'a_log': jnp.log(jax.random.uniform(ks[9], (DEPTH, GDN_HEADS), jnp.float32, 1.0, 16.0)),
        'dt_bias': -4.0 + nrm(ks[10], (DEPTH, GDN_HEADS), 0.5),
        'gdn_norm_g': 1.0 + nrm(ks[11], (DEPTH, GDN_HEAD_DIM), 0.02),
        'w_proj_a': nrm(ks[12], (DEPTH, GDN_WIDTH, d), GDN_WIDTH ** -0.5),
        'conv_b_w': nrm(ks[13], (DEPTH, CONV_KERNEL, CONV_WIDTH), CONV_KERNEL ** -0.5),
        'conv_b_b': nrm(ks[14], (DEPTH, CONV_WIDTH), 0.02),
        'ln_b_g': 1.0 + nrm(ks[15], (DEPTH, CONV_WIDTH), 0.02),
        'ln_b_b': nrm(ks[16], (DEPTH, CONV_WIDTH), 0.02),
        'w_proj_b': nrm(ks[17], (DEPTH, CONV_WIDTH, d), CONV_WIDTH ** -0.5),
        'q_norm_g': 1.0 + nrm(ks[18], (DEPTH, ATT_HEAD_DIM), 0.02),
        'k_norm_g': 1.0 + nrm(ks[19], (DEPTH, ATT_HEAD_DIM), 0.02),
        'w_proj_c': nrm(ks[20], (DEPTH, ATT_WIDTH, d), ATT_WIDTH ** -0.5),
        'w_out': nrm(ks[21], (DEPTH, d, d), d ** -0.5),
    }


def reference(x, c, rel_bias, w_ada, b_ada, norm_g, w_in, b_gate, conv_a_w, a_log, dt_bias,
              gdn_norm_g, w_proj_a, conv_b_w, conv_b_b, ln_b_g, ln_b_b, w_proj_b, q_norm_g,
              k_norm_g, w_proj_c, w_out):
    bsz, seq, _ = x.shape
    f32 = jnp.float32
    cond = jax.nn.silu(c)
    for layer in range(DEPTH):
        shift, scale, gate = jnp.split((cond @ w_ada[layer] + b_ada[layer])[:, None, :], 3, axis=-1)
        h = rms_norm(x, norm_g[layer]) * (1.0 + scale) + shift
        (qkv_a, z_a, beta_raw, alpha_raw, glu_in, z_b, q_c, k_c, v_c, z_c,
         q_i, k_i, w_i, gate_raw) = split_columns(h @ w_in[layer])

        qkv_a = jax.nn.silu(causal_depthwise_conv(qkv_a, conv_a_w[layer]))
        q_a, k_a, v_a = (t.reshape(bsz, seq, GDN_HEADS, GDN_HEAD_DIM) for t in jnp.split(qkv_a, 3, axis=-1))
        beta = jax.nn.sigmoid(beta_raw.astype(f32))
        g = -jnp.exp(a_log[layer].astype(f32)) * jax.nn.softplus(alpha_raw.astype(f32) + dt_bias[layer])
        o_a = gated_delta_rule(l2_norm(q_a), l2_norm(k_a), v_a, g, beta)
        y_a = rms_norm(o_a, gdn_norm_g[layer]).reshape(bsz, seq, GDN_WIDTH) * jax.nn.silu(z_a)

        a_half, b_half = jnp.split(glu_in, 2, axis=-1)
        u = causal_depthwise_conv(a_half * jax.nn.sigmoid(b_half), conv_b_w[layer]) + conv_b_b[layer]
        y_b = jax.nn.silu(layer_norm(u, ln_b_g[layer], ln_b_b[layer])) * jax.nn.silu(z_b)

        q_c = rms_norm(q_c.reshape(bsz, seq, ATT_HEADS, ATT_HEAD_DIM), q_norm_g[layer])
        k_c = rms_norm(k_c.reshape(bsz, seq, ATT_KV_HEADS, ATT_HEAD_DIM), k_norm_g[layer])
        v_c = v_c.reshape(bsz, seq, ATT_KV_HEADS, ATT_HEAD_DIM)
        o_c = dsa_sparse_attention(q_c, k_c, v_c, q_i.reshape(bsz, seq, IDX_HEADS, IDX_HEAD_DIM),
                                   k_i, w_i, rel_bias)
        y_c = o_c * jax.nn.silu(z_c)

        g_a, g_b, g_c = jnp.split(jax.nn.sigmoid(gate_raw + b_gate[layer]), 3, axis=-1)
        merged = (g_a * (y_a @ w_proj_a[layer]) + g_b * (y_b @ w_proj_b[layer])
                  + g_c * (y_c @ w_proj_c[layer]))
        x = x + gate * (merged @ w_out[layer])
    return x
```

```python
import functools

import jax
import jax.numpy as jnp
from jax import lax
from jax.experimental import pallas as pl
from jax.experimental.pallas import tpu as pltpu

F32 = jnp.float32
BF16 = jnp.bfloat16
I32 = jnp.int32

D_MODEL = 2048
DEPTH = 4
CHUNK = 64
GDN_HEADS = 8
GDN_HEAD_DIM = 128
GDN_WIDTH = GDN_HEADS * GDN_HEAD_DIM
GDN_CONV = 4
CONV_WIDTH = 1024
CONV_KERNEL = 31
ATT_HEADS = 8
ATT_KV_HEADS = 2
ATT_HEAD_DIM = 128
ATT_WIDTH = ATT_HEADS * ATT_HEAD_DIM
ATT_KV_WIDTH = ATT_KV_HEADS * ATT_HEAD_DIM
ATT_GROUP = ATT_HEADS // ATT_KV_HEADS
IDX_HEADS = 8
IDX_HEAD_DIM = 64
TOPK_MAX = 256
REL_BUCKETS = 32
N_BRANCH = 3
EPS = 1e-6

_OFF_QKV_A = 0
_OFF_Z_A = _OFF_QKV_A + 3 * GDN_WIDTH
_OFF_BETA = _OFF_Z_A + GDN_WIDTH
_OFF_ALPHA = _OFF_BETA + GDN_HEADS
_OFF_GLU = _OFF_ALPHA + GDN_HEADS
_OFF_Z_B = _OFF_GLU + 2 * CONV_WIDTH
_OFF_Q_C = _OFF_Z_B + CONV_WIDTH
_OFF_K_C = _OFF_Q_C + ATT_WIDTH
_OFF_V_C = _OFF_K_C + ATT_KV_WIDTH
_OFF_Z_C = _OFF_V_C + ATT_KV_WIDTH
_OFF_Q_I = _OFF_Z_C + ATT_WIDTH
_OFF_K_I = _OFF_Q_I + IDX_HEADS * IDX_HEAD_DIM
_OFF_W_I = _OFF_K_I + IDX_HEAD_DIM
_OFF_GATE = _OFF_W_I + IDX_HEADS
N_IN = _OFF_GATE + N_BRANCH * D_MODEL

P_QKV_A = 0
P_Z_A = 3072
P_GLU = 4096
P_Z_B = 6144
P_Q_C = 7168
P_Z_C = 8192
P_GATE = 9216
P_K_C = 15360
P_V_C = 15616
NP_WIDE = 15872
S_Q_I = 0
S_K_I = 512
S_W_I = 576
S_BETA = 584
S_ALPHA = 592
NS_NARROW = 640

LANES = 128
DSA_BLOCK = 128
IDX_PAD = 256
VMEM_LIMIT = 56 * 1024 * 1024
NEG_BIG = -1e30
INT_MIN = -(2 ** 31)

NN = (((1,), (0,)), ((), ()))
NT = (((1,), (1,)), ((), ()))
TN = (((0,), (0,)), ((), ()))


def _dg(a, b, dims=NN):
    return lax.dot_general(a, b, dims, preferred_element_type=F32)


def _split2(x):
    hi = x.astype(BF16)
    lo = (x - hi.astype(F32)).astype(BF16)
    return hi, lo


def _dot_b(a, b, dims=NN):
    return _dg(a.astype(BF16), b.astype(BF16), dims)


def _dot_hp(a, b, dims=NN):
    ah, al = _split2(a)
    bh, bl = _split2(b)
    return _dg(ah, bh, dims) + (_dg(al, bh, dims) + _dg(ah, bl, dims))


def _dot_hp_exact_rhs(a, b_bf16, dims=NN):
    a0 = a.astype(BF16)
    r = a - a0.astype(F32)
    a1 = r.astype(BF16)
    a2 = (r - a1.astype(F32)).astype(BF16)
    return _dg(a0, b_bf16, dims) + (_dg(a1, b_bf16, dims) + _dg(a2, b_bf16, dims))


def _sigmoid(x):
    return 1.0 / (1.0 + jnp.exp(-x))


def _silu(x):
    return x * _sigmoid(x)


def _softplus(x):
    return jnp.maximum(x, 0.0) + jnp.log(1.0 + jnp.exp(-jnp.abs(x)))


def _cparams(sem):
    return pltpu.CompilerParams(dimension_semantics=sem, vmem_limit_bytes=VMEM_LIMIT)


def _ada_kernel(c_ref, w_ref, b_ref, o_ref):
    cond = _silu(c_ref[...])
    o_ref[...] = _dot_hp(cond, w_ref[...]) + b_ref[...]


def _ada_modulation(c, w_ada, b_ada):
    depth, d, n = w_ada.shape
    bsz = c.shape[0]
    tn = 512
    return pl.pallas_call(
        _ada_kernel,
        grid=(depth, n // tn),
        in_specs=[
            pl.BlockSpec((bsz, d), lambda l, j: (0, 0)),
            pl.BlockSpec((None, d, tn), lambda l, j: (l, 0, j)),
            pl.BlockSpec((None, 1, tn), lambda l, j: (l, 0, j)),
        ],
        out_specs=pl.BlockSpec((None, bsz, tn), lambda l, j: (l, 0, j)),
        out_shape=jax.ShapeDtypeStruct((depth, bsz, n), F32),
        compiler_params=_cparams(("parallel", "parallel")),
    )(c, w_ada, b_ada.reshape(depth, 1, n))


def _norm_kernel(x_ref, g_ref, sc_ref, sh_ref, o_ref):
    x = x_ref[...]
    ms = jnp.mean(x * x, axis=-1, keepdims=True)
    y = x * lax.rsqrt(ms + EPS) * g_ref[...]
    o_ref[...] = (y * (1.0 + sc_ref[...]) + sh_ref[...]).astype(o_ref.dtype)


def _norm_modulate(x, g, mod):
    bsz, seq, d = x.shape
    tl = 512
    return pl.pallas_call(
        _norm_kernel,
        grid=(bsz, seq // tl),
        in_specs=[
            pl.BlockSpec((None, tl, d), lambda b, i: (b, i, 0)),
            pl.BlockSpec((1, d), lambda b, i: (0, 0)),
            pl.BlockSpec((None, 1, d), lambda b, i: (b, 0, 1)),
            pl.BlockSpec((None, 1, d), lambda b, i: (b, 0, 0)),
        ],
        out_specs=pl.BlockSpec((None, tl, d), lambda b, i: (b, i, 0)),
        out_shape=jax.ShapeDtypeStruct((bsz, seq, d), BF16),
        compiler_params=_cparams(("parallel", "parallel")),
    )(x, g, mod, mod)


def _mm_kernel(a_ref, b_ref, o_ref):
    o_ref[...] = _dg(a_ref[...], b_ref[...]).astype(o_ref.dtype)


def _matmul(a, b, out_dtype, tm, tn):
    m, k = a.shape
    n = b.shape[1]
    return pl.pallas_call(
        _mm_kernel,
        grid=(m // tm, n // tn),
        in_specs=[
            pl.BlockSpec((tm, k), lambda i, j: (i, 0)),
            pl.BlockSpec((k, tn), lambda i, j: (0, j)),
        ],
        out_specs=pl.BlockSpec((tm, tn), lambda i, j: (i, j)),
        out_shape=jax.ShapeDtypeStruct((m, n), out_dtype),
        compiler_params=_cparams(("parallel", "parallel")),
    )(a, b)


def _tri_inverse(a, eye):
    p = a
    x = eye - a
    steps = CHUNK.bit_length() - 2
    for _ in range(steps):
        p = _dot_hp(p, p)
        x = x + _dot_hp(x, p)
    return x


def _gdn_kernel(hp_ref, q_ref, k_ref, v_ref, z_ref, cwq_ref, cwk_ref, cwv_ref, br_ref, ar_ref,
                gn_ref, o_ref, pad_scr, qn_scr, kn_scr, vn_scr, gc_scr, beta_scr, s_scr, *, hb, seq):
    hd = GDN_HEAD_DIM
    nch = seq // CHUNK
    head0 = pl.program_id(1) * hb
    row_tile = min(seq, 256)

    ii = lax.broadcasted_iota(I32, (CHUNK, CHUNK), 0)
    jj = lax.broadcasted_iota(I32, (CHUNK, CHUNK), 1)
    incl = ii >= jj
    strict = ii > jj
    eye = jnp.where(ii == jj, 1.0, 0.0).astype(F32)
    upper = jnp.where(ii <= jj, 1.0, 0.0).astype(BF16)

    def row_to_col(row):
        return jnp.sum(jnp.where(ii == jj, jnp.broadcast_to(row, (CHUNK, CHUNK)), 0.0),
                       axis=1, keepdims=True)

    for hh in range(hb):
        ls = slice(hh * hd, (hh + 1) * hd)
        for src, cw, dst, mode in ((q_ref, cwq_ref, qn_scr, "q"), (k_ref, cwk_ref, kn_scr, "k"),
                                   (v_ref, cwv_ref, vn_scr, "v")):
            pad_scr[0:8, :] = jnp.zeros((8, hd), F32)
            pad_scr[8:8 + seq, :] = src[:, ls].astype(F32)
            for rt in range(seq // row_tile):
                base = rt * row_tile
                acc = jnp.zeros((row_tile, hd), F32)
                for kk in range(GDN_CONV):
                    off = base + 8 - (GDN_CONV - 1) + kk
                    acc = acc + cw[kk:kk + 1, ls] * pad_scr[off:off + row_tile, :]
                y = _silu(acc)
                if mode != "v":
                    y = y * lax.rsqrt(jnp.sum(y * y, axis=-1, keepdims=True) + EPS)
                if mode == "q":
                    y = y * (hd ** -0.5)
                dst[hh, base:base + row_tile, :] = y
        a_log = hp_ref[head0 + hh, 0]
        dt_b = hp_ref[head0 + hh, 1]
        g2 = -jnp.exp(jnp.full((nch, CHUNK), a_log, F32)) * _softplus(ar_ref[hh] + dt_b)
        gc_scr[hh] = _dot_hp_exact_rhs(g2, upper)
        beta_scr[hh] = _sigmoid(br_ref[hh])
        s_scr[hh] = jnp.zeros((hd, hd), F32)

    def chunk_body(n, carry):
        r0 = pl.multiple_of(n * CHUNK, CHUNK)
        for hh in range(hb):
            ls = slice(hh * hd, (hh + 1) * hd)
            q = qn_scr[hh, pl.ds(r0, CHUNK), :]
            k = kn_scr[hh, pl.ds(r0, CHUNK), :]
            v = vn_scr[hh, pl.ds(r0, CHUNK), :]
            gc_row = gc_scr[hh, pl.ds(n, 1), :]
            beta_row = beta_scr[hh, pl.ds(n, 1), :]
            gc_col = row_to_col(gc_row)
            beta_col = row_to_col(beta_row)
            g_last = gc_row[:, CHUNK - 1:CHUNK]
            decay = jnp.where(incl, jnp.exp(jnp.where(incl, gc_col - gc_row, 0.0)), 0.0)
            kk_t = _dot_hp(k, k, NT)
            a_strict = jnp.where(strict, beta_col * kk_t * decay, 0.0)
            intra = jnp.where(incl, _dot_hp(q, k, NT) * decay, 0.0)
            t_inv = _tri_inverse(a_strict, eye)
            e_gc = jnp.exp(gc_col)
            kb = k * beta_col
            u = _dot_hp(t_inv, v * beta_col)
            w = _dot_hp(t_inv, kb * e_gc)
            state = s_scr[hh]
            v_new = u - _dot_hp(w, state)
            o = _dot_hp(q * e_gc, state) + _dot_hp(intra, v_new)
            k_dec = k * jnp.exp(g_last - gc_col)
            s_scr[hh] = state * jnp.exp(g_last) + _dot_hp(k_dec, v_new, TN)
            on = o * lax.rsqrt(jnp.mean(o * o, axis=-1, keepdims=True) + EPS) * gn_ref[...]
            z = z_ref[pl.ds(r0, CHUNK), ls].astype(F32)
            o_ref[pl.ds(r0, CHUNK), ls] = (on * _silu(z)).astype(o_ref.dtype)
        return carry

    lax.fori_loop(0, nch, chunk_body, 0)


def _gdn_mixer(p3, beta_r, alpha_r, conv_w, head_params, gn, hb=2):
    bsz, seq, _ = p3.shape
    wb = hb * GDN_HEAD_DIM
    nhb = GDN_HEADS // hb
    nch = seq // CHUNK
    qb, kb_, vb, zb = (P_QKV_A // wb, (P_QKV_A + GDN_WIDTH) // wb, (P_QKV_A + 2 * GDN_WIDTH) // wb,
                       P_Z_A // wb)
    col = lambda off: pl.BlockSpec((None, seq, wb), lambda b, h, off=off: (b, 0, off + h))
    cwspec = lambda off: pl.BlockSpec((GDN_CONV, wb), lambda b, h, off=off: (0, off + h))
    rows = pl.BlockSpec((None, hb, nch, CHUNK), lambda b, h: (b, h, 0, 0))
    return pl.pallas_call(
        functools.partial(_gdn_kernel, hb=hb, seq=seq),
        grid=(bsz, nhb),
        in_specs=[
            pl.BlockSpec(memory_space=pltpu.SMEM),
            col(qb), col(kb_), col(vb), col(zb),
            cwspec(0), cwspec(nhb), cwspec(2 * nhb),
            rows, rows,
            pl.BlockSpec((1, GDN_HEAD_DIM), lambda b, h: (0, 0)),
        ],
        out_specs=pl.BlockSpec((None, seq, wb), lambda b, h: (b, 0, h)),
        out_shape=jax.ShapeDtypeStruct((bsz, seq, GDN_WIDTH), BF16),
        scratch_shapes=[
            pltpu.VMEM((seq + 8, GDN_HEAD_DIM), F32),
            pltpu.VMEM((hb, seq, GDN_HEAD_DIM), F32),
            pltpu.VMEM((hb, seq, GDN_HEAD_DIM), F32),
            pltpu.VMEM((hb, seq, GDN_HEAD_DIM), F32),
            pltpu.VMEM((hb, nch, CHUNK), F32),
            pltpu.VMEM((hb, nch, CHUNK), F32),
            pltpu.VMEM((hb, GDN_HEAD_DIM, GDN_HEAD_DIM), F32),
        ],
        compiler_params=_cparams(("parallel", "parallel")),
    )(head_params, p3, p3, p3, p3, conv_w, conv_w, conv_w, beta_r, alpha_r, gn)


CONF_HALO = 32
CONF_ROWS = 32


def _conf_kernel(a_ref, b_ref, ap_ref, bp_ref, z_ref, w_ref, cb_ref, lg_ref, lb_ref, o_ref, glu_scr,
                 *, tl):
    i = pl.program_id(1)
    glu_scr[CONF_HALO:CONF_HALO + tl, :] = a_ref[...].astype(F32) * _sigmoid(b_ref[...].astype(F32))
    prev = ap_ref[...].astype(F32) * _sigmoid(bp_ref[...].astype(F32))
    glu_scr[0:CONF_HALO, :] = jnp.where(i > 0, prev, 0.0)
    for rt in range(tl // CONF_ROWS):
        base = rt * CONF_ROWS
        acc = jnp.broadcast_to(cb_ref[...], (CONF_ROWS, CONV_WIDTH))
        for kk in range(CONV_KERNEL):
            off = base + CONF_HALO - (CONV_KERNEL - 1) + kk
            acc = acc + w_ref[kk:kk + 1, :] * glu_scr[off:off + CONF_ROWS, :]
        mu = jnp.mean(acc, axis=-1, keepdims=True)
        cen = acc - mu
        var = jnp.mean(cen * cen, axis=-1, keepdims=True)
        y = cen * lax.rsqrt(var + EPS) * lg_ref[...] + lb_ref[...]
        z = z_ref[base:base + CONF_ROWS, :].astype(F32)
        o_ref[base:base + CONF_ROWS, :] = (_silu(y) * _silu(z)).astype(o_ref.dtype)


def _conformer_mixer(p3, conv_w, conv_b, ln_g, ln_b):
    bsz, seq, _ = p3.shape
    tl = 256
    cw = CONV_WIDTH
    ca, cb, cz = P_GLU // cw, P_GLU // cw + 1, P_Z_B // cw
    halo_per_tile = tl // CONF_HALO
    cur = lambda c: pl.BlockSpec((None, tl, cw), lambda b, i, c=c: (b, i, c))
    prev = lambda c: pl.BlockSpec(
        (None, CONF_HALO, cw), lambda b, i, c=c: (b, jnp.maximum(i * halo_per_tile - 1, 0), c))
    vec = pl.BlockSpec((1, cw), lambda b, i: (0, 0))
    return pl.pallas_call(
        functools.partial(_conf_kernel, tl=tl),
        grid=(bsz, seq // tl),
        in_specs=[cur(ca), cur(cb), prev(ca), prev(cb), cur(cz),
                  pl.BlockSpec((CONV_KERNEL, cw), lambda b, i: (0, 0)), vec, vec, vec],
        out_specs=pl.BlockSpec((None, tl, cw), lambda b, i: (b, i, 0)),
        out_shape=jax.ShapeDtypeStruct((bsz, seq, cw), BF16),
        scratch_shapes=[pltpu.VMEM((CONF_HALO + tl, cw), F32)],
        compiler_params=_cparams(("parallel", "parallel")),
    )(p3, p3, p3, p3, p3, conv_w, conv_b, ln_g, ln_b)


def _t5_bucket_int(rel):
    nb = REL_BUCKETS // 2
    max_exact = nb // 2
    n = jnp.abs(rel)
    large = jnp.full(rel.shape, max_exact, I32)
    for thr in (12, 16, 23, 32, 46, 64, 91):
        large = large + jnp.where(n >= thr, 1, 0)
    large = jnp.minimum(large, nb - 1)
    return jnp.where(rel > 0, nb, 0) + jnp.where(n < max_exact, n, large)


def _bias_kernel(rb_ref, o_ref):
    blk = DSA_BLOCK
    r = lax.broadcasted_iota(I32, (blk, blk), 0)
    c = lax.broadcasted_iota(I32, (blk, blk), 1)
    for kind, shift in ((0, -2 * blk), (1, -blk), (2, 0)):
        bucket = _t5_bucket_int(c - r + shift)
        for h in range(ATT_HEADS):
            val = jnp.zeros((blk, blk), F32)
            for b in range(REL_BUCKETS):
                val = jnp.where(bucket == b, rb_ref[b, h], val)
            o_ref[kind, h] = val


def _bias_tiles(rel_bias):
    return pl.pallas_call(
        _bias_kernel,
        in_specs=[pl.BlockSpec(memory_space=pltpu.SMEM)],
        out_shape=jax.ShapeDtypeStruct((3, ATT_HEADS, DSA_BLOCK, DSA_BLOCK), F32),
    )(rel_bias)


def _dsa_kernel(q3_ref, k3_ref, wi_ref, qc_ref, zc_ref, kc_ref, vc_ref, qg_ref, kg_ref, tab_ref,
                o_ref, kp_scr, kn_scr, qp_scr, key_scr, sel_scr, qs_scr, m_scr, l_scr, acc_scr,
                *, seq, topk):
    blk = DSA_BLOCK
    hd = ATT_HEAD_DIM
    i = pl.program_id(1)
    nkv = i + 1
    kf = float(topk)

    @pl.when(i == 0)
    def _prepare_keys():
        rt = min(seq, 256)
        for t in range(seq // rt):
            rs = slice(t * rt, (t + 1) * rt)
            k3 = k3_ref[rs, :]
            lane = lax.broadcasted_iota(I32, k3.shape, 1)
            hi = k3.astype(BF16)
            lo = (k3 - hi.astype(F32)).astype(BF16)
            kp_scr[rs, :] = jnp.where(lane >= 2 * IDX_HEAD_DIM, lo, hi)
            kc = kc_ref[rs, :].astype(F32)
            for g in range(ATT_KV_HEADS):
                ls = slice(g * hd, (g + 1) * hd)
                kg = kc[:, ls]
                kgn = kg * lax.rsqrt(jnp.mean(kg * kg, axis=-1, keepdims=True) + EPS) * kg_ref[...]
                kn_scr[rs, ls] = kgn.astype(BF16)

    q3 = q3_ref[...]
    lane = lax.broadcasted_iota(I32, q3.shape, 1) & (IDX_PAD - 1)
    hi = q3.astype(BF16)
    lo = (q3 - hi.astype(F32)).astype(BF16)
    qp = jnp.where((lane >= IDX_HEAD_DIM) & (lane < 2 * IDX_HEAD_DIM), lo, hi)
    for h in range(IDX_HEADS):
        qp_scr[h * blk:(h + 1) * blk, :] = qp[:, h * IDX_PAD:(h + 1) * IDX_PAD]
    wsc = wi_ref[...] * ((IDX_HEAD_DIM ** -0.5) * (IDX_HEADS ** -0.5))

    row = lax.broadcasted_iota(I32, (blk, blk), 0)
    col = lax.broadcasted_iota(I32, (blk, blk), 1)
    chunk_gap = (col // CHUNK) - (row // CHUNK)

    def score_body(j, carry):
        kb = kp_scr[pl.ds(pl.multiple_of(j * blk, blk), blk), :]
        s_all = _dg(qp_scr[...], kb, NT)
        sc = jnp.zeros((blk, blk), F32)
        for h in range(IDX_HEADS):
            sc = sc + wsc[:, h:h + 1] * jnp.maximum(s_all[h * blk:(h + 1) * blk, :], 0.0)
        sc = sc + 0.0
        bits = pltpu.bitcast(sc, I32)
        key = jnp.where(bits < 0, bits ^ jnp.int32(0x7FFFFFFF), bits)
        slack = jnp.where(j < i, blk, 0)
        key = jnp.where(chunk_gap <= slack, key, jnp.int32(INT_MIN))
        key_scr[j] = key
        return carry

    lax.fori_loop(0, nkv, score_body, 0)

    def count_where(pred):
        def body(j, acc):
            return acc + jnp.where(pred(key_scr[j]), 1.0, 0.0)
        acc = lax.fori_loop(0, nkv, body, jnp.zeros((blk, blk), F32))
        return jnp.sum(acc, axis=1, keepdims=True)

    cnt0 = count_where(lambda kv: kv >= 0)
    cur0 = jnp.where(cnt0 >= kf, jnp.int32(0), jnp.int32(INT_MIN))

    def bit_body(t, cur):
        cand = cur | lax.shift_left(jnp.int32(1), jnp.int32(30) - t)
        cnt = count_where(lambda kv: kv >= cand)
        return jnp.where(cnt >= kf, cand, cur)

    thr = lax.fori_loop(0, 31, bit_body, cur0)
    need = kf - count_where(lambda kv: kv > thr)

    upper = jnp.where(row <= col, 1.0, 0.0).astype(BF16)

    def sel_body(j, run):
        key = key_scr[j]
        eq = key == thr
        pre = _dg(jnp.where(eq, 1.0, 0.0).astype(BF16), upper)
        take_tie = jnp.where(eq & (run + pre <= need), 1.0, 0.0)
        take = jnp.where(key > thr, 1.0, take_tie)
        sel_scr[j] = jnp.where(key == jnp.int32(INT_MIN), 0.0, take)
        return run + pre[:, blk - 1:blk]

    lax.fori_loop(0, nkv, sel_body, jnp.zeros((blk, 1), F32))

    qc = qc_ref[...].astype(F32)
    for h in range(ATT_HEADS):
        qh = qc[:, h * hd:(h + 1) * hd]
        qn = qh * lax.rsqrt(jnp.mean(qh * qh, axis=-1, keepdims=True) + EPS) * qg_ref[...]
        g, hh = divmod(h, ATT_GROUP)
        qs_scr[g, hh * blk:(hh + 1) * blk, :] = (qn * (hd ** -0.5)).astype(BF16)
    m_scr[...] = jnp.full(m_scr.shape, NEG_BIG, F32)
    l_scr[...] = jnp.zeros(l_scr.shape, F32)
    acc_scr[...] = jnp.zeros(acc_scr.shape, F32)

    def att_body(j, carry):
        r0 = pl.multiple_of(j * blk, blk)
        kk = kn_scr[pl.ds(r0, blk), :]
        vv = vc_ref[pl.ds(r0, blk), :]
        sel1 = sel_scr[j]
        sel4 = jnp.concatenate([sel1] * ATT_GROUP, axis=0) > 0.5
        kind = jnp.clip(j - i + 2, 0, 2)
        for g in range(ATT_KV_HEADS):
            ls = slice(g * hd, (g + 1) * hd)
            bias = jnp.concatenate([tab_ref[kind, g * ATT_GROUP + hh] for hh in range(ATT_GROUP)],
                                   axis=0)
            lg = _dg(qs_scr[g], kk[:, ls], NT) + bias
            m_old = m_scr[g]
            m_new = jnp.maximum(m_old, jnp.max(jnp.where(sel4, lg, NEG_BIG), axis=1, keepdims=True))
            p = jnp.where(sel4, jnp.exp(lg - m_new), 0.0)
            alpha = jnp.exp(m_old - m_new)
            l_scr[g] = alpha * l_scr[g] + jnp.sum(p, axis=1, keepdims=True)
            acc_scr[g] = alpha * acc_scr[g] + _dg(p.astype(BF16), vv[:, ls])
            m_scr[g] = m_new
        return carry

    lax.fori_loop(0, nkv, att_body, 0)

    for h in range(ATT_HEADS):
        g, hh = divmod(h, ATT_GROUP)
        rs = slice(hh * blk, (hh + 1) * blk)
        ls = slice(h * hd, (h + 1) * hd)
        o = acc_scr[g, rs, :] / l_scr[g, rs, :]
        z = zc_ref[:, ls].astype(F32)
        o_ref[:, ls] = (o * _silu(z)).astype(o_ref.dtype)


def _dsa_mixer(p3, q3, k3, w_i, q_gain, k_gain, bias_tab):
    bsz, seq, _ = p3.shape
    blk = DSA_BLOCK
    nblk = seq // blk
    topk = min(TOPK_MAX, seq // 4)
    return pl.pallas_call(
        functools.partial(_dsa_kernel, seq=seq, topk=topk),
        grid=(bsz, nblk),
        in_specs=[
            pl.BlockSpec((None, blk, IDX_HEADS * IDX_PAD), lambda b, i: (b, i, 0)),
            pl.BlockSpec((None, seq, IDX_PAD), lambda b, i: (b, 0, 0)),
            pl.BlockSpec((None, blk, IDX_HEADS), lambda b, i: (b, i, 0)),
            pl.BlockSpec((None, blk, ATT_WIDTH), lambda b, i: (b, i, P_Q_C // ATT_WIDTH)),
            pl.BlockSpec((None, blk, ATT_WIDTH), lambda b, i: (b, i, P_Z_C // ATT_WIDTH)),
            pl.BlockSpec((None, seq, ATT_KV_WIDTH), lambda b, i: (b, 0, P_K_C // ATT_KV_WIDTH)),
            pl.BlockSpec((None, seq, ATT_KV_WIDTH), lambda b, i: (b, 0, P_V_C // ATT_KV_WIDTH)),
            pl.BlockSpec((1, ATT_HEAD_DIM), lambda b, i: (0, 0)),
            pl.BlockSpec((1, ATT_HEAD_DIM), lambda b, i: (0, 0)),
            pl.BlockSpec((3, ATT_HEADS, blk, blk), lambda b, i: (0, 0, 0, 0)),
        ],
        out_specs=pl.BlockSpec((None, blk, ATT_WIDTH), lambda b, i: (b, i, 0)),
        out_shape=jax.ShapeDtypeStruct((bsz, seq, ATT_WIDTH), BF16),
        scratch_shapes=[
            pltpu.VMEM((seq, IDX_PAD), BF16),
            pltpu.VMEM((seq, ATT_KV_WIDTH), BF16),
            pltpu.VMEM((IDX_HEADS * blk, IDX_PAD), BF16),
            pltpu.VMEM((nblk, blk, blk), I32),
            pltpu.VMEM((nblk, blk, blk), F32),
            pltpu.VMEM((ATT_KV_HEADS, ATT_GROUP * blk, ATT_HEAD_DIM), BF16),
            pltpu.VMEM((ATT_KV_HEADS, ATT_GROUP * blk, 1), F32),
            pltpu.VMEM((ATT_KV_HEADS, ATT_GROUP * blk, 1), F32),
            pltpu.VMEM((ATT_KV_HEADS, ATT_GROUP * blk, ATT_HEAD_DIM), F32),
        ],
        compiler_params=_cparams(("arbitrary", "arbitrary")),
    )(q3, k3, w_i, p3, p3, p3, p3, q_gain, k_gain, bias_tab)


def _merge_kernel(ya_ref, yb_ref, yc_ref, wa_ref, wb_ref, wc_ref, ga_ref, gb_ref, gc_ref,
                  ba_ref, bb_ref, bc_ref, o_ref):
    acc = _sigmoid(ga_ref[...].astype(F32) + ba_ref[...]) * _dg(ya_ref[...], wa_ref[...])
    acc = acc + _sigmoid(gb_ref[...].astype(F32) + bb_ref[...]) * _dg(yb_ref[...], wb_ref[...])
    acc = acc + _sigmoid(gc_ref[...].astype(F32) + bc_ref[...]) * _dg(yc_ref[...], wc_ref[...])
    o_ref[...] = acc.astype(o_ref.dtype)


def _merge(ya, yb, yc, wa, wb, wc, p2, b_gate):
    t = ya.shape[0]
    d = D_MODEL
    tm, tn = 512, 512
    nj = d // tn
    g0 = P_GATE // tn
    yspec = pl.BlockSpec((tm, ya.shape[1]), lambda i, j: (i, 0))
    wspec = pl.BlockSpec((wa.shape[0], tn), lambda i, j: (0, j))
    gspec = lambda br: pl.BlockSpec((tm, tn), lambda i, j, br=br: (i, g0 + br * nj + j))
    bspec = lambda br: pl.BlockSpec((1, tn), lambda i, j, br=br: (0, br * nj + j))
    return pl.pallas_call(
        _merge_kernel,
        grid=(t // tm, nj),
        in_specs=[yspec, yspec, yspec, wspec, wspec, wspec, gspec(0), gspec(1), gspec(2),
                  bspec(0), bspec(1), bspec(2)],
        out_specs=pl.BlockSpec((tm, tn), lambda i, j: (i, j)),
        out_shape=jax.ShapeDtypeStruct((t, d), BF16),
        compiler_params=_cparams(("parallel", "parallel")),
    )(ya, yb, yc, wa, wb, wc, p2, p2, p2, b_gate, b_gate, b_gate)


def _out_kernel(m_ref, w_ref, x_ref, g_ref, o_ref):
    o_ref[...] = x_ref[...] + g_ref[...] * _dg(m_ref[...], w_ref[...])


def _out_proj(merged, w_out, x2, mod, seq):
    t, d = x2.shape
    tm, tn = min(seq, 1024), 512
    per_batch = seq // tm
    nj = d // tn
    return pl.pallas_call(
        _out_kernel,
        grid=(t // tm, nj),
        in_specs=[
            pl.BlockSpec((tm, d), lambda i, j: (i, 0)),
            pl.BlockSpec((d, tn), lambda i, j: (0, j)),
            pl.BlockSpec((tm, tn), lambda i, j: (i, j)),
            pl.BlockSpec((None, 1, tn), lambda i, j: (i // per_batch, 0, 2 * nj + j)),
        ],
        out_specs=pl.BlockSpec((tm, tn), lambda i, j: (i, j)),
        out_shape=jax.ShapeDtypeStruct((t, d), F32),
        compiler_params=_cparams(("parallel", "parallel")),
    )(merged, w_out, x2, mod)


def _wide_weight(w):
    cols = [(_OFF_QKV_A, 3 * GDN_WIDTH), (_OFF_Z_A, GDN_WIDTH), (_OFF_GLU, 2 * CONV_WIDTH),
            (_OFF_Z_B, CONV_WIDTH), (_OFF_Q_C, ATT_WIDTH), (_OFF_Z_C, ATT_WIDTH),
            (_OFF_GATE, N_BRANCH * D_MODEL), (_OFF_K_C, ATT_KV_WIDTH), (_OFF_V_C, ATT_KV_WIDTH)]
    return jnp.concatenate([w[:, o:o + n] for o, n in cols], axis=1).astype(BF16)


def _narrow_weight(w):
    cols = [(_OFF_Q_I, IDX_HEADS * IDX_HEAD_DIM), (_OFF_K_I, IDX_HEAD_DIM), (_OFF_W_I, IDX_HEADS),
            (_OFF_BETA, GDN_HEADS), (_OFF_ALPHA, GDN_HEADS)]
    used = sum(n for _, n in cols)
    parts = [w[:, o:o + n] for o, n in cols] + [jnp.zeros((w.shape[0], NS_NARROW - used), w.dtype)]
    return jnp.concatenate(parts, axis=1).astype(BF16)


def _expand3(x, width):
    pad = jnp.zeros(x.shape[:-1] + (IDX_PAD - 3 * width,), x.dtype)
    y = jnp.concatenate([x, x, x, pad], axis=-1)
    return y.reshape(y.shape[:-2] + (-1,))


def kernel(x, c, rel_bias, w_ada, b_ada, norm_g, w_in, b_gate, conv_a_w, a_log, dt_bias, gdn_norm_g,
           w_proj_a, conv_b_w, conv_b_b, ln_b_g, ln_b_b, w_proj_b, q_norm_g, k_norm_g, w_proj_c, w_out):
    bsz, seq, d = x.shape
    t = bsz * seq
    nch = seq // CHUNK
    mod_all = _ada_modulation(c, w_ada, b_ada)
    bias_tab = _bias_tiles(rel_bias)
    for layer in range(DEPTH):
        mod = mod_all[layer].reshape(bsz, 1, 3 * d)
        h = _norm_modulate(x, norm_g[layer].reshape(1, d), mod).reshape(t, d)
        p2 = _matmul(h, _wide_weight(w_in[layer]), BF16, min(t, 1024), 512)
        s2 = _matmul(h, _narrow_weight(w_in[layer]), F32, min(t, 1024), NS_NARROW)
        p3 = p2.reshape(bsz, seq, NP_WIDE)
        s3 = s2.reshape(bsz, seq, NS_NARROW)

        def head_rows(off):
            r = s3[:, :, off:off + GDN_HEADS]
            return jnp.swapaxes(r, 1, 2).reshape(bsz, GDN_HEADS, nch, CHUNK)

        head_params = jnp.stack([a_log[layer], dt_bias[layer]], axis=1)
        y_a = _gdn_mixer(p3, head_rows(S_BETA), head_rows(S_ALPHA), conv_a_w[layer], head_params,
                         gdn_norm_g[layer].reshape(1, GDN_HEAD_DIM))
        y_b = _conformer_mixer(p3, conv_b_w[layer], conv_b_b[layer].reshape(1, CONV_WIDTH),
                               ln_b_g[layer].reshape(1, CONV_WIDTH), ln_b_b[layer].reshape(1, CONV_WIDTH))
        q3 = _expand3(s3[:, :, S_Q_I:S_K_I].reshape(bsz, seq, IDX_HEADS, IDX_HEAD_DIM), IDX_HEAD_DIM)
        k3 = _expand3(s3[:, :, S_K_I:S_W_I].reshape(bsz, seq, 1, IDX_HEAD_DIM), IDX_HEAD_DIM)
        y_c = _dsa_mixer(p3, q3, k3, s3[:, :, S_W_I:S_W_I + IDX_HEADS],
                         q_norm_g[layer].reshape(1, ATT_HEAD_DIM), k_norm_g[layer].reshape(1, ATT_HEAD_DIM),
                         bias_tab)
        merged = _merge(y_a.reshape(t, GDN_WIDTH), y_b.reshape(t, CONV_WIDTH), y_c.reshape(t, ATT_WIDTH),
                        w_proj_a[layer].astype(BF16), w_proj_b[layer].astype(BF16),
                        w_proj_c[layer].astype(BF16), p2, b_gate[layer].reshape(1, N_BRANCH * d))
        x = _out_proj(merged, w_out[layer].astype(BF16), x.reshape(t, d), mod, seq).reshape(bsz, seq, d)
    return x
```

```python
import functools

import jax
import jax.numpy as jnp
from jax import lax
from jax.experimental import pallas as pl
from jax.experimental.pallas import tpu as pltpu

F32 = jnp.float32
BF16 = jnp.bfloat16
I32 = jnp.int32

D_MODEL = 2048
DEPTH = 4
CHUNK = 64
GDN_HEADS = 8
GDN_HEAD_DIM = 128
GDN_WIDTH = GDN_HEADS * GDN_HEAD_DIM
GDN_CONV = 4
CONV_WIDTH = 1024
CONV_KERNEL = 31
ATT_HEADS = 8
ATT_KV_HEADS = 2
ATT_HEAD_DIM = 128
ATT_WIDTH = ATT_HEADS * ATT_HEAD_DIM
ATT_KV_WIDTH = ATT_KV_HEADS * ATT_HEAD_DIM
ATT_GROUP = ATT_HEADS // ATT_KV_HEADS
IDX_HEADS = 8
IDX_HEAD_DIM = 64
TOPK_MAX = 256
REL_BUCKETS = 32
N_BRANCH = 3
EPS = 1e-6

_OFF_QKV_A = 0
_OFF_Z_A = _OFF_QKV_A + 3 * GDN_WIDTH
_OFF_BETA = _OFF_Z_A + GDN_WIDTH
_OFF_ALPHA = _OFF_BETA + GDN_HEADS
_OFF_GLU = _OFF_ALPHA + GDN_HEADS
_OFF_Z_B = _OFF_GLU + 2 * CONV_WIDTH
_OFF_Q_C = _OFF_Z_B + CONV_WIDTH
_OFF_K_C = _OFF_Q_C + ATT_WIDTH
_OFF_V_C = _OFF_K_C + ATT_KV_WIDTH
_OFF_Z_C = _OFF_V_C + ATT_KV_WIDTH
_OFF_Q_I = _OFF_Z_C + ATT_WIDTH
_OFF_K_I = _OFF_Q_I + IDX_HEADS * IDX_HEAD_DIM
_OFF_W_I = _OFF_K_I + IDX_HEAD_DIM
_OFF_GATE = _OFF_W_I + IDX_HEADS
N_IN = _OFF_GATE + N_BRANCH * D_MODEL

P_QKV_A = 0
P_Z_A = 3072
P_GLU = 4096
P_Z_B = 6144
P_Q_C = 7168
P_Z_C = 8192
P_GATE = 9216
P_K_C = 15360
P_V_C = 15616
NP_WIDE = 15872
S_Q_I = 0
S_K_I = 512
S_W_I = 576
S_BETA = 584
S_ALPHA = 592
NS_NARROW = 640

LANES = 128
DSA_BLOCK = 128
DSA_KBLOCK = 256
DSA_BIAS_KINDS = DSA_KBLOCK // DSA_BLOCK + 2
IDX_PAD = 256
VMEM_LIMIT = 56 * 1024 * 1024
NEG_BIG = -1e30
INT_MIN = -(2 ** 31)

NN = (((1,), (0,)), ((), ()))
NT = (((1,), (1,)), ((), ()))
TN = (((0,), (0,)), ((), ()))


def _dg(a, b, dims=NN):
    return lax.dot_general(a, b, dims, preferred_element_type=F32)


def _split2(x):
    hi = x.astype(BF16)
    lo = (x - hi.astype(F32)).astype(BF16)
    return hi, lo


def _dot_b(a, b, dims=NN):
    return _dg(a.astype(BF16), b.astype(BF16), dims)


def _dot_hp(a, b, dims=NN):
    ah, al = _split2(a)
    bh, bl = _split2(b)
    return _dg(ah, bh, dims) + (_dg(al, bh, dims) + _dg(ah, bl, dims))


def _dot_hp_exact_rhs(a, b_bf16, dims=NN):
    a0 = a.astype(BF16)
    r = a - a0.astype(F32)
    a1 = r.astype(BF16)
    a2 = (r - a1.astype(F32)).astype(BF16)
    return _dg(a0, b_bf16, dims) + (_dg(a1, b_bf16, dims) + _dg(a2, b_bf16, dims))


def _sigmoid(x):
    return 1.0 / (1.0 + jnp.exp(-x))


def _silu(x):
    return x * _sigmoid(x)


def _softplus(x):
    return jnp.maximum(x, 0.0) + jnp.log(1.0 + jnp.exp(-jnp.abs(x)))


def _cparams(sem):
    return pltpu.CompilerParams(dimension_semantics=sem, vmem_limit_bytes=VMEM_LIMIT)


def _ada_kernel(c_ref, w_ref, b_ref, o_ref):
    cond = _silu(c_ref[...])
    o_ref[...] = _dot_hp(cond, w_ref[...]) + b_ref[...]


def _ada_modulation(c, w_ada, b_ada):
    depth, d, n = w_ada.shape
    bsz = c.shape[0]
    tn = 512
    return pl.pallas_call(
        _ada_kernel,
        grid=(depth, n // tn),
        in_specs=[
            pl.BlockSpec((bsz, d), lambda l, j: (0, 0)),
            pl.BlockSpec((None, d, tn), lambda l, j: (l, 0, j)),
            pl.BlockSpec((None, 1, tn), lambda l, j: (l, 0, j)),
        ],
        out_specs=pl.BlockSpec((None, bsz, tn), lambda l, j: (l, 0, j)),
        out_shape=jax.ShapeDtypeStruct((depth, bsz, n), F32),
        compiler_params=_cparams(("parallel", "parallel")),
    )(c, w_ada, b_ada.reshape(depth, 1, n))


def _norm_kernel(x_ref, g_ref, sc_ref, sh_ref, o_ref):
    x = x_ref[...]
    ms = jnp.mean(x * x, axis=-1, keepdims=True)
    y = x * lax.rsqrt(ms + EPS) * g_ref[...]
    o_ref[...] = (y * (1.0 + sc_ref[...]) + sh_ref[...]).astype(o_ref.dtype)


def _norm_modulate(x, g, mod):
    bsz, seq, d = x.shape
    tl = 512
    return pl.pallas_call(
        _norm_kernel,
        grid=(bsz, seq // tl),
        in_specs=[
            pl.BlockSpec((None, tl, d), lambda b, i: (b, i, 0)),
            pl.BlockSpec((1, d), lambda b, i: (0, 0)),
            pl.BlockSpec((None, 1, d), lambda b, i: (b, 0, 1)),
            pl.BlockSpec((None, 1, d), lambda b, i: (b, 0, 0)),
        ],
        out_specs=pl.BlockSpec((None, tl, d), lambda b, i: (b, i, 0)),
        out_shape=jax.ShapeDtypeStruct((bsz, seq, d), BF16),
        compiler_params=_cparams(("parallel", "parallel")),
    )(x, g, mod, mod)


def _mm_kernel(a_ref, b_ref, o_ref):
    o_ref[...] = _dg(a_ref[...], b_ref[...]).astype(o_ref.dtype)


def _matmul(a, b, out_dtype, tm, tn):
    m, k = a.shape
    n = b.shape[1]
    return pl.pallas_call(
        _mm_kernel,
        grid=(m // tm, n // tn),
        in_specs=[
            pl.BlockSpec((tm, k), lambda i, j: (i, 0)),
            pl.BlockSpec((k, tn), lambda i, j: (0, j)),
        ],
        out_specs=pl.BlockSpec((tm, tn), lambda i, j: (i, j)),
        out_shape=jax.ShapeDtypeStruct((m, n), out_dtype),
        compiler_params=_cparams(("parallel", "parallel")),
    )(a, b)


def _tri_inverse(mats, eye):
    ps = list(mats)
    xs = [eye - a for a in mats]
    steps = CHUNK.bit_length() - 2
    for _ in range(steps):
        ps = [_dot_b(p, p) for p in ps]
        xs = [x + _dot_b(x, p) for x, p in zip(xs, ps)]
    return xs


GDN_UNROLL = 4


def _gdn_kernel(hp_ref, q_ref, k_ref, v_ref, z_ref, cwq_ref, cwk_ref, cwv_ref, br_ref, ar_ref,
                gn_ref, o_ref, pad_scr, qn_scr, kn_scr, vn_scr, gc_scr, beta_scr,
                wq_scr, kdt_scr, intra_scr, *, hb, seq):
    hd = GDN_HEAD_DIM
    nch = seq // CHUNK
    head0 = pl.program_id(1) * hb
    row_tile = min(seq, 256)
    unroll = min(GDN_UNROLL, nch)

    ii = lax.broadcasted_iota(I32, (CHUNK, CHUNK), 0)
    jj = lax.broadcasted_iota(I32, (CHUNK, CHUNK), 1)
    incl = ii >= jj
    strict = ii > jj
    eye = jnp.where(ii == jj, 1.0, 0.0).astype(F32)
    upper = jnp.where(ii <= jj, 1.0, 0.0).astype(BF16)

    def row_to_col(row):
        return jnp.sum(jnp.where(ii == jj, jnp.broadcast_to(row, (CHUNK, CHUNK)), 0.0),
                       axis=1, keepdims=True)

    for hh in range(hb):
        ls = slice(hh * hd, (hh + 1) * hd)
        for src, cw, dst, mode in ((q_ref, cwq_ref, qn_scr, "q"), (k_ref, cwk_ref, kn_scr, "k"),
                                   (v_ref, cwv_ref, vn_scr, "v")):
            pad_scr[0:8, :] = jnp.zeros((8, hd), F32)
            pad_scr[8:8 + seq, :] = src[:, ls].astype(F32)
            for rt in range(seq // row_tile):
                base = rt * row_tile
                acc = jnp.zeros((row_tile, hd), F32)
                for kk in range(GDN_CONV):
                    off = base + 8 - (GDN_CONV - 1) + kk
                    acc = acc + cw[kk:kk + 1, ls] * pad_scr[off:off + row_tile, :]
                y = _silu(acc)
                if mode != "v":
                    y = y * lax.rsqrt(jnp.sum(y * y, axis=-1, keepdims=True) + EPS)
                if mode == "q":
                    y = y * (hd ** -0.5)
                dst[hh, base:base + row_tile, :] = y
        a_log = hp_ref[head0 + hh, 0]
        dt_b = hp_ref[head0 + hh, 1]
        g2 = -jnp.exp(jnp.full((nch, CHUNK), a_log, F32)) * _softplus(ar_ref[hh] + dt_b)
        gc_scr[hh] = _dot_hp_exact_rhs(g2, upper)
        beta_scr[hh] = _sigmoid(br_ref[hh])

    i_hd = lax.broadcasted_iota(I32, (hd, hd), 0)
    j_hd = lax.broadcasted_iota(I32, (hd, hd), 1)
    eye_hd = jnp.where(i_hd == j_hd, 1.0, 0.0).astype(BF16)

    def load_chunk(n, hh):
        r0 = pl.multiple_of(n * CHUNK, CHUNK)
        return (qn_scr[hh, pl.ds(r0, CHUNK), :], kn_scr[hh, pl.ds(r0, CHUNK), :],
                vn_scr[hh, pl.ds(r0, CHUNK), :], gc_scr[hh, pl.ds(n, 1), :],
                beta_scr[hh, pl.ds(n, 1), :])

    def compute_chunks(loaded):
        qs, ks, vs, gc_rows, beta_rows = zip(*loaded)
        gc_cols = [row_to_col(r) for r in gc_rows]
        beta_cols = [row_to_col(r) for r in beta_rows]
        decays = [jnp.where(incl, jnp.exp(jnp.where(incl, c - r, 0.0)), 0.0)
                  for c, r in zip(gc_cols, gc_rows)]
        k16s = [k.astype(BF16) for k in ks]
        qk_kks = [_dg(jnp.concatenate([q.astype(BF16), k16], axis=0), k16, NT)
                  for q, k16 in zip(qs, k16s)]
        intras = [jnp.where(incl, m[:CHUNK] * d, 0.0).astype(BF16) for m, d in zip(qk_kks, decays)]
        t_invs = _tri_inverse([jnp.where(strict, b * m[CHUNK:] * d, 0.0)
                               for b, m, d in zip(beta_cols, qk_kks, decays)], eye)
        e_gcs = [jnp.exp(c) for c in gc_cols]
        uws = [_dot_b(t, jnp.concatenate([v * b, k * (b * e)], axis=1))
               for t, v, k, b, e in zip(t_invs, vs, ks, beta_cols, e_gcs)]
        wqs = [jnp.concatenate([uw[:, hd:], q * e], axis=0).astype(BF16)
               for uw, q, e in zip(uws, qs, e_gcs)]
        kdts = [_dg(eye_hd, (k * jnp.exp(r[:, CHUNK - 1:CHUNK] - c)).astype(BF16), NT).astype(BF16)
                for k, r, c in zip(ks, gc_rows, gc_cols)]
        return [(uw[:, :hd], wq, kdt, intra) for uw, wq, kdt, intra in zip(uws, wqs, kdts, intras)]

    def store_chunk(n, hh, u, wq, kdt, intra):
        vn_scr[hh, pl.ds(pl.multiple_of(n * CHUNK, CHUNK), CHUNK), :] = u
        wq_scr[hh, n] = wq
        kdt_scr[hh, n] = kdt
        intra_scr[hh, n] = intra

    def prepare_body(step, carry):
        jobs = [(step * unroll + cu, hh) for cu in range(unroll) for hh in range(hb)]
        results = compute_chunks([load_chunk(n, hh) for n, hh in jobs])
        for (n, hh), res in zip(jobs, results):
            store_chunk(n, hh, *res)
        return carry

    lax.fori_loop(0, nch // unroll, prepare_body, 0)

    def scan_body(n, states):
        r0 = pl.multiple_of(n * CHUNK, CHUNK)
        loaded = [(wq_scr[hh, n], vn_scr[hh, pl.ds(r0, CHUNK), :], intra_scr[hh, n], kdt_scr[hh, n],
                   gc_scr[hh, pl.ds(n, 1), CHUNK - 1:CHUNK]) for hh in range(hb)]
        wqs, us, intras, kdts, g_lasts = zip(*loaded)
        ws_qs = [_dg(wq, s.astype(BF16)) for wq, s in zip(wqs, states)]
        v_news = [(u - m[:CHUNK]).astype(BF16) for u, m in zip(us, ws_qs)]
        new_states = [s * jnp.exp(g) + _dg(kdt, v) for s, g, kdt, v in zip(states, g_lasts, kdts, v_news)]
        outs = [m[CHUNK:] + _dg(intra, v) for m, intra, v in zip(ws_qs, intras, v_news)]
        for hh in range(hb):
            qn_scr[hh, pl.ds(r0, CHUNK), :] = outs[hh]
        return tuple(new_states)

    lax.fori_loop(0, nch, scan_body, tuple(jnp.zeros((hd, hd), F32) for _ in range(hb)))

    for hh in range(hb):
        ls = slice(hh * hd, (hh + 1) * hd)
        for rt in range(seq // row_tile):
            rs = slice(rt * row_tile, (rt + 1) * row_tile)
            o = qn_scr[hh, rs, :]
            on = o * lax.rsqrt(jnp.mean(o * o, axis=-1, keepdims=True) + EPS) * gn_ref[...]
            o_ref[rs, ls] = (on * _silu(z_ref[rs, ls].astype(F32))).astype(o_ref.dtype)


def _gdn_mixer(p3, beta_r, alpha_r, conv_w, head_params, gn, hb=2):
    bsz, seq, _ = p3.shape
    wb = hb * GDN_HEAD_DIM
    nhb = GDN_HEADS // hb
    nch = seq // CHUNK
    qb, kb_, vb, zb = (P_QKV_A // wb, (P_QKV_A + GDN_WIDTH) // wb, (P_QKV_A + 2 * GDN_WIDTH) // wb,
                       P_Z_A // wb)
    col = lambda off: pl.BlockSpec((None, seq, wb), lambda b, h, off=off: (b, 0, off + h))
    cwspec = lambda off: pl.BlockSpec((GDN_CONV, wb), lambda b, h, off=off: (0, off + h))
    rows = pl.BlockSpec((None, hb, nch, CHUNK), lambda b, h: (b, h, 0, 0))
    return pl.pallas_call(
        functools.partial(_gdn_kernel, hb=hb, seq=seq),
        grid=(bsz, nhb),
        in_specs=[
            pl.BlockSpec(memory_space=pltpu.SMEM),
            col(qb), col(kb_), col(vb), col(zb),
            cwspec(0), cwspec(nhb), cwspec(2 * nhb),
            rows, rows,
            pl.BlockSpec((1, GDN_HEAD_DIM), lambda b, h: (0, 0)),
        ],
        out_specs=pl.BlockSpec((None, seq, wb), lambda b, h: (b, 0, h)),
        out_shape=jax.ShapeDtypeStruct((bsz, seq, GDN_WIDTH), BF16),
        scratch_shapes=[
            pltpu.VMEM((seq + 8, GDN_HEAD_DIM), F32),
            pltpu.VMEM((hb, seq, GDN_HEAD_DIM), F32),
            pltpu.VMEM((hb, seq, GDN_HEAD_DIM), F32),
            pltpu.VMEM((hb, seq, GDN_HEAD_DIM), F32),
            pltpu.VMEM((hb, nch, CHUNK), F32),
            pltpu.VMEM((hb, nch, CHUNK), F32),
            pltpu.VMEM((hb, nch, 2 * CHUNK, GDN_HEAD_DIM), BF16),
            pltpu.VMEM((hb, nch, GDN_HEAD_DIM, CHUNK), BF16),
            pltpu.VMEM((hb, nch, CHUNK, CHUNK), BF16),
        ],
        compiler_params=_cparams(("parallel", "parallel")),
    )(head_params, p3, p3, p3, p3, conv_w, conv_w, conv_w, beta_r, alpha_r, gn)


CONF_HALO = 32
CONF_ROWS = 32


def _conf_kernel(a_ref, b_ref, ap_ref, bp_ref, z_ref, w_ref, cb_ref, lg_ref, lb_ref, o_ref, glu_scr,
                 *, tl):
    i = pl.program_id(1)
    glu_scr[CONF_HALO:CONF_HALO + tl, :] = a_ref[...].astype(F32) * _sigmoid(b_ref[...].astype(F32))
    prev = ap_ref[...].astype(F32) * _sigmoid(bp_ref[...].astype(F32))
    glu_scr[0:CONF_HALO, :] = jnp.where(i > 0, prev, 0.0)
    for rt in range(tl // CONF_ROWS):
        base = rt * CONF_ROWS
        acc = jnp.broadcast_to(cb_ref[...], (CONF_ROWS, CONV_WIDTH))
        for kk in range(CONV_KERNEL):
            off = base + CONF_HALO - (CONV_KERNEL - 1) + kk
            acc = acc + w_ref[kk:kk + 1, :] * glu_scr[off:off + CONF_ROWS, :]
        mu = jnp.mean(acc, axis=-1, keepdims=True)
        cen = acc - mu
        var = jnp.mean(cen * cen, axis=-1, keepdims=True)
        y = cen * lax.rsqrt(var + EPS) * lg_ref[...] + lb_ref[...]
        z = z_ref[base:base + CONF_ROWS, :].astype(F32)
        o_ref[base:base + CONF_ROWS, :] = (_silu(y) * _silu(z)).astype(o_ref.dtype)


def _conformer_mixer(p3, conv_w, conv_b, ln_g, ln_b):
    bsz, seq, _ = p3.shape
    tl = 256
    cw = CONV_WIDTH
    ca, cb, cz = P_GLU // cw, P_GLU // cw + 1, P_Z_B // cw
    halo_per_tile = tl // CONF_HALO
    cur = lambda c: pl.BlockSpec((None, tl, cw), lambda b, i, c=c: (b, i, c))
    prev = lambda c: pl.BlockSpec(
        (None, CONF_HALO, cw), lambda b, i, c=c: (b, jnp.maximum(i * halo_per_tile - 1, 0), c))
    vec = pl.BlockSpec((1, cw), lambda b, i: (0, 0))
    return pl.pallas_call(
        functools.partial(_conf_kernel, tl=tl),
        grid=(bsz, seq // tl),
        in_specs=[cur(ca), cur(cb), prev(ca), prev(cb), cur(cz),
                  pl.BlockSpec((CONV_KERNEL, cw), lambda b, i: (0, 0)), vec, vec, vec],
        out_specs=pl.BlockSpec((None, tl, cw), lambda b, i: (b, i, 0)),
        out_shape=jax.ShapeDtypeStruct((bsz, seq, cw), BF16),
        scratch_shapes=[pltpu.VMEM((CONF_HALO + tl, cw), F32)],
        compiler_params=_cparams(("parallel", "parallel")),
    )(p3, p3, p3, p3, p3, conv_w, conv_b, ln_g, ln_b)


def _t5_bucket_int(rel):
    nb = REL_BUCKETS // 2
    max_exact = nb // 2
    n = jnp.abs(rel)
    large = jnp.full(rel.shape, max_exact, I32)
    for thr in (12, 16, 23, 32, 46, 64, 91):
        large = large + jnp.where(n >= thr, 1, 0)
    large = jnp.minimum(large, nb - 1)
    return jnp.where(rel > 0, nb, 0) + jnp.where(n < max_exact, n, large)


def _bias_kernel(rb_ref, o_ref):
    r = lax.broadcasted_iota(I32, (DSA_BLOCK, DSA_KBLOCK), 0)
    c = lax.broadcasted_iota(I32, (DSA_BLOCK, DSA_KBLOCK), 1)
    for kind in range(DSA_BIAS_KINDS):
        bucket = _t5_bucket_int(c - r + (kind - (DSA_BIAS_KINDS - 1)) * DSA_BLOCK)
        for h in range(ATT_HEADS):
            val = jnp.zeros((DSA_BLOCK, DSA_KBLOCK), F32)
            for b in range(REL_BUCKETS):
                val = jnp.where(bucket == b, rb_ref[b, h], val)
            o_ref[kind, h] = val


def _bias_tiles(rel_bias):
    return pl.pallas_call(
        _bias_kernel,
        in_specs=[pl.BlockSpec(memory_space=pltpu.SMEM)],
        out_shape=jax.ShapeDtypeStruct((DSA_BIAS_KINDS, ATT_HEADS, DSA_BLOCK, DSA_KBLOCK), F32),
    )(rel_bias)


def _dsa_kernel(q3_ref, k3_ref, wi_ref, qc_ref, zc_ref, kc_ref, vc_ref, qg_ref, kg_ref, tab_ref,
                o_ref, kp_scr, kn_scr, qp_scr, key_scr, sel_scr, qs_scr, lg_scr, mx_scr, ls_scr, acc_scr,
                *, seq, topk):
    blk = DSA_BLOCK
    kblk = DSA_KBLOCK
    hd = ATT_HEAD_DIM
    i = pl.program_id(1)
    nkv = (i * blk) // kblk + 1
    kf = float(topk)

    @pl.when(i == 0)
    def _prepare_keys():
        rt = min(seq, 256)
        for t in range(seq // rt):
            rs = slice(t * rt, (t + 1) * rt)
            k3 = k3_ref[rs, :]
            lane = lax.broadcasted_iota(I32, k3.shape, 1)
            hi = k3.astype(BF16)
            lo = (k3 - hi.astype(F32)).astype(BF16)
            kp_scr[rs, :] = jnp.where(lane >= 2 * IDX_HEAD_DIM, lo, hi)
            kc = kc_ref[rs, :].astype(F32)
            for g in range(ATT_KV_HEADS):
                ls = slice(g * hd, (g + 1) * hd)
                kg = kc[:, ls]
                kgn = kg * lax.rsqrt(jnp.mean(kg * kg, axis=-1, keepdims=True) + EPS) * kg_ref[...]
                kn_scr[rs, ls] = kgn.astype(BF16)

    q3 = q3_ref[...]
    lane = lax.broadcasted_iota(I32, q3.shape, 1) & (IDX_PAD - 1)
    hi = q3.astype(BF16)
    lo = (q3 - hi.astype(F32)).astype(BF16)
    qp = jnp.where((lane >= IDX_HEAD_DIM) & (lane < 2 * IDX_HEAD_DIM), lo, hi)
    for h in range(IDX_HEADS):
        qp_scr[h * blk:(h + 1) * blk, :] = qp[:, h * IDX_PAD:(h + 1) * IDX_PAD]
    wsc = wi_ref[...] * ((IDX_HEAD_DIM ** -0.5) * (IDX_HEADS ** -0.5))

    row = lax.broadcasted_iota(I32, (blk, kblk), 0)
    col = lax.broadcasted_iota(I32, (blk, kblk), 1)
    chunk_gap = (col // CHUNK) - (row // CHUNK)
    q_chunk0 = i * (blk // CHUNK)

    def score_body(j, carry):
        kb = kp_scr[pl.ds(pl.multiple_of(j * kblk, kblk), kblk), :]
        s_all = _dg(qp_scr[...], kb, NT)
        sc = jnp.zeros((blk, kblk), F32)
        for h in range(IDX_HEADS):
            sc = sc + wsc[:, h:h + 1] * jnp.maximum(s_all[h * blk:(h + 1) * blk, :], 0.0)
        sc = sc + 0.0
        bits = pltpu.bitcast(sc, I32)
        key = jnp.where(bits < 0, bits ^ jnp.int32(0x7FFFFFFF), bits)
        slack = q_chunk0 - j * (kblk // CHUNK)
        key_scr[j] = jnp.where(chunk_gap <= slack, key, jnp.int32(INT_MIN))
        return carry

    lax.fori_loop(0, nkv, score_body, 0)

    def count_where(pred):
        def body(j, acc):
            m = jnp.where(pred(key_scr[j]), 1.0, 0.0)
            return acc + (m[:, :LANES] + m[:, LANES:])
        acc = lax.fori_loop(0, nkv, body, jnp.zeros((blk, LANES), F32))
        return jnp.sum(acc, axis=1, keepdims=True)

    cnt0 = count_where(lambda kv: kv >= 0)
    cur0 = jnp.where(cnt0 >= kf, jnp.int32(0), jnp.int32(INT_MIN))

    def bit_body(t, cur):
        cand = cur | lax.shift_left(jnp.int32(1), jnp.int32(30) - t)
        cnt = count_where(lambda kv: kv >= cand)
        return jnp.where(cnt >= kf, cand, cur)

    thr = lax.fori_loop(0, 31, bit_body, cur0)
    need = kf - count_where(lambda kv: kv > thr)

    ku = lax.broadcasted_iota(I32, (kblk, kblk), 0)
    kv_ = lax.broadcasted_iota(I32, (kblk, kblk), 1)
    upper = jnp.where(ku <= kv_, 1.0, 0.0).astype(BF16)

    def sel_body(j, run):
        key = key_scr[j]
        eq = key == thr
        pre = _dg(jnp.where(eq, 1.0, 0.0).astype(BF16), upper)
        take_tie = jnp.where(eq & (run + pre <= need), 0.0, NEG_BIG)
        take = jnp.where(key > thr, 0.0, take_tie)
        sel_scr[j] = jnp.where(key == jnp.int32(INT_MIN), NEG_BIG, take)
        return run + pre[:, kblk - 1:kblk]

    lax.fori_loop(0, nkv, sel_body, jnp.zeros((blk, 1), F32))

    qc = qc_ref[...].astype(F32)
    for h in range(ATT_HEADS):
        qh = qc[:, h * hd:(h + 1) * hd]
        qn = qh * lax.rsqrt(jnp.mean(qh * qh, axis=-1, keepdims=True) + EPS) * qg_ref[...]
        g, hh = divmod(h, ATT_GROUP)
        qs_scr[g, hh * blk:(hh + 1) * blk, :] = (qn * (hd ** -0.5)).astype(BF16)

    rows4 = ATT_GROUP * blk

    mx_scr[...] = jnp.full(mx_scr.shape, NEG_BIG, F32)

    def logit_body(j, carry):
        kk = kn_scr[pl.ds(pl.multiple_of(j * kblk, kblk), kblk), :]
        neg4 = jnp.concatenate([sel_scr[j]] * ATT_GROUP, axis=0)
        kind = jnp.clip((j * kblk) // blk - i + (DSA_BIAS_KINDS - 1), 0, DSA_BIAS_KINDS - 1)
        for g in range(ATT_KV_HEADS):
            bias = jnp.concatenate([tab_ref[kind, g * ATT_GROUP + hh] for hh in range(ATT_GROUP)],
                                   axis=0)
            lg = _dg(qs_scr[g], kk[:, g * hd:(g + 1) * hd], NT) + bias + neg4
            lg_scr[j, g] = lg
            mx_scr[g] = jnp.maximum(mx_scr[g], jnp.maximum(lg[:, :LANES], lg[:, LANES:]))
        return carry

    lax.fori_loop(0, nkv, logit_body, 0)
    for g in range(ATT_KV_HEADS):
        mx_scr[g] = jnp.broadcast_to(jnp.max(mx_scr[g], axis=1, keepdims=True), (rows4, LANES))
    acc_scr[...] = jnp.zeros(acc_scr.shape, F32)
    ls_scr[...] = jnp.zeros(ls_scr.shape, F32)

    def pv_body(j, carry):
        vv = vc_ref[pl.ds(pl.multiple_of(j * kblk, kblk), kblk), :]
        for g in range(ATT_KV_HEADS):
            m = mx_scr[g]
            lg = lg_scr[j, g]
            p = jnp.exp(jnp.concatenate([lg[:, :LANES] - m, lg[:, LANES:] - m], axis=1))
            acc_scr[g] += _dg(p.astype(BF16), vv[:, g * hd:(g + 1) * hd])
            ls_scr[g] += p[:, :LANES] + p[:, LANES:]
        return carry

    lax.fori_loop(0, nkv, pv_body, 0)

    for h in range(ATT_HEADS):
        g, hh = divmod(h, ATT_GROUP)
        rs = slice(hh * blk, (hh + 1) * blk)
        ls = slice(h * hd, (h + 1) * hd)
        o = acc_scr[g, rs, :] / jnp.sum(ls_scr[g, rs, :], axis=1, keepdims=True)
        z = zc_ref[:, ls].astype(F32)
        o_ref[:, ls] = (o * _silu(z)).astype(o_ref.dtype)


def _dsa_mixer(p3, q3, k3, w_i, q_gain, k_gain, bias_tab):
    bsz, seq, _ = p3.shape
    blk = DSA_BLOCK
    kblk = DSA_KBLOCK
    nblk = seq // blk
    nkb = seq // kblk
    rows4 = ATT_GROUP * blk
    topk = min(TOPK_MAX, seq // 4)
    return pl.pallas_call(
        functools.partial(_dsa_kernel, seq=seq, topk=topk),
        grid=(bsz, nblk),
        in_specs=[
            pl.BlockSpec((None, blk, IDX_HEADS * IDX_PAD), lambda b, i: (b, i, 0)),
            pl.BlockSpec((None, seq, IDX_PAD), lambda b, i: (b, 0, 0)),
            pl.BlockSpec((None, blk, IDX_HEADS), lambda b, i: (b, i, 0)),
            pl.BlockSpec((None, blk, ATT_WIDTH), lambda b, i: (b, i, P_Q_C // ATT_WIDTH)),
            pl.BlockSpec((None, blk, ATT_WIDTH), lambda b, i: (b, i, P_Z_C // ATT_WIDTH)),
            pl.BlockSpec((None, seq, ATT_KV_WIDTH), lambda b, i: (b, 0, P_K_C // ATT_KV_WIDTH)),
            pl.BlockSpec((None, seq, ATT_KV_WIDTH), lambda b, i: (b, 0, P_V_C // ATT_KV_WIDTH)),
            pl.BlockSpec((1, ATT_HEAD_DIM), lambda b, i: (0, 0)),
            pl.BlockSpec((1, ATT_HEAD_DIM), lambda b, i: (0, 0)),
            pl.BlockSpec((DSA_BIAS_KINDS, ATT_HEADS, blk, kblk), lambda b, i: (0, 0, 0, 0)),
        ],
        out_specs=pl.BlockSpec((None, blk, ATT_WIDTH), lambda b, i: (b, i, 0)),
        out_shape=jax.ShapeDtypeStruct((bsz, seq, ATT_WIDTH), BF16),
        scratch_shapes=[
            pltpu.VMEM((seq, IDX_PAD), BF16),
            pltpu.VMEM((seq, ATT_KV_WIDTH), BF16),
            pltpu.VMEM((IDX_HEADS * blk, IDX_PAD), BF16),
            pltpu.VMEM((nkb, blk, kblk), I32),
            pltpu.VMEM((nkb, blk, kblk), F32),
            pltpu.VMEM((ATT_KV_HEADS, rows4, ATT_HEAD_DIM), BF16),
            pltpu.VMEM((nkb, ATT_KV_HEADS, rows4, kblk), F32),
            pltpu.VMEM((ATT_KV_HEADS, rows4, LANES), F32),
            pltpu.VMEM((ATT_KV_HEADS, rows4, LANES), F32),
            pltpu.VMEM((ATT_KV_HEADS, rows4, ATT_HEAD_DIM), F32),
        ],
        compiler_params=_cparams(("arbitrary", "arbitrary")),
    )(q3, k3, w_i, p3, p3, p3, p3, q_gain, k_gain, bias_tab)


def _merge_kernel(ya_ref, yb_ref, yc_ref, wa_ref, wb_ref, wc_ref, ga_ref, gb_ref, gc_ref,
                  ba_ref, bb_ref, bc_ref, o_ref):
    acc = _sigmoid(ga_ref[...].astype(F32) + ba_ref[...]) * _dg(ya_ref[...], wa_ref[...])
    acc = acc + _sigmoid(gb_ref[...].astype(F32) + bb_ref[...]) * _dg(yb_ref[...], wb_ref[...])
    acc = acc + _sigmoid(gc_ref[...].astype(F32) + bc_ref[...]) * _dg(yc_ref[...], wc_ref[...])
    o_ref[...] = acc.astype(o_ref.dtype)


def _merge(ya, yb, yc, wa, wb, wc, p2, b_gate):
    t = ya.shape[0]
    d = D_MODEL
    tm, tn = 512, 512
    nj = d // tn
    g0 = P_GATE // tn
    yspec = pl.BlockSpec((tm, ya.shape[1]), lambda i, j: (i, 0))
    wspec = pl.BlockSpec((wa.shape[0], tn), lambda i, j: (0, j))
    gspec = lambda br: pl.BlockSpec((tm, tn), lambda i, j, br=br: (i, g0 + br * nj + j))
    bspec = lambda br: pl.BlockSpec((1, tn), lambda i, j, br=br: (0, br * nj + j))
    return pl.pallas_call(
        _merge_kernel,
        grid=(t // tm, nj),
        in_specs=[yspec, yspec, yspec, wspec, wspec, wspec, gspec(0), gspec(1), gspec(2),
                  bspec(0), bspec(1), bspec(2)],
        out_specs=pl.BlockSpec((tm, tn), lambda i, j: (i, j)),
        out_shape=jax.ShapeDtypeStruct((t, d), BF16),
        compiler_params=_cparams(("parallel", "parallel")),
    )(ya, yb, yc, wa, wb, wc, p2, p2, p2, b_gate, b_gate, b_gate)


def _out_kernel(m_ref, w_ref, x_ref, g_ref, o_ref):
    o_ref[...] = x_ref[...] + g_ref[...] * _dg(m_ref[...], w_ref[...])


def _out_proj(merged, w_out, x2, mod, seq):
    t, d = x2.shape
    tm, tn = min(seq, 1024), 512
    per_batch = seq // tm
    nj = d // tn
    return pl.pallas_call(
        _out_kernel,
        grid=(t // tm, nj),
        in_specs=[
            pl.BlockSpec((tm, d), lambda i, j: (i, 0)),
            pl.BlockSpec((d, tn), lambda i, j: (0, j)),
            pl.BlockSpec((tm, tn), lambda i, j: (i, j)),
            pl.BlockSpec((None, 1, tn), lambda i, j: (i // per_batch, 0, 2 * nj + j)),
        ],
        out_specs=pl.BlockSpec((tm, tn), lambda i, j: (i, j)),
        out_shape=jax.ShapeDtypeStruct((t, d), F32),
        compiler_params=_cparams(("parallel", "parallel")),
    )(merged, w_out, x2, mod)


def _wide_weight(w):
    cols = [(_OFF_QKV_A, 3 * GDN_WIDTH), (_OFF_Z_A, GDN_WIDTH), (_OFF_GLU, 2 * CONV_WIDTH),
            (_OFF_Z_B, CONV_WIDTH), (_OFF_Q_C, ATT_WIDTH), (_OFF_Z_C, ATT_WIDTH),
            (_OFF_GATE, N_BRANCH * D_MODEL), (_OFF_K_C, ATT_KV_WIDTH), (_OFF_V_C, ATT_KV_WIDTH)]
    return jnp.concatenate([w[:, o:o + n] for o, n in cols], axis=1).astype(BF16)


def _narrow_weight(w):
    cols = [(_OFF_Q_I, IDX_HEADS * IDX_HEAD_DIM), (_OFF_K_I, IDX_HEAD_DIM), (_OFF_W_I, IDX_HEADS),
            (_OFF_BETA, GDN_HEADS), (_OFF_ALPHA, GDN_HEADS)]
    used = sum(n for _, n in cols)
    parts = [w[:, o:o + n] for o, n in cols] + [jnp.zeros((w.shape[0], NS_NARROW - used), w.dtype)]
    return jnp.concatenate(parts, axis=1).astype(BF16)


def _expand3(x, width):
    pad = jnp.zeros(x.shape[:-1] + (IDX_PAD - 3 * width,), x.dtype)
    y = jnp.concatenate([x, x, x, pad], axis=-1)
    return y.reshape(y.shape[:-2] + (-1,))


def kernel(x, c, rel_bias, w_ada, b_ada, norm_g, w_in, b_gate, conv_a_w, a_log, dt_bias, gdn_norm_g,
           w_proj_a, conv_b_w, conv_b_b, ln_b_g, ln_b_b, w_proj_b, q_norm_g, k_norm_g, w_proj_c, w_out):
    bsz, seq, d = x.shape
    t = bsz * seq
    nch = seq // CHUNK
    mod_all = _ada_modulation(c, w_ada, b_ada)
    bias_tab = _bias_tiles(rel_bias)
    for layer in range(DEPTH):
        mod = mod_all[layer].reshape(bsz, 1, 3 * d)
        h = _norm_modulate(x, norm_g[layer].reshape(1, d), mod).reshape(t, d)
        p2 = _matmul(h, _wide_weight(w_in[layer]), BF16, min(t, 1024), 512)
        s2 = _matmul(h, _narrow_weight(w_in[layer]), F32, min(t, 1024), NS_NARROW)
        p3 = p2.reshape(bsz, seq, NP_WIDE)
        s3 = s2.reshape(bsz, seq, NS_NARROW)

        def head_rows(off):
            r = s3[:, :, off:off + GDN_HEADS]
            return jnp.swapaxes(r, 1, 2).reshape(bsz, GDN_HEADS, nch, CHUNK)

        head_params = jnp.stack([a_log[layer], dt_bias[layer]], axis=1)
        y_a = _gdn_mixer(p3, head_rows(S_BETA), head_rows(S_ALPHA), conv_a_w[layer], head_params,
                         gdn_norm_g[layer].reshape(1, GDN_HEAD_DIM))
        y_b = _conformer_mixer(p3, conv_b_w[layer], conv_b_b[layer].reshape(1, CONV_WIDTH),
                               ln_b_g[layer].reshape(1, CONV_WIDTH), ln_b_b[layer].reshape(1, CONV_WIDTH))
        q3 = _expand3(s3[:, :, S_Q_I:S_K_I].reshape(bsz, seq, IDX_HEADS, IDX_HEAD_DIM), IDX_HEAD_DIM)
        k3 = _expand3(s3[:, :, S_K_I:S_W_I].reshape(bsz, seq, 1, IDX_HEAD_DIM), IDX_HEAD_DIM)
        y_c = _dsa_mixer(p3, q3, k3, s3[:, :, S_W_I:S_W_I + IDX_HEADS],
                         q_norm_g[layer].reshape(1, ATT_HEAD_DIM), k_norm_g[layer].reshape(1, ATT_HEAD_DIM),
                         bias_tab)
        merged = _merge(y_a.reshape(t, GDN_WIDTH), y_b.reshape(t, CONV_WIDTH), y_c.reshape(t, ATT_WIDTH),
                        w_proj_a[layer].astype(BF16), w_proj_b[layer].astype(BF16),
                        w_proj_c[layer].astype(BF16), p2, b_gate[layer].reshape(1, N_BRANCH * d))
        x = _out_proj(merged, w_out[layer].astype(BF16), x.reshape(t, d), mod, seq).reshape(bsz, seq, d)
    return x
```

```python
import functools

import jax
import jax.numpy as jnp
from jax import lax
from jax.experimental import pallas as pl
from jax.experimental.pallas import tpu as pltpu

F32 = jnp.float32
BF16 = jnp.bfloat16
I32 = jnp.int32

D_MODEL = 2048
DEPTH = 4
CHUNK = 64
GDN_HEADS = 8
GDN_HEAD_DIM = 128
GDN_WIDTH = GDN_HEADS * GDN_HEAD_DIM
GDN_CONV = 4
CONV_WIDTH = 1024
CONV_KERNEL = 31
ATT_HEADS = 8
ATT_KV_HEADS = 2
ATT_HEAD_DIM = 128
ATT_WIDTH = ATT_HEADS * ATT_HEAD_DIM
ATT_KV_WIDTH = ATT_KV_HEADS * ATT_HEAD_DIM
ATT_GROUP = ATT_HEADS // ATT_KV_HEADS
IDX_HEADS = 8
IDX_HEAD_DIM = 64
TOPK_MAX = 256
REL_BUCKETS = 32
N_BRANCH = 3
EPS = 1e-6

_OFF_QKV_A = 0
_OFF_Z_A = _OFF_QKV_A + 3 * GDN_WIDTH
_OFF_BETA = _OFF_Z_A + GDN_WIDTH
_OFF_ALPHA = _OFF_BETA + GDN_HEADS
_OFF_GLU = _OFF_ALPHA + GDN_HEADS
_OFF_Z_B = _OFF_GLU + 2 * CONV_WIDTH
_OFF_Q_C = _OFF_Z_B + CONV_WIDTH
_OFF_K_C = _OFF_Q_C + ATT_WIDTH
_OFF_V_C = _OFF_K_C + ATT_KV_WIDTH
_OFF_Z_C = _OFF_V_C + ATT_KV_WIDTH
_OFF_Q_I = _OFF_Z_C + ATT_WIDTH
_OFF_K_I = _OFF_Q_I + IDX_HEADS * IDX_HEAD_DIM
_OFF_W_I = _OFF_K_I + IDX_HEAD_DIM
_OFF_GATE = _OFF_W_I + IDX_HEADS
N_IN = _OFF_GATE + N_BRANCH * D_MODEL

P_QKV_A = 0
P_Z_A = 3072
P_GLU = 4096
P_Z_B = 6144
P_Q_C = 7168
P_Z_C = 8192
P_GATE = 9216
P_K_C = 15360
P_V_C = 15616
NP_WIDE = 15872
S_Q_I = 0
S_K_I = 512
S_W_I = 576
S_BETA = 584
S_ALPHA = 592
NS_NARROW = 640

LANES = 128
SUBLANES = 8
DSA_BLOCK = 128
DSA_QROWS = 256
DSA_KBLOCK = 256
DSA_BIAS_KINDS = DSA_KBLOCK // DSA_BLOCK + 2
IDX_PAD = 256
VMEM_LIMIT = 56 * 1024 * 1024
NEG_BIG = -1e30
INT_MIN = -(2 ** 31)

NN = (((1,), (0,)), ((), ()))
NT = (((1,), (1,)), ((), ()))
TN = (((0,), (0,)), ((), ()))


def _dg(a, b, dims=NN):
    return lax.dot_general(a, b, dims, preferred_element_type=F32)


def _split2(x):
    hi = x.astype(BF16)
    lo = (x - hi.astype(F32)).astype(BF16)
    return hi, lo


def _dot_b(a, b, dims=NN):
    return _dg(a.astype(BF16), b.astype(BF16), dims)


def _dot_hp(a, b, dims=NN):
    ah, al = _split2(a)
    bh, bl = _split2(b)
    return _dg(ah, bh, dims) + (_dg(al, bh, dims) + _dg(ah, bl, dims))


def _dot_hp_exact_rhs(a, b_bf16, dims=NN):
    a0 = a.astype(BF16)
    r = a - a0.astype(F32)
    a1 = r.astype(BF16)
    a2 = (r - a1.astype(F32)).astype(BF16)
    return _dg(a0, b_bf16, dims) + (_dg(a1, b_bf16, dims) + _dg(a2, b_bf16, dims))


def _sigmoid(x):
    return 1.0 / (1.0 + jnp.exp(-x))


def _silu(x):
    return x * _sigmoid(x)


def _softplus(x):
    return jnp.maximum(x, 0.0) + jnp.log(1.0 + jnp.exp(-jnp.abs(x)))


def _cparams(sem):
    return pltpu.CompilerParams(dimension_semantics=sem, vmem_limit_bytes=VMEM_LIMIT)


def _ada_kernel(c_ref, w_ref, b_ref, o_ref):
    cond = _silu(c_ref[...])
    o_ref[...] = _dot_hp(cond, w_ref[...]) + b_ref[...]


def _ada_modulation(c, w_ada, b_ada):
    depth, d, n = w_ada.shape
    bsz = c.shape[0]
    tn = 512
    return pl.pallas_call(
        _ada_kernel,
        grid=(depth, n // tn),
        in_specs=[
            pl.BlockSpec((bsz, d), lambda l, j: (0, 0)),
            pl.BlockSpec((None, d, tn), lambda l, j: (l, 0, j)),
            pl.BlockSpec((None, 1, tn), lambda l, j: (l, 0, j)),
        ],
        out_specs=pl.BlockSpec((None, bsz, tn), lambda l, j: (l, 0, j)),
        out_shape=jax.ShapeDtypeStruct((depth, bsz, n), F32),
        compiler_params=_cparams(("parallel", "parallel")),
    )(c, w_ada, b_ada.reshape(depth, 1, n))


def _norm_kernel(x_ref, g_ref, sc_ref, sh_ref, o_ref):
    x = x_ref[...]
    ms = jnp.mean(x * x, axis=-1, keepdims=True)
    y = x * lax.rsqrt(ms + EPS) * g_ref[...]
    o_ref[...] = (y * (1.0 + sc_ref[...]) + sh_ref[...]).astype(o_ref.dtype)


def _norm_modulate(x, g, mod):
    bsz, seq, d = x.shape
    tl = 512
    return pl.pallas_call(
        _norm_kernel,
        grid=(bsz, seq // tl),
        in_specs=[
            pl.BlockSpec((None, tl, d), lambda b, i: (b, i, 0)),
            pl.BlockSpec((1, d), lambda b, i: (0, 0)),
            pl.BlockSpec((None, 1, d), lambda b, i: (b, 0, 1)),
            pl.BlockSpec((None, 1, d), lambda b, i: (b, 0, 0)),
        ],
        out_specs=pl.BlockSpec((None, tl, d), lambda b, i: (b, i, 0)),
        out_shape=jax.ShapeDtypeStruct((bsz, seq, d), BF16),
        compiler_params=_cparams(("parallel", "parallel")),
    )(x, g, mod, mod)


def _mm_kernel(a_ref, b_ref, o_ref):
    o_ref[...] = _dg(a_ref[...], b_ref[...]).astype(o_ref.dtype)


def _matmul(a, b, out_dtype, tm, tn):
    m, k = a.shape
    n = b.shape[1]
    return pl.pallas_call(
        _mm_kernel,
        grid=(m // tm, n // tn),
        in_specs=[
            pl.BlockSpec((tm, k), lambda i, j: (i, 0)),
            pl.BlockSpec((k, tn), lambda i, j: (0, j)),
        ],
        out_specs=pl.BlockSpec((tm, tn), lambda i, j: (i, j)),
        out_shape=jax.ShapeDtypeStruct((m, n), out_dtype),
        compiler_params=_cparams(("parallel", "parallel")),
    )(a, b)


def _tri_inverse(mats, eye):
    ps = list(mats)
    xs = [eye - a for a in mats]
    steps = CHUNK.bit_length() - 2
    for _ in range(steps):
        ps = [_dot_b(p, p) for p in ps]
        xs = [x + _dot_b(x, p) for x, p in zip(xs, ps)]
    return xs


GDN_UNROLL = 4


def _gdn_kernel(hp_ref, q_ref, k_ref, v_ref, z_ref, cwq_ref, cwk_ref, cwv_ref, br_ref, ar_ref,
                gn_ref, o_ref, pad_scr, qn_scr, kn_scr, vn_scr, gc_scr, beta_scr,
                lhs_scr, cadd_scr, *, hb, seq):
    hd = GDN_HEAD_DIM
    nch = seq // CHUNK
    head0 = pl.program_id(1) * hb
    row_tile = min(seq, 256)
    unroll = min(GDN_UNROLL, nch)

    ii = lax.broadcasted_iota(I32, (CHUNK, CHUNK), 0)
    jj = lax.broadcasted_iota(I32, (CHUNK, CHUNK), 1)
    incl = ii >= jj
    strict = ii > jj
    eye = jnp.where(ii == jj, 1.0, 0.0).astype(F32)
    upper = jnp.where(ii <= jj, 1.0, 0.0).astype(BF16)

    def row_to_col(row):
        return jnp.sum(jnp.where(ii == jj, jnp.broadcast_to(row, (CHUNK, CHUNK)), 0.0),
                       axis=1, keepdims=True)

    for hh in range(hb):
        ls = slice(hh * hd, (hh + 1) * hd)
        for src, cw, dst, mode in ((q_ref, cwq_ref, qn_scr, "q"), (k_ref, cwk_ref, kn_scr, "k"),
                                   (v_ref, cwv_ref, vn_scr, "v")):
            pad_scr[0:8, :] = jnp.zeros((8, hd), F32)
            pad_scr[8:8 + seq, :] = src[:, ls].astype(F32)
            for rt in range(seq // row_tile):
                base = rt * row_tile
                acc = jnp.zeros((row_tile, hd), F32)
                for kk in range(GDN_CONV):
                    off = base + 8 - (GDN_CONV - 1) + kk
                    acc = acc + cw[kk:kk + 1, ls] * pad_scr[off:off + row_tile, :]
                y = _silu(acc)
                if mode != "v":
                    y = y * lax.rsqrt(jnp.sum(y * y, axis=-1, keepdims=True) + EPS)
                if mode == "q":
                    y = y * (hd ** -0.5)
                dst[hh, base:base + row_tile, :] = y
        a_log = hp_ref[head0 + hh, 0]
        dt_b = hp_ref[head0 + hh, 1]
        g2 = -jnp.exp(jnp.full((nch, CHUNK), a_log, F32)) * _softplus(ar_ref[hh] + dt_b)
        gc_scr[hh] = _dot_hp_exact_rhs(g2, upper)
        beta_scr[hh] = _sigmoid(br_ref[hh])

    i_hd = lax.broadcasted_iota(I32, (hd, hd), 0)
    j_hd = lax.broadcasted_iota(I32, (hd, hd), 1)
    eye_hd = jnp.where(i_hd == j_hd, 1.0, 0.0).astype(BF16)

    def load_chunk(n, hh):
        r0 = pl.multiple_of(n * CHUNK, CHUNK)
        return (qn_scr[hh, pl.ds(r0, CHUNK), :], kn_scr[hh, pl.ds(r0, CHUNK), :],
                vn_scr[hh, pl.ds(r0, CHUNK), :], gc_scr[hh, pl.ds(n, 1), :],
                beta_scr[hh, pl.ds(n, 1), :])

    def compute_chunks(loaded):
        qs, ks, vs, gc_rows, beta_rows = zip(*loaded)
        gc_cols = [row_to_col(r) for r in gc_rows]
        beta_cols = [row_to_col(r) for r in beta_rows]
        decays = [jnp.where(incl, jnp.exp(jnp.where(incl, c - r, 0.0)), 0.0)
                  for c, r in zip(gc_cols, gc_rows)]
        k16s = [k.astype(BF16) for k in ks]
        qk_kks = [_dg(jnp.concatenate([q.astype(BF16), k16], axis=0), k16, NT)
                  for q, k16 in zip(qs, k16s)]
        intras = [jnp.where(incl, m[:CHUNK] * d, 0.0).astype(BF16) for m, d in zip(qk_kks, decays)]
        t_invs = _tri_inverse([jnp.where(strict, b * m[CHUNK:] * d, 0.0)
                               for b, m, d in zip(beta_cols, qk_kks, decays)], eye)
        e_gcs = [jnp.exp(c) for c in gc_cols]
        uws = [_dot_b(t, jnp.concatenate([v * b, k * (b * e)], axis=1))
               for t, v, k, b, e in zip(t_invs, vs, ks, beta_cols, e_gcs)]
        kdts = [_dg(eye_hd, (k * jnp.exp(r[:, CHUNK - 1:CHUNK] - c)).astype(BF16), NT).astype(BF16)
                for k, r, c in zip(ks, gc_rows, gc_cols)]
        wu16s = [jnp.concatenate([uw[:, hd:], uw[:, :hd]], axis=1).astype(BF16) for uw in uws]
        kwcs = [_dg(kdt, wu) for kdt, wu in zip(kdts, wu16s)]
        iwus = [_dg(intra, wu) for intra, wu in zip(intras, wu16s)]
        lhs = [jnp.concatenate([kwc[:, :hd], q * e - iwu[:, :hd]], axis=0).astype(BF16)
               for kwc, q, e, iwu in zip(kwcs, qs, e_gcs, iwus)]
        return [(l, kwc[:, hd:], iwu[:, hd:]) for l, kwc, iwu in zip(lhs, kwcs, iwus)]

    def store_chunk(n, hh, lhs, c_add, o_add):
        lhs_scr[hh, n] = lhs
        cadd_scr[hh, n] = c_add
        vn_scr[hh, pl.ds(pl.multiple_of(n * CHUNK, CHUNK), CHUNK), :] = o_add

    def prepare_body(step, carry):
        jobs = [(step * unroll + cu, hh) for cu in range(unroll) for hh in range(hb)]
        results = compute_chunks([load_chunk(n, hh) for n, hh in jobs])
        for (n, hh), res in zip(jobs, results):
            store_chunk(n, hh, *res)
        return carry

    lax.fori_loop(0, nch // unroll, prepare_body, 0)

    def scan_body(n, states):
        r0 = pl.multiple_of(n * CHUNK, CHUNK)
        loaded = [(lhs_scr[hh, n], cadd_scr[hh, n], vn_scr[hh, pl.ds(r0, CHUNK), :],
                   gc_scr[hh, pl.ds(n, 1), CHUNK - 1:CHUNK]) for hh in range(hb)]
        lhss, c_adds, o_adds, g_lasts = zip(*loaded)
        prods = [_dg(l, s.astype(BF16)) for l, s in zip(lhss, states)]
        new_states = [s * jnp.exp(g) - p[:hd] + c for s, g, p, c in zip(states, g_lasts, prods, c_adds)]
        for hh in range(hb):
            qn_scr[hh, pl.ds(r0, CHUNK), :] = prods[hh][hd:] + o_adds[hh]
        return tuple(new_states)

    lax.fori_loop(0, nch, scan_body, tuple(jnp.zeros((hd, hd), F32) for _ in range(hb)))

    for hh in range(hb):
        ls = slice(hh * hd, (hh + 1) * hd)
        for rt in range(seq // row_tile):
            rs = slice(rt * row_tile, (rt + 1) * row_tile)
            o = qn_scr[hh, rs, :]
            on = o * lax.rsqrt(jnp.mean(o * o, axis=-1, keepdims=True) + EPS) * gn_ref[...]
            o_ref[rs, ls] = (on * _silu(z_ref[rs, ls].astype(F32))).astype(o_ref.dtype)


def _gdn_mixer(p3, beta_r, alpha_r, conv_w, head_params, gn, hb=2):
    bsz, seq, _ = p3.shape
    wb = hb * GDN_HEAD_DIM
    nhb = GDN_HEADS // hb
    nch = seq // CHUNK
    qb, kb_, vb, zb = (P_QKV_A // wb, (P_QKV_A + GDN_WIDTH) // wb, (P_QKV_A + 2 * GDN_WIDTH) // wb,
                       P_Z_A // wb)
    col = lambda off: pl.BlockSpec((None, seq, wb), lambda b, h, off=off: (b, 0, off + h))
    cwspec = lambda off: pl.BlockSpec((GDN_CONV, wb), lambda b, h, off=off: (0, off + h))
    rows = pl.BlockSpec((None, hb, nch, CHUNK), lambda b, h: (b, h, 0, 0))
    return pl.pallas_call(
        functools.partial(_gdn_kernel, hb=hb, seq=seq),
        grid=(bsz, nhb),
        in_specs=[
            pl.BlockSpec(memory_space=pltpu.SMEM),
            col(qb), col(kb_), col(vb), col(zb),
            cwspec(0), cwspec(nhb), cwspec(2 * nhb),
            rows, rows,
            pl.BlockSpec((1, GDN_HEAD_DIM), lambda b, h: (0, 0)),
        ],
        out_specs=pl.BlockSpec((None, seq, wb), lambda b, h: (b, 0, h)),
        out_shape=jax.ShapeDtypeStruct((bsz, seq, GDN_WIDTH), BF16),
        scratch_shapes=[
            pltpu.VMEM((seq + 8, GDN_HEAD_DIM), F32),
            pltpu.VMEM((hb, seq, GDN_HEAD_DIM), F32),
            pltpu.VMEM((hb, seq, GDN_HEAD_DIM), F32),
            pltpu.VMEM((hb, seq, GDN_HEAD_DIM), F32),
            pltpu.VMEM((hb, nch, CHUNK), F32),
            pltpu.VMEM((hb, nch, CHUNK), F32),
            pltpu.VMEM((hb, nch, GDN_HEAD_DIM + CHUNK, GDN_HEAD_DIM), BF16),
            pltpu.VMEM((hb, nch, GDN_HEAD_DIM, GDN_HEAD_DIM), F32),
        ],
        compiler_params=_cparams(("parallel", "parallel")),
    )(head_params, p3, p3, p3, p3, conv_w, conv_w, conv_w, beta_r, alpha_r, gn)


CONF_HALO = 32
CONF_ROWS = 32
CONF_COPY_ROWS = 64


def _conf_kernel(a_ref, b_ref, ap_ref, bp_ref, z_ref, w_ref, cb_ref, lg_ref, lb_ref, o_ref, glu_scr,
                 *, tl):
    i = pl.program_id(1)
    rows = CONF_HALO + tl
    for rt in range(tl // CONF_COPY_ROWS):
        rs = slice(rt * CONF_COPY_ROWS, (rt + 1) * CONF_COPY_ROWS)
        glu_scr[0, CONF_HALO + rs.start:CONF_HALO + rs.stop, :] = (
            a_ref[rs, :].astype(F32) * _sigmoid(b_ref[rs, :].astype(F32)))
    prev = ap_ref[...].astype(F32) * _sigmoid(bp_ref[...].astype(F32))
    glu_scr[0, 0:CONF_HALO, :] = jnp.where(i > 0, prev, 0.0)
    for s in range(1, SUBLANES):
        for r0 in range(0, rows - SUBLANES, CONF_COPY_ROWS):
            n = min(CONF_COPY_ROWS, rows - SUBLANES - r0)
            glu_scr[s, r0:r0 + n, :] = glu_scr[0, r0 + s:r0 + s + n, :]
    groups = CONF_ROWS // SUBLANES
    for rt in range(tl // CONF_ROWS):
        base = rt * CONF_ROWS
        acc = jnp.broadcast_to(cb_ref[...], (groups, SUBLANES, CONV_WIDTH))
        for kk in range(CONV_KERNEL):
            off = base + CONF_HALO - (CONV_KERNEL - 1) + kk
            s = off % SUBLANES
            win = glu_scr[s, off - s:off - s + CONF_ROWS, :].reshape(groups, SUBLANES, CONV_WIDTH)
            acc = acc + w_ref[kk][None] * win
        acc = acc.reshape(CONF_ROWS, CONV_WIDTH)
        mu = jnp.mean(acc, axis=-1, keepdims=True)
        cen = acc - mu
        var = jnp.mean(cen * cen, axis=-1, keepdims=True)
        y = cen * lax.rsqrt(var + EPS) * lg_ref[...] + lb_ref[...]
        z = z_ref[base:base + CONF_ROWS, :].astype(F32)
        o_ref[base:base + CONF_ROWS, :] = (_silu(y) * _silu(z)).astype(o_ref.dtype)


def _conformer_mixer(p3, conv_w, conv_b, ln_g, ln_b):
    bsz, seq, _ = p3.shape
    tl = 256
    cw = CONV_WIDTH
    ca, cb, cz = P_GLU // cw, P_GLU // cw + 1, P_Z_B // cw
    halo_per_tile = tl // CONF_HALO
    cur = lambda c: pl.BlockSpec((None, tl, cw), lambda b, i, c=c: (b, i, c))
    prev = lambda c: pl.BlockSpec(
        (None, CONF_HALO, cw), lambda b, i, c=c: (b, jnp.maximum(i * halo_per_tile - 1, 0), c))
    vec = pl.BlockSpec((1, cw), lambda b, i: (0, 0))
    return pl.pallas_call(
        functools.partial(_conf_kernel, tl=tl),
        grid=(bsz, seq // tl),
        in_specs=[cur(ca), cur(cb), prev(ca), prev(cb), cur(cz),
                  pl.BlockSpec((CONV_KERNEL, SUBLANES, cw), lambda b, i: (0, 0, 0)), vec, vec, vec],
        out_specs=pl.BlockSpec((None, tl, cw), lambda b, i: (b, i, 0)),
        out_shape=jax.ShapeDtypeStruct((bsz, seq, cw), BF16),
        scratch_shapes=[pltpu.VMEM((SUBLANES, CONF_HALO + tl, cw), F32)],
        compiler_params=_cparams(("parallel", "parallel")),
    )(p3, p3, p3, p3, p3, jnp.broadcast_to(conv_w[:, None, :], (CONV_KERNEL, SUBLANES, cw)), conv_b, ln_g, ln_b)


def _t5_bucket_int(rel):
    nb = REL_BUCKETS // 2
    max_exact = nb // 2
    n = jnp.abs(rel)
    large = jnp.full(rel.shape, max_exact, I32)
    for thr in (12, 16, 23, 32, 46, 64, 91):
        large = large + jnp.where(n >= thr, 1, 0)
    large = jnp.minimum(large, nb - 1)
    return jnp.where(rel > 0, nb, 0) + jnp.where(n < max_exact, n, large)


def _bias_kernel(rb_ref, o_ref):
    r = lax.broadcasted_iota(I32, (DSA_BLOCK, DSA_KBLOCK), 0)
    c = lax.broadcasted_iota(I32, (DSA_BLOCK, DSA_KBLOCK), 1)
    for kind in range(DSA_BIAS_KINDS):
        bucket = _t5_bucket_int(c - r + (kind - (DSA_BIAS_KINDS - 1)) * DSA_BLOCK)
        for h in range(ATT_HEADS):
            val = jnp.zeros((DSA_BLOCK, DSA_KBLOCK), F32)
            for b in range(REL_BUCKETS):
                val = jnp.where(bucket == b, rb_ref[b, h], val)
            o_ref[kind, h] = val


def _bias_tiles(rel_bias):
    return pl.pallas_call(
        _bias_kernel,
        in_specs=[pl.BlockSpec(memory_space=pltpu.SMEM)],
        out_shape=jax.ShapeDtypeStruct((DSA_BIAS_KINDS, ATT_HEADS, DSA_BLOCK, DSA_KBLOCK), F32),
    )(rel_bias)


def _dsa_kernel(q3_ref, k3_ref, wi_ref, qc_ref, zc_ref, kc_ref, vc_ref, qg_ref, kg_ref, tab_ref,
                o_ref, kp_scr, kn_scr, qp_scr, key_scr, sel_scr, qs_scr, lg_scr, mx_scr, ls_scr, acc_scr,
                *, seq, topk):
    blk = DSA_BLOCK
    qrows = DSA_QROWS
    kblk = DSA_KBLOCK
    hd = ATT_HEAD_DIM
    i = pl.program_id(1)
    nkv = (i * qrows) // kblk + 1
    kf = float(topk)

    @pl.when(i == 0)
    def _prepare_keys():
        rt = min(seq, 256)
        for t in range(seq // rt):
            rs = slice(t * rt, (t + 1) * rt)
            k3 = k3_ref[rs, :]
            lane = lax.broadcasted_iota(I32, k3.shape, 1)
            hi = k3.astype(BF16)
            lo = (k3 - hi.astype(F32)).astype(BF16)
            kp_scr[rs, :] = jnp.where(lane >= 2 * IDX_HEAD_DIM, lo, hi)
            kc = kc_ref[rs, :].astype(F32)
            for g in range(ATT_KV_HEADS):
                ls = slice(g * hd, (g + 1) * hd)
                kg = kc[:, ls]
                kgn = kg * lax.rsqrt(jnp.mean(kg * kg, axis=-1, keepdims=True) + EPS) * kg_ref[...]
                kn_scr[rs, ls] = kgn.astype(BF16)

    for a in range(qrows // blk):
        q3 = q3_ref[a * blk:(a + 1) * blk, :]
        lane = lax.broadcasted_iota(I32, q3.shape, 1) & (IDX_PAD - 1)
        hi = q3.astype(BF16)
        lo = (q3 - hi.astype(F32)).astype(BF16)
        qp = jnp.where((lane >= IDX_HEAD_DIM) & (lane < 2 * IDX_HEAD_DIM), lo, hi)
        for h in range(IDX_HEADS):
            qp_scr[h * qrows + a * blk:h * qrows + (a + 1) * blk, :] = qp[:, h * IDX_PAD:(h + 1) * IDX_PAD]
    wsc = wi_ref[...] * ((IDX_HEAD_DIM ** -0.5) * (IDX_HEADS ** -0.5))

    row = lax.broadcasted_iota(I32, (qrows, kblk), 0)
    col = lax.broadcasted_iota(I32, (qrows, kblk), 1)
    chunk_gap = (col // CHUNK) - (row // CHUNK)
    q_chunk0 = i * (qrows // CHUNK)

    def score_body(j, carry):
        kb = kp_scr[pl.ds(pl.multiple_of(j * kblk, kblk), kblk), :]
        s_all = _dg(qp_scr[...], kb, NT)
        sc = jnp.zeros((qrows, kblk), F32)
        for h in range(IDX_HEADS):
            sc = sc + wsc[:, h:h + 1] * jnp.maximum(s_all[h * qrows:(h + 1) * qrows, :], 0.0)
        sc = sc + 0.0
        bits = pltpu.bitcast(sc, I32)
        key = jnp.where(bits < 0, bits ^ jnp.int32(0x7FFFFFFF), bits)
        slack = q_chunk0 - j * (kblk // CHUNK)
        key_scr[j] = jnp.where(chunk_gap <= slack, key, jnp.int32(INT_MIN))
        return carry

    lax.fori_loop(0, nkv, score_body, 0)

    def count_where(pred, ref):
        parts = []
        for a in range(qrows // blk):
            rq = slice(a * blk, (a + 1) * blk)
            ref_a = ref[rq]

            def body(j, acc, rq=rq, ref_a=ref_a):
                m = jnp.where(pred(key_scr[j, rq, :], ref_a), 1.0, 0.0)
                return acc + (m[:, :LANES] + m[:, LANES:])
            parts.append(lax.fori_loop(0, nkv, body, jnp.zeros((blk, LANES), F32)))
        return jnp.sum(jnp.concatenate(parts, axis=0), axis=1, keepdims=True)

    ge = lambda kv, ref: kv >= ref
    cnt0 = count_where(ge, jnp.zeros((qrows, 1), I32))
    cur0 = jnp.where(cnt0 >= kf, jnp.int32(0), jnp.int32(INT_MIN))

    def bit_body(t, cur):
        cand = cur | lax.shift_left(jnp.int32(1), jnp.int32(30) - t)
        return jnp.where(count_where(ge, cand) >= kf, cand, cur)

    thr = lax.fori_loop(0, 31, bit_body, cur0)
    need = kf - count_where(lambda kv, ref: kv > ref, thr)

    ku = lax.broadcasted_iota(I32, (kblk, kblk), 0)
    kv_ = lax.broadcasted_iota(I32, (kblk, kblk), 1)
    upper = jnp.where(ku <= kv_, 1.0, 0.0).astype(BF16)

    def sel_body(j, run):
        key = key_scr[j]
        eq = key == thr
        pre = _dg(jnp.where(eq, 1.0, 0.0).astype(BF16), upper)
        take_tie = jnp.where(eq & (run + pre <= need), 0.0, NEG_BIG)
        take = jnp.where(key > thr, 0.0, take_tie)
        sel_scr[j] = jnp.where(key == jnp.int32(INT_MIN), NEG_BIG, take)
        return run + pre[:, kblk - 1:kblk]

    lax.fori_loop(0, nkv, sel_body, jnp.zeros((qrows, 1), F32))

    rows4 = ATT_GROUP * blk
    for a in range(qrows // blk):
        _dsa_attend(a, i * (qrows // blk) + a, nkv, qc_ref, zc_ref, vc_ref, qg_ref, tab_ref, o_ref,
                    kn_scr, sel_scr, qs_scr, lg_scr, mx_scr, ls_scr, acc_scr)


def _dsa_attend(a, qblock, nkv, qc_ref, zc_ref, vc_ref, qg_ref, tab_ref, o_ref,
                kn_scr, sel_scr, qs_scr, lg_scr, mx_scr, ls_scr, acc_scr):
    blk = DSA_BLOCK
    kblk = DSA_KBLOCK
    hd = ATT_HEAD_DIM
    rows4 = ATT_GROUP * blk
    rq = slice(a * blk, (a + 1) * blk)
    qc = qc_ref[rq, :].astype(F32)
    for h in range(ATT_HEADS):
        qh = qc[:, h * hd:(h + 1) * hd]
        qn = qh * lax.rsqrt(jnp.mean(qh * qh, axis=-1, keepdims=True) + EPS) * qg_ref[...]
        g, hh = divmod(h, ATT_GROUP)
        qs_scr[g, hh * blk:(hh + 1) * blk, :] = (qn * (hd ** -0.5)).astype(BF16)
    mx_scr[...] = jnp.full(mx_scr.shape, NEG_BIG, F32)

    def logit_body(j, carry):
        kk = kn_scr[pl.ds(pl.multiple_of(j * kblk, kblk), kblk), :]
        neg4 = jnp.concatenate([sel_scr[j, rq, :]] * ATT_GROUP, axis=0)
        kind = jnp.clip(j * (kblk // blk) - qblock + (DSA_BIAS_KINDS - 1), 0, DSA_BIAS_KINDS - 1)
        for g in range(ATT_KV_HEADS):
            bias = jnp.concatenate([tab_ref[kind, g * ATT_GROUP + hh] for hh in range(ATT_GROUP)],
                                   axis=0)
            lg = _dg(qs_scr[g], kk[:, g * hd:(g + 1) * hd], NT) + bias + neg4
            lg_scr[j, g] = lg
            mx_scr[g] = jnp.maximum(mx_scr[g], jnp.maximum(lg[:, :LANES], lg[:, LANES:]))
        return carry

    lax.fori_loop(0, nkv, logit_body, 0)
    for g in range(ATT_KV_HEADS):
        mx_scr[g] = jnp.broadcast_to(jnp.max(mx_scr[g], axis=1, keepdims=True), (rows4, LANES))
    acc_scr[...] = jnp.zeros(acc_scr.shape, F32)
    ls_scr[...] = jnp.zeros(ls_scr.shape, F32)

    def pv_body(j, carry):
        vv = vc_ref[pl.ds(pl.multiple_of(j * kblk, kblk), kblk), :]
        for g in range(ATT_KV_HEADS):
            m = mx_scr[g]
            lg = lg_scr[j, g]
            p = jnp.exp(jnp.concatenate([lg[:, :LANES] - m, lg[:, LANES:] - m], axis=1))
            acc_scr[g] += _dg(p.astype(BF16), vv[:, g * hd:(g + 1) * hd])
            ls_scr[g] += p[:, :LANES] + p[:, LANES:]
        return carry

    lax.fori_loop(0, nkv, pv_body, 0)

    for h in range(ATT_HEADS):
        g, hh = divmod(h, ATT_GROUP)
        rs = slice(hh * blk, (hh + 1) * blk)
        ls = slice(h * hd, (h + 1) * hd)
        o = acc_scr[g, rs, :] / jnp.sum(ls_scr[g, rs, :], axis=1, keepdims=True)
        z = zc_ref[rq, ls].astype(F32)
        o_ref[rq, ls] = (o * _silu(z)).astype(o_ref.dtype)


def _dsa_mixer(p3, q3, k3, w_i, q_gain, k_gain, bias_tab):
    bsz, seq, _ = p3.shape
    blk = DSA_BLOCK
    kblk = DSA_KBLOCK
    qrows = DSA_QROWS
    nkb = seq // kblk
    rows4 = ATT_GROUP * blk
    topk = min(TOPK_MAX, seq // 4)
    return pl.pallas_call(
        functools.partial(_dsa_kernel, seq=seq, topk=topk),
        grid=(bsz, seq // qrows),
        in_specs=[
            pl.BlockSpec((None, qrows, IDX_HEADS * IDX_PAD), lambda b, i: (b, i, 0)),
            pl.BlockSpec((None, seq, IDX_PAD), lambda b, i: (b, 0, 0)),
            pl.BlockSpec((None, qrows, IDX_HEADS), lambda b, i: (b, i, 0)),
            pl.BlockSpec((None, qrows, ATT_WIDTH), lambda b, i: (b, i, P_Q_C // ATT_WIDTH)),
            pl.BlockSpec((None, qrows, ATT_WIDTH), lambda b, i: (b, i, P_Z_C // ATT_WIDTH)),
            pl.BlockSpec((None, seq, ATT_KV_WIDTH), lambda b, i: (b, 0, P_K_C // ATT_KV_WIDTH)),
            pl.BlockSpec((None, seq, ATT_KV_WIDTH), lambda b, i: (b, 0, P_V_C // ATT_KV_WIDTH)),
            pl.BlockSpec((1, ATT_HEAD_DIM), lambda b, i: (0, 0)),
            pl.BlockSpec((1, ATT_HEAD_DIM), lambda b, i: (0, 0)),
            pl.BlockSpec((DSA_BIAS_KINDS, ATT_HEADS, blk, kblk), lambda b, i: (0, 0, 0, 0)),
        ],
        out_specs=pl.BlockSpec((None, qrows, ATT_WIDTH), lambda b, i: (b, i, 0)),
        out_shape=jax.ShapeDtypeStruct((bsz, seq, ATT_WIDTH), BF16),
        scratch_shapes=[
            pltpu.VMEM((seq, IDX_PAD), BF16),
            pltpu.VMEM((seq, ATT_KV_WIDTH), BF16),
            pltpu.VMEM((IDX_HEADS * qrows, IDX_PAD), BF16),
            pltpu.VMEM((nkb, qrows, kblk), I32),
            pltpu.VMEM((nkb, qrows, kblk), F32),
            pltpu.VMEM((ATT_KV_HEADS, rows4, ATT_HEAD_DIM), BF16),
            pltpu.VMEM((nkb, ATT_KV_HEADS, rows4, kblk), F32),
            pltpu.VMEM((ATT_KV_HEADS, rows4, LANES), F32),
            pltpu.VMEM((ATT_KV_HEADS, rows4, LANES), F32),
            pltpu.VMEM((ATT_KV_HEADS, rows4, ATT_HEAD_DIM), F32),
        ],
        compiler_params=_cparams(("arbitrary", "arbitrary")),
    )(q3, k3, w_i, p3, p3, p3, p3, q_gain, k_gain, bias_tab)


def _merge_kernel(ya_ref, yb_ref, yc_ref, wa_ref, wb_ref, wc_ref, ga_ref, gb_ref, gc_ref,
                  ba_ref, bb_ref, bc_ref, o_ref):
    acc = _sigmoid(ga_ref[...].astype(F32) + ba_ref[...]) * _dg(ya_ref[...], wa_ref[...])
    acc = acc + _sigmoid(gb_ref[...].astype(F32) + bb_ref[...]) * _dg(yb_ref[...], wb_ref[...])
    acc = acc + _sigmoid(gc_ref[...].astype(F32) + bc_ref[...]) * _dg(yc_ref[...], wc_ref[...])
    o_ref[...] = acc.astype(o_ref.dtype)


def _merge(ya, yb, yc, wa, wb, wc, p2, b_gate):
    t = ya.shape[0]
    d = D_MODEL
    tm, tn = min(t, 1024), 512
    nj = d // tn
    g0 = P_GATE // tn
    yspec = pl.BlockSpec((tm, ya.shape[1]), lambda i, j: (i, 0))
    wspec = pl.BlockSpec((wa.shape[0], tn), lambda i, j: (0, j))
    gspec = lambda br: pl.BlockSpec((tm, tn), lambda i, j, br=br: (i, g0 + br * nj + j))
    bspec = lambda br: pl.BlockSpec((1, tn), lambda i, j, br=br: (0, br * nj + j))
    return pl.pallas_call(
        _merge_kernel,
        grid=(t // tm, nj),
        in_specs=[yspec, yspec, yspec, wspec, wspec, wspec, gspec(0), gspec(1), gspec(2),
                  bspec(0), bspec(1), bspec(2)],
        out_specs=pl.BlockSpec((tm, tn), lambda i, j: (i, j)),
        out_shape=jax.ShapeDtypeStruct((t, d), BF16),
        compiler_params=_cparams(("parallel", "parallel")),
    )(ya, yb, yc, wa, wb, wc, p2, p2, p2, b_gate, b_gate, b_gate)


def _out_kernel(m_ref, w_ref, x_ref, g_ref, o_ref):
    o_ref[...] = x_ref[...] + g_ref[...] * _dg(m_ref[...], w_ref[...])


def _out_proj(merged, w_out, x2, mod, seq):
    t, d = x2.shape
    tm, tn = min(seq, 2048), 512
    per_batch = seq // tm
    nj = d // tn
    return pl.pallas_call(
        _out_kernel,
        grid=(t // tm, nj),
        in_specs=[
            pl.BlockSpec((tm, d), lambda i, j: (i, 0)),
            pl.BlockSpec((d, tn), lambda i, j: (0, j)),
            pl.BlockSpec((tm, tn), lambda i, j: (i, j)),
            pl.BlockSpec((None, 1, tn), lambda i, j: (i // per_batch, 0, 2 * nj + j)),
        ],
        out_specs=pl.BlockSpec((tm, tn), lambda i, j: (i, j)),
        out_shape=jax.ShapeDtypeStruct((t, d), F32),
        compiler_params=_cparams(("parallel", "parallel")),
    )(merged, w_out, x2, mod)


def _wide_weight(w):
    cols = [(_OFF_QKV_A, 3 * GDN_WIDTH), (_OFF_Z_A, GDN_WIDTH), (_OFF_GLU, 2 * CONV_WIDTH),
            (_OFF_Z_B, CONV_WIDTH), (_OFF_Q_C, ATT_WIDTH), (_OFF_Z_C, ATT_WIDTH),
            (_OFF_GATE, N_BRANCH * D_MODEL), (_OFF_K_C, ATT_KV_WIDTH), (_OFF_V_C, ATT_KV_WIDTH)]
    return jnp.concatenate([w[:, o:o + n] for o, n in cols], axis=1).astype(BF16)


def _narrow_weight(w):
    cols = [(_OFF_Q_I, IDX_HEADS * IDX_HEAD_DIM), (_OFF_K_I, IDX_HEAD_DIM), (_OFF_W_I, IDX_HEADS),
            (_OFF_BETA, GDN_HEADS), (_OFF_ALPHA, GDN_HEADS)]
    used = sum(n for _, n in cols)
    parts = [w[:, o:o + n] for o, n in cols] + [jnp.zeros((w.shape[0], NS_NARROW - used), w.dtype)]
    return jnp.concatenate(parts, axis=1).astype(BF16)


def _expand3(x, width):
    pad = jnp.zeros(x.shape[:-1] + (IDX_PAD - 3 * width,), x.dtype)
    y = jnp.concatenate([x, x, x, pad], axis=-1)
    return y.reshape(y.shape[:-2] + (-1,))


def kernel(x, c, rel_bias, w_ada, b_ada, norm_g, w_in, b_gate, conv_a_w, a_log, dt_bias, gdn_norm_g,
           w_proj_a, conv_b_w, conv_b_b, ln_b_g, ln_b_b, w_proj_b, q_norm_g, k_norm_g, w_proj_c, w_out):
    bsz, seq, d = x.shape
    t = bsz * seq
    nch = seq // CHUNK
    mod_all = _ada_modulation(c, w_ada, b_ada)
    bias_tab = _bias_tiles(rel_bias)
    for layer in range(DEPTH):
        mod = mod_all[layer].reshape(bsz, 1, 3 * d)
        h = _norm_modulate(x, norm_g[layer].reshape(1, d), mod).reshape(t, d)
        p2 = _matmul(h, _wide_weight(w_in[layer]), BF16, min(t, 2048), 512)
        s2 = _matmul(h, _narrow_weight(w_in[layer]), F32, min(t, 1024), NS_NARROW)
        p3 = p2.reshape(bsz, seq, NP_WIDE)
        s3 = s2.reshape(bsz, seq, NS_NARROW)

        def head_rows(off):
            r = s3[:, :, off:off + GDN_HEADS]
            return jnp.swapaxes(r, 1, 2).reshape(bsz, GDN_HEADS, nch, CHUNK)

        head_params = jnp.stack([a_log[layer], dt_bias[layer]], axis=1)
        y_a = _gdn_mixer(p3, head_rows(S_BETA), head_rows(S_ALPHA), conv_a_w[layer], head_params,
                         gdn_norm_g[layer].reshape(1, GDN_HEAD_DIM))
        y_b = _conformer_mixer(p3, conv_b_w[layer], conv_b_b[layer].reshape(1, CONV_WIDTH),
                               ln_b_g[layer].reshape(1, CONV_WIDTH), ln_b_b[layer].reshape(1, CONV_WIDTH))
        q3 = _expand3(s3[:, :, S_Q_I:S_K_I].reshape(bsz, seq, IDX_HEADS, IDX_HEAD_DIM), IDX_HEAD_DIM)
        k3 = _expand3(s3[:, :, S_K_I:S_W_I].reshape(bsz, seq, 1, IDX_HEAD_DIM), IDX_HEAD_DIM)
        y_c = _dsa_mixer(p3, q3, k3, s3[:, :, S_W_I:S_W_I + IDX_HEADS],
                         q_norm_g[layer].reshape(1, ATT_HEAD_DIM), k_norm_g[layer].reshape(1, ATT_HEAD_DIM),
                         bias_tab)
        merged = _merge(y_a.reshape(t, GDN_WIDTH), y_b.reshape(t, CONV_WIDTH), y_c.reshape(t, ATT_WIDTH),
                        w_proj_a[layer].astype(BF16), w_proj_b[layer].astype(BF16),
                        w_proj_c[layer].astype(BF16), p2, b_gate[layer].reshape(1, N_BRANCH * d))
        x = _out_proj(merged, w_out[layer].astype(BF16), x.reshape(t, d), mod, seq).reshape(bsz, seq, d)
    return x
```

```python
import functools
import math

import jax
import jax.numpy as jnp
from jax import lax
from jax.experimental import pallas as pl
from jax.experimental.pallas import tpu as pltpu

F32 = jnp.float32
BF16 = jnp.bfloat16
I32 = jnp.int32

D_MODEL = 2048
DEPTH = 4
CHUNK = 64
GDN_HEADS = 8
GDN_HEAD_DIM = 128
GDN_WIDTH = GDN_HEADS * GDN_HEAD_DIM
GDN_CONV = 4
CONV_WIDTH = 1024
CONV_KERNEL = 31
ATT_HEADS = 8
ATT_KV_HEADS = 2
ATT_HEAD_DIM = 128
ATT_WIDTH = ATT_HEADS * ATT_HEAD_DIM
ATT_KV_WIDTH = ATT_KV_HEADS * ATT_HEAD_DIM
ATT_GROUP = ATT_HEADS // ATT_KV_HEADS
IDX_HEADS = 8
IDX_HEAD_DIM = 64
TOPK_MAX = 256
REL_BUCKETS = 32
N_BRANCH = 3
EPS = 1e-6

_OFF_QKV_A = 0
_OFF_Z_A = _OFF_QKV_A + 3 * GDN_WIDTH
_OFF_BETA = _OFF_Z_A + GDN_WIDTH
_OFF_ALPHA = _OFF_BETA + GDN_HEADS
_OFF_GLU = _OFF_ALPHA + GDN_HEADS
_OFF_Z_B = _OFF_GLU + 2 * CONV_WIDTH
_OFF_Q_C = _OFF_Z_B + CONV_WIDTH
_OFF_K_C = _OFF_Q_C + ATT_WIDTH
_OFF_V_C = _OFF_K_C + ATT_KV_WIDTH
_OFF_Z_C = _OFF_V_C + ATT_KV_WIDTH
_OFF_Q_I = _OFF_Z_C + ATT_WIDTH
_OFF_K_I = _OFF_Q_I + IDX_HEADS * IDX_HEAD_DIM
_OFF_W_I = _OFF_K_I + IDX_HEAD_DIM
_OFF_GATE = _OFF_W_I + IDX_HEADS
N_IN = _OFF_GATE + N_BRANCH * D_MODEL

P_QKV_A = 0
P_Z_A = 3072
P_GLU = 4096
P_Z_B = 6144
P_Q_C = 7168
P_Z_C = 8192
P_GATE = 9216
P_K_C = 15360
P_V_C = 15616
NP_WIDE = 15872
S_Q_I = 0
S_K_I = 512
S_W_I = 576
S_BETA = 584
S_ALPHA = 592
NS_NARROW = 640

LANES = 128
SUBLANES = 8
DSA_BLOCK = 128
DSA_QROWS = 256
DSA_KBLOCK = 256
DSA_BIAS_KINDS = DSA_KBLOCK // DSA_BLOCK + 2
IDX_PAD = 256
VMEM_LIMIT = 56 * 1024 * 1024
NEG_BIG = -1e30
INT_MIN = -(2 ** 31)

NN = (((1,), (0,)), ((), ()))
NT = (((1,), (1,)), ((), ()))
TN = (((0,), (0,)), ((), ()))


def _dg(a, b, dims=NN):
    return lax.dot_general(a, b, dims, preferred_element_type=F32)


def _split2(x):
    hi = x.astype(BF16)
    lo = (x - hi.astype(F32)).astype(BF16)
    return hi, lo


def _dot_b(a, b, dims=NN):
    return _dg(a.astype(BF16), b.astype(BF16), dims)


def _dot_hp(a, b, dims=NN):
    ah, al = _split2(a)
    bh, bl = _split2(b)
    return _dg(ah, bh, dims) + (_dg(al, bh, dims) + _dg(ah, bl, dims))


def _dot_hp_exact_rhs(a, b_bf16, dims=NN):
    a0 = a.astype(BF16)
    r = a - a0.astype(F32)
    a1 = r.astype(BF16)
    a2 = (r - a1.astype(F32)).astype(BF16)
    return _dg(a0, b_bf16, dims) + (_dg(a1, b_bf16, dims) + _dg(a2, b_bf16, dims))


def _sigmoid(x):
    return 1.0 / (1.0 + jnp.exp(-x))


def _silu(x):
    return x * _sigmoid(x)


def _softplus(x):
    return jnp.maximum(x, 0.0) + jnp.log(1.0 + jnp.exp(-jnp.abs(x)))


def _cparams(sem):
    return pltpu.CompilerParams(dimension_semantics=sem, vmem_limit_bytes=VMEM_LIMIT)


def _ada_kernel(c_ref, w_ref, b_ref, o_ref):
    cond = _silu(c_ref[...])
    o_ref[...] = _dot_hp(cond, w_ref[...]) + b_ref[...]


def _ada_modulation(c, w_ada, b_ada):
    depth, d, n = w_ada.shape
    bsz = c.shape[0]
    tn = 512
    return pl.pallas_call(
        _ada_kernel,
        grid=(depth, n // tn),
        in_specs=[
            pl.BlockSpec((bsz, d), lambda l, j: (0, 0)),
            pl.BlockSpec((None, d, tn), lambda l, j: (l, 0, j)),
            pl.BlockSpec((None, 1, tn), lambda l, j: (l, 0, j)),
        ],
        out_specs=pl.BlockSpec((None, bsz, tn), lambda l, j: (l, 0, j)),
        out_shape=jax.ShapeDtypeStruct((depth, bsz, n), F32),
        compiler_params=_cparams(("parallel", "parallel")),
    )(c, w_ada, b_ada.reshape(depth, 1, n))


def _norm_kernel(x_ref, g_ref, sc_ref, sh_ref, o_ref):
    x = x_ref[...]
    ms = jnp.mean(x * x, axis=-1, keepdims=True)
    y = x * lax.rsqrt(ms + EPS) * g_ref[...]
    o_ref[...] = (y * (1.0 + sc_ref[...]) + sh_ref[...]).astype(o_ref.dtype)


def _norm_modulate(x, g, mod):
    bsz, seq, d = x.shape
    tl = 512
    return pl.pallas_call(
        _norm_kernel,
        grid=(bsz, seq // tl),
        in_specs=[
            pl.BlockSpec((None, tl, d), lambda b, i: (b, i, 0)),
            pl.BlockSpec((1, d), lambda b, i: (0, 0)),
            pl.BlockSpec((None, 1, d), lambda b, i: (b, 0, 1)),
            pl.BlockSpec((None, 1, d), lambda b, i: (b, 0, 0)),
        ],
        out_specs=pl.BlockSpec((None, tl, d), lambda b, i: (b, i, 0)),
        out_shape=jax.ShapeDtypeStruct((bsz, seq, d), BF16),
        compiler_params=_cparams(("parallel", "parallel")),
    )(x, g, mod, mod)


def _mm_kernel(a_ref, b_ref, o_ref):
    o_ref[...] = _dg(a_ref[...], b_ref[...]).astype(o_ref.dtype)


def _matmul(a, b, out_dtype, tm, tn):
    m, k = a.shape
    n = b.shape[1]
    return pl.pallas_call(
        _mm_kernel,
        grid=(m // tm, n // tn),
        in_specs=[
            pl.BlockSpec((tm, k), lambda i, j: (i, 0)),
            pl.BlockSpec((k, tn), lambda i, j: (0, j)),
        ],
        out_specs=pl.BlockSpec((tm, tn), lambda i, j: (i, j)),
        out_shape=jax.ShapeDtypeStruct((m, n), out_dtype),
        compiler_params=_cparams(("parallel", "parallel")),
    )(a, b)


def _tri_inverse(mats, eye):
    ps = list(mats)
    xs = [eye - a for a in mats]
    steps = CHUNK.bit_length() - 2
    for _ in range(steps):
        ps = [_dot_b(p, p) for p in ps]
        xs = [x + _dot_b(x, p) for x, p in zip(xs, ps)]
    return xs


GDN_UNROLL = 8


def _gdn_kernel(hp_ref, q_ref, k_ref, v_ref, z_ref, cwq_ref, cwk_ref, cwv_ref, br_ref, ar_ref,
                gn_ref, o_ref, pad_scr, qn_scr, kn_scr, vn_scr, gc_scr, beta_scr,
                lhs_scr, cadd_scr, *, hb, seq):
    hd = GDN_HEAD_DIM
    nch = seq // CHUNK
    head0 = pl.program_id(1) * hb
    row_tile = min(seq, 256)
    unroll = math.gcd(GDN_UNROLL, nch)

    ii = lax.broadcasted_iota(I32, (CHUNK, CHUNK), 0)
    jj = lax.broadcasted_iota(I32, (CHUNK, CHUNK), 1)
    incl = ii >= jj
    strict = ii > jj
    eye = jnp.where(ii == jj, 1.0, 0.0).astype(F32)
    upper = jnp.where(ii <= jj, 1.0, 0.0).astype(BF16)

    def row_to_col(row):
        return jnp.sum(jnp.where(ii == jj, jnp.broadcast_to(row, (CHUNK, CHUNK)), 0.0),
                       axis=1, keepdims=True)

    for hh in range(hb):
        ls = slice(hh * hd, (hh + 1) * hd)
        for src, cw, dst, mode in ((q_ref, cwq_ref, qn_scr, "q"), (k_ref, cwk_ref, kn_scr, "k"),
                                   (v_ref, cwv_ref, vn_scr, "v")):
            pad_scr[0:8, :] = jnp.zeros((8, hd), F32)
            pad_scr[8:8 + seq, :] = src[:, ls].astype(F32)
            for rt in range(seq // row_tile):
                base = rt * row_tile
                acc = jnp.zeros((row_tile, hd), F32)
                for kk in range(GDN_CONV):
                    off = base + 8 - (GDN_CONV - 1) + kk
                    acc = acc + cw[kk:kk + 1, ls] * pad_scr[off:off + row_tile, :]
                y = _silu(acc)
                if mode != "v":
                    y = y * lax.rsqrt(jnp.sum(y * y, axis=-1, keepdims=True) + EPS)
                if mode == "q":
                    y = y * (hd ** -0.5)
                dst[hh, base:base + row_tile, :] = y
        a_log = hp_ref[head0 + hh, 0]
        dt_b = hp_ref[head0 + hh, 1]
        g2 = -jnp.exp(jnp.full((nch, CHUNK), a_log, F32)) * _softplus(ar_ref[hh] + dt_b)
        gc_scr[hh] = _dot_hp_exact_rhs(g2, upper)
        beta_scr[hh] = _sigmoid(br_ref[hh])

    i_hd = lax.broadcasted_iota(I32, (hd, hd), 0)
    j_hd = lax.broadcasted_iota(I32, (hd, hd), 1)
    eye_hd = jnp.where(i_hd == j_hd, 1.0, 0.0).astype(BF16)

    def load_chunk(n, hh):
        r0 = pl.multiple_of(n * CHUNK, CHUNK)
        return (qn_scr[hh, pl.ds(r0, CHUNK), :], kn_scr[hh, pl.ds(r0, CHUNK), :],
                vn_scr[hh, pl.ds(r0, CHUNK), :], gc_scr[hh, pl.ds(n, 1), :],
                beta_scr[hh, pl.ds(n, 1), :])

    def compute_chunks(loaded):
        qs, ks, vs, gc_rows, beta_rows = zip(*loaded)
        gc_cols = [row_to_col(r) for r in gc_rows]
        beta_cols = [row_to_col(r) for r in beta_rows]
        decays = [jnp.where(incl, jnp.exp(jnp.where(incl, c - r, 0.0)), 0.0)
                  for c, r in zip(gc_cols, gc_rows)]
        k16s = [k.astype(BF16) for k in ks]
        qk_kks = [_dg(jnp.concatenate([q.astype(BF16), k16], axis=0), k16, NT)
                  for q, k16 in zip(qs, k16s)]
        intras = [jnp.where(incl, m[:CHUNK] * d, 0.0).astype(BF16) for m, d in zip(qk_kks, decays)]
        t_invs = _tri_inverse([jnp.where(strict, b * m[CHUNK:] * d, 0.0)
                               for b, m, d in zip(beta_cols, qk_kks, decays)], eye)
        e_gcs = [jnp.exp(c) for c in gc_cols]
        uws = [_dot_b(t, jnp.concatenate([v * b, k * (b * e)], axis=1))
               for t, v, k, b, e in zip(t_invs, vs, ks, beta_cols, e_gcs)]
        kdts = [_dg(eye_hd, (k * jnp.exp(r[:, CHUNK - 1:CHUNK] - c)).astype(BF16), NT).astype(BF16)
                for k, r, c in zip(ks, gc_rows, gc_cols)]
        wu16s = [jnp.concatenate([uw[:, hd:], uw[:, :hd]], axis=1).astype(BF16) for uw in uws]
        kwcs = [_dg(kdt, wu) for kdt, wu in zip(kdts, wu16s)]
        iwus = [_dg(intra, wu) for intra, wu in zip(intras, wu16s)]
        lhs = [jnp.concatenate([kwc[:, :hd], q * e - iwu[:, :hd]], axis=0).astype(BF16)
               for kwc, q, e, iwu in zip(kwcs, qs, e_gcs, iwus)]
        return [(l, kwc[:, hd:], iwu[:, hd:]) for l, kwc, iwu in zip(lhs, kwcs, iwus)]

    def store_chunk(n, hh, lhs, c_add, o_add):
        lhs_scr[hh, n] = lhs
        cadd_scr[hh, n] = c_add
        vn_scr[hh, pl.ds(pl.multiple_of(n * CHUNK, CHUNK), CHUNK), :] = o_add

    def prepare_body(step, carry):
        jobs = [(step * unroll + cu, hh) for cu in range(unroll) for hh in range(hb)]
        results = compute_chunks([load_chunk(n, hh) for n, hh in jobs])
        for (n, hh), res in zip(jobs, results):
            store_chunk(n, hh, *res)
        return carry

    lax.fori_loop(0, nch // unroll, prepare_body, 0)

    def scan_body(n, states):
        r0 = pl.multiple_of(n * CHUNK, CHUNK)
        loaded = [(lhs_scr[hh, n], cadd_scr[hh, n], vn_scr[hh, pl.ds(r0, CHUNK), :],
                   gc_scr[hh, pl.ds(n, 1), CHUNK - 1:CHUNK]) for hh in range(hb)]
        lhss, c_adds, o_adds, g_lasts = zip(*loaded)
        prods = [_dg(l, s.astype(BF16)) for l, s in zip(lhss, states)]
        new_states = [s * jnp.exp(g) - p[:hd] + c for s, g, p, c in zip(states, g_lasts, prods, c_adds)]
        for hh in range(hb):
            qn_scr[hh, pl.ds(r0, CHUNK), :] = prods[hh][hd:] + o_adds[hh]
        return tuple(new_states)

    lax.fori_loop(0, nch, scan_body, tuple(jnp.zeros((hd, hd), F32) for _ in range(hb)))

    for hh in range(hb):
        ls = slice(hh * hd, (hh + 1) * hd)
        for rt in range(seq // row_tile):
            rs = slice(rt * row_tile, (rt + 1) * row_tile)
            o = qn_scr[hh, rs, :]
            on = o * lax.rsqrt(jnp.mean(o * o, axis=-1, keepdims=True) + EPS) * gn_ref[...]
            o_ref[rs, ls] = (on * _silu(z_ref[rs, ls].astype(F32))).astype(o_ref.dtype)


def _gdn_mixer(p3, beta_r, alpha_r, conv_w, head_params, gn, hb=2):
    bsz, seq, _ = p3.shape
    wb = hb * GDN_HEAD_DIM
    nhb = GDN_HEADS // hb
    nch = seq // CHUNK
    qb, kb_, vb, zb = (P_QKV_A // wb, (P_QKV_A + GDN_WIDTH) // wb, (P_QKV_A + 2 * GDN_WIDTH) // wb,
                       P_Z_A // wb)
    col = lambda off: pl.BlockSpec((None, seq, wb), lambda b, h, off=off: (b, 0, off + h))
    cwspec = lambda off: pl.BlockSpec((GDN_CONV, wb), lambda b, h, off=off: (0, off + h))
    rows = pl.BlockSpec((None, hb, nch, CHUNK), lambda b, h: (b, h, 0, 0))
    return pl.pallas_call(
        functools.partial(_gdn_kernel, hb=hb, seq=seq),
        grid=(bsz, nhb),
        in_specs=[
            pl.BlockSpec(memory_space=pltpu.SMEM),
            col(qb), col(kb_), col(vb), col(zb),
            cwspec(0), cwspec(nhb), cwspec(2 * nhb),
            rows, rows,
            pl.BlockSpec((1, GDN_HEAD_DIM), lambda b, h: (0, 0)),
        ],
        out_specs=pl.BlockSpec((None, seq, wb), lambda b, h: (b, 0, h)),
        out_shape=jax.ShapeDtypeStruct((bsz, seq, GDN_WIDTH), BF16),
        scratch_shapes=[
            pltpu.VMEM((seq + 8, GDN_HEAD_DIM), F32),
            pltpu.VMEM((hb, seq, GDN_HEAD_DIM), F32),
            pltpu.VMEM((hb, seq, GDN_HEAD_DIM), F32),
            pltpu.VMEM((hb, seq, GDN_HEAD_DIM), F32),
            pltpu.VMEM((hb, nch, CHUNK), F32),
            pltpu.VMEM((hb, nch, CHUNK), F32),
            pltpu.VMEM((hb, nch, GDN_HEAD_DIM + CHUNK, GDN_HEAD_DIM), BF16),
            pltpu.VMEM((hb, nch, GDN_HEAD_DIM, GDN_HEAD_DIM), F32),
        ],
        compiler_params=_cparams(("parallel", "parallel")),
    )(head_params, p3, p3, p3, p3, conv_w, conv_w, conv_w, beta_r, alpha_r, gn)


CONF_HALO = 32
CONF_ROWS = 32
CONF_COPY_ROWS = 64


def _conf_kernel(a_ref, b_ref, ap_ref, bp_ref, z_ref, w_ref, cb_ref, lg_ref, lb_ref, o_ref, glu_scr,
                 *, tl):
    i = pl.program_id(1)
    rows = CONF_HALO + tl
    for rt in range(tl // CONF_COPY_ROWS):
        rs = slice(rt * CONF_COPY_ROWS, (rt + 1) * CONF_COPY_ROWS)
        glu_scr[0, CONF_HALO + rs.start:CONF_HALO + rs.stop, :] = (
            a_ref[rs, :].astype(F32) * _sigmoid(b_ref[rs, :].astype(F32)))
    prev = ap_ref[...].astype(F32) * _sigmoid(bp_ref[...].astype(F32))
    glu_scr[0, 0:CONF_HALO, :] = jnp.where(i > 0, prev, 0.0)
    for s in range(1, SUBLANES):
        for r0 in range(0, rows - SUBLANES, CONF_COPY_ROWS):
            n = min(CONF_COPY_ROWS, rows - SUBLANES - r0)
            glu_scr[s, r0:r0 + n, :] = glu_scr[0, r0 + s:r0 + s + n, :]
    groups = CONF_ROWS // SUBLANES
    for rt in range(tl // CONF_ROWS):
        base = rt * CONF_ROWS
        acc = jnp.broadcast_to(cb_ref[...], (groups, SUBLANES, CONV_WIDTH))
        for kk in range(CONV_KERNEL):
            off = base + CONF_HALO - (CONV_KERNEL - 1) + kk
            s = off % SUBLANES
            win = glu_scr[s, off - s:off - s + CONF_ROWS, :].reshape(groups, SUBLANES, CONV_WIDTH)
            acc = acc + w_ref[kk][None] * win
        acc = acc.reshape(CONF_ROWS, CONV_WIDTH)
        mu = jnp.mean(acc, axis=-1, keepdims=True)
        cen = acc - mu
        var = jnp.mean(cen * cen, axis=-1, keepdims=True)
        y = cen * lax.rsqrt(var + EPS) * lg_ref[...] + lb_ref[...]
        z = z_ref[base:base + CONF_ROWS, :].astype(F32)
        o_ref[base:base + CONF_ROWS, :] = (_silu(y) * _silu(z)).astype(o_ref.dtype)


def _conformer_mixer(p3, conv_w, conv_b, ln_g, ln_b):
    bsz, seq, _ = p3.shape
    tl = 256
    cw = CONV_WIDTH
    ca, cb, cz = P_GLU // cw, P_GLU // cw + 1, P_Z_B // cw
    halo_per_tile = tl // CONF_HALO
    cur = lambda c: pl.BlockSpec((None, tl, cw), lambda b, i, c=c: (b, i, c))
    prev = lambda c: pl.BlockSpec(
        (None, CONF_HALO, cw), lambda b, i, c=c: (b, jnp.maximum(i * halo_per_tile - 1, 0), c))
    vec = pl.BlockSpec((1, cw), lambda b, i: (0, 0))
    return pl.pallas_call(
        functools.partial(_conf_kernel, tl=tl),
        grid=(bsz, seq // tl),
        in_specs=[cur(ca), cur(cb), prev(ca), prev(cb), cur(cz),
                  pl.BlockSpec((CONV_KERNEL, SUBLANES, cw), lambda b, i: (0, 0, 0)), vec, vec, vec],
        out_specs=pl.BlockSpec((None, tl, cw), lambda b, i: (b, i, 0)),
        out_shape=jax.ShapeDtypeStruct((bsz, seq, cw), BF16),
        scratch_shapes=[pltpu.VMEM((SUBLANES, CONF_HALO + tl, cw), F32)],
        compiler_params=_cparams(("parallel", "parallel")),
    )(p3, p3, p3, p3, p3, jnp.broadcast_to(conv_w[:, None, :], (CONV_KERNEL, SUBLANES, cw)), conv_b, ln_g, ln_b)


def _t5_bucket_int(rel):
    nb = REL_BUCKETS // 2
    max_exact = nb // 2
    n = jnp.abs(rel)
    large = jnp.full(rel.shape, max_exact, I32)
    for thr in (12, 16, 23, 32, 46, 64, 91):
        large = large + jnp.where(n >= thr, 1, 0)
    large = jnp.minimum(large, nb - 1)
    return jnp.where(rel > 0, nb, 0) + jnp.where(n < max_exact, n, large)


def _bias_kernel(rb_ref, o_ref):
    kpos = lax.broadcasted_iota(I32, (DSA_KBLOCK, DSA_BLOCK), 0)
    qpos = lax.broadcasted_iota(I32, (DSA_KBLOCK, DSA_BLOCK), 1)
    for kind in range(DSA_BIAS_KINDS):
        bucket = _t5_bucket_int(kpos - qpos + (kind - (DSA_BIAS_KINDS - 1)) * DSA_BLOCK)
        for h in range(ATT_HEADS):
            val = jnp.zeros((DSA_KBLOCK, DSA_BLOCK), F32)
            for b in range(REL_BUCKETS):
                val = jnp.where(bucket == b, rb_ref[b, h], val)
            o_ref[kind, h] = val


def _bias_tiles(rel_bias):
    return pl.pallas_call(
        _bias_kernel,
        in_specs=[pl.BlockSpec(memory_space=pltpu.SMEM)],
        out_shape=jax.ShapeDtypeStruct((DSA_BIAS_KINDS, ATT_HEADS, DSA_KBLOCK, DSA_BLOCK), F32),
    )(rel_bias)


def _for_block_pairs(n, fn, carry):
    def pair(p, c):
        return fn([2 * p, 2 * p + 1], c)

    c = lax.fori_loop(0, lax.shift_right_logical(n, 1), pair, carry)

    @pl.when((n & 1) == 1)
    def _last():
        fn([n - 1], c)


def _dsa_kernel(q3_ref, k3_ref, wt_ref, qc_ref, zc_ref, kc_ref, vc_ref, qg_ref, kg_ref, tab_ref,
                o_ref, kp_scr, kn_scr, vt_scr, qp_scr, key_scr, sel_scr, qs_scr, lg_scr, mx_scr, ls_scr,
                acc_scr, *, seq, topk):
    blk = DSA_BLOCK
    qrows = DSA_QROWS
    kblk = DSA_KBLOCK
    hd = ATT_HEAD_DIM
    i = pl.program_id(1)
    nkv = (i * qrows) // kblk + 1
    kf = float(topk)

    @pl.when(i == 0)
    def _prepare_keys():
        e_r = lax.broadcasted_iota(I32, (hd, hd), 0)
        e_c = lax.broadcasted_iota(I32, (hd, hd), 1)
        eye_hd = jnp.where(e_r == e_c, 1.0, 0.0).astype(BF16)
        for t in range(seq // kblk):
            rs = slice(t * kblk, (t + 1) * kblk)
            k3 = k3_ref[rs, :]
            lane = lax.broadcasted_iota(I32, k3.shape, 1)
            hi = k3.astype(BF16)
            lo = (k3 - hi.astype(F32)).astype(BF16)
            kp_scr[rs, :] = jnp.where(lane >= 2 * IDX_HEAD_DIM, lo, hi)
            kc = kc_ref[rs, :].astype(F32)
            for g in range(ATT_KV_HEADS):
                ls = slice(g * hd, (g + 1) * hd)
                kg = kc[:, ls]
                kgn = kg * lax.rsqrt(jnp.mean(kg * kg, axis=-1, keepdims=True) + EPS) * kg_ref[...]
                kn_scr[rs, ls] = kgn.astype(BF16)
                vt_scr[t, g] = _dg(eye_hd, vc_ref[rs, ls], NT).astype(BF16)

    for a in range(qrows // blk):
        q3 = q3_ref[a * blk:(a + 1) * blk, :]
        lane = lax.broadcasted_iota(I32, q3.shape, 1) & (IDX_PAD - 1)
        hi = q3.astype(BF16)
        lo = (q3 - hi.astype(F32)).astype(BF16)
        qp = jnp.where((lane >= IDX_HEAD_DIM) & (lane < 2 * IDX_HEAD_DIM), lo, hi)
        for h in range(IDX_HEADS):
            qp_scr[h * qrows + a * blk:h * qrows + (a + 1) * blk, :] = qp[:, h * IDX_PAD:(h + 1) * IDX_PAD]
    wsc = wt_ref[...] * ((IDX_HEAD_DIM ** -0.5) * (IDX_HEADS ** -0.5))

    kpos = lax.broadcasted_iota(I32, (kblk, qrows), 0)
    qpos = lax.broadcasted_iota(I32, (kblk, qrows), 1)
    chunk_gap = (kpos // CHUNK) - (qpos // CHUNK)
    q_chunk0 = i * (qrows // CHUNK)

    def fold_rows(x):
        return x.reshape(kblk // SUBLANES, SUBLANES, x.shape[-1])

    def score_blocks(js, carry):
        qp = qp_scr[...]
        s_alls = [_dg(kp_scr[pl.ds(pl.multiple_of(j * kblk, kblk), kblk), :], qp, NT) for j in js]
        for j, s_all in zip(js, s_alls):
            sc = jnp.zeros((kblk, qrows), F32)
            for h in range(IDX_HEADS):
                sc = sc + wsc[h:h + 1, :] * jnp.maximum(s_all[:, h * qrows:(h + 1) * qrows], 0.0)
            sc = sc + 0.0
            bits = pltpu.bitcast(sc, I32)
            key = jnp.where(bits < 0, bits ^ jnp.int32(0x7FFFFFFF), bits)
            slack = q_chunk0 - j * (kblk // CHUNK)
            key_scr[j] = jnp.where(chunk_gap <= slack, key, jnp.int32(INT_MIN))
        return carry

    _for_block_pairs(nkv, score_blocks, 0)

    def count_where(pred, ref):
        def body(j, acc):
            return acc + jnp.sum(fold_rows(jnp.where(pred(key_scr[j], ref), 1.0, 0.0)), axis=0)
        acc = lax.fori_loop(0, nkv, body, jnp.zeros((SUBLANES, qrows), F32))
        return jnp.sum(acc, axis=0, keepdims=True)

    ge = lambda kv, ref: kv >= ref
    cnt0 = count_where(ge, jnp.zeros((1, qrows), I32))
    cur0 = jnp.where(cnt0 >= kf, jnp.int32(0), jnp.int32(INT_MIN))

    def bit_body(t, cur):
        cand = cur | lax.shift_left(jnp.int32(1), jnp.int32(30) - t)
        return jnp.where(count_where(ge, cand) >= kf, cand, cur)

    thr = lax.fori_loop(0, 31, bit_body, cur0)
    need = kf - count_where(lambda kv, ref: kv > ref, thr)

    ku = lax.broadcasted_iota(I32, (kblk, kblk), 0)
    kv_ = lax.broadcasted_iota(I32, (kblk, kblk), 1)
    lower = jnp.where(ku >= kv_, 1.0, 0.0).astype(BF16)

    def sel_blocks(js, run):
        keys = [key_scr[j] for j in js]
        eqs = [key == thr for key in keys]
        pres = [_dg(lower, jnp.where(eq, 1.0, 0.0).astype(BF16)) for eq in eqs]
        for j, key, eq, pre in zip(js, keys, eqs, pres):
            take_tie = jnp.where(eq & (run + pre <= need), 0.0, NEG_BIG)
            take = jnp.where(key > thr, 0.0, take_tie)
            sel_scr[j] = jnp.where(key == jnp.int32(INT_MIN), NEG_BIG, take)
            run = run + pre[kblk - 1:kblk, :]
        return run

    _for_block_pairs(nkv, sel_blocks, jnp.zeros((1, qrows), F32))

    for a in range(qrows // blk):
        _dsa_attend(a, i * (qrows // blk) + a, nkv, qc_ref, zc_ref, qg_ref, tab_ref, o_ref,
                    kn_scr, vt_scr, sel_scr, qs_scr, lg_scr, mx_scr, ls_scr, acc_scr)


def _dsa_attend(a, qblock, nkv, qc_ref, zc_ref, qg_ref, tab_ref, o_ref,
                kn_scr, vt_scr, sel_scr, qs_scr, lg_scr, mx_scr, ls_scr, acc_scr):
    blk = DSA_BLOCK
    kblk = DSA_KBLOCK
    hd = ATT_HEAD_DIM
    cols4 = ATT_GROUP * blk
    rq = slice(a * blk, (a + 1) * blk)
    qc = qc_ref[rq, :].astype(F32)
    for h in range(ATT_HEADS):
        qh = qc[:, h * hd:(h + 1) * hd]
        qn = qh * lax.rsqrt(jnp.mean(qh * qh, axis=-1, keepdims=True) + EPS) * qg_ref[...]
        g, hh = divmod(h, ATT_GROUP)
        qs_scr[g, hh * blk:(hh + 1) * blk, :] = (qn * (hd ** -0.5)).astype(BF16)
    mx_scr[...] = jnp.full(mx_scr.shape, NEG_BIG, F32)

    def fold(x):
        return x.reshape(kblk // SUBLANES, SUBLANES, x.shape[-1])

    groups = range(ATT_KV_HEADS)

    def logit_blocks(js, carry):
        jg = [(n, g) for n in range(len(js)) for g in groups]
        kks = [kn_scr[pl.ds(pl.multiple_of(j * kblk, kblk), kblk), :] for j in js]
        raw = [_dg(kks[n][:, g * hd:(g + 1) * hd], qs_scr[g], NT) for n, g in jg]
        negs = [jnp.concatenate([sel_scr[j, :, rq]] * ATT_GROUP, axis=1) for j in js]
        kinds = [jnp.clip(j * (kblk // blk) - qblock + (DSA_BIAS_KINDS - 1), 0, DSA_BIAS_KINDS - 1)
                 for j in js]
        tops = [mx_scr[g] for g in groups]
        for (n, g), r in zip(jg, raw):
            bias = jnp.concatenate([tab_ref[kinds[n], g * ATT_GROUP + hh] for hh in range(ATT_GROUP)],
                                   axis=1)
            lg = r + bias + negs[n]
            lg_scr[js[n], g] = lg
            tops[g] = jnp.maximum(tops[g], jnp.max(fold(lg), axis=0))
        for g in groups:
            mx_scr[g] = tops[g]
        return carry

    _for_block_pairs(nkv, logit_blocks, 0)
    for g in groups:
        mx_scr[g] = jnp.broadcast_to(jnp.max(mx_scr[g], axis=0, keepdims=True), (SUBLANES, cols4))
    acc_scr[...] = jnp.zeros(acc_scr.shape, F32)
    ls_scr[...] = jnp.zeros(ls_scr.shape, F32)

    def pv_blocks(js, carry):
        jg = [(j, g) for j in js for g in groups]
        ps = [jnp.exp(lg_scr[j, g] - mx_scr[g, 0:1, :]) for j, g in jg]
        pvs = [_dg(vt_scr[j, g], p.astype(BF16)) for (j, g), p in zip(jg, ps)]
        for g in groups:
            acc_scr[g] += sum(pv for (_, gg), pv in zip(jg, pvs) if gg == g)
            ls_scr[g] += sum(jnp.sum(fold(p), axis=0) for (_, gg), p in zip(jg, ps) if gg == g)
        return carry

    _for_block_pairs(nkv, pv_blocks, 0)

    for h in range(ATT_HEADS):
        g, hh = divmod(h, ATT_GROUP)
        cs = slice(hh * blk, (hh + 1) * blk)
        ls = slice(h * hd, (h + 1) * hd)
        o_t = acc_scr[g, :, cs] / jnp.sum(ls_scr[g, :, cs], axis=0, keepdims=True)
        z = zc_ref[rq, ls].astype(F32)
        o_ref[rq, ls] = (o_t.T * _silu(z)).astype(o_ref.dtype)


def _dsa_mixer(p3, q3, k3, w_t, q_gain, k_gain, bias_tab):
    bsz, seq, _ = p3.shape
    blk = DSA_BLOCK
    kblk = DSA_KBLOCK
    qrows = DSA_QROWS
    nkb = seq // kblk
    rows4 = ATT_GROUP * blk
    topk = min(TOPK_MAX, seq // 4)
    return pl.pallas_call(
        functools.partial(_dsa_kernel, seq=seq, topk=topk),
        grid=(bsz, seq // qrows),
        in_specs=[
            pl.BlockSpec((None, qrows, IDX_HEADS * IDX_PAD), lambda b, i: (b, i, 0)),
            pl.BlockSpec((None, seq, IDX_PAD), lambda b, i: (b, 0, 0)),
            pl.BlockSpec((None, IDX_HEADS, qrows), lambda b, i: (b, 0, i)),
            pl.BlockSpec((None, qrows, ATT_WIDTH), lambda b, i: (b, i, P_Q_C // ATT_WIDTH)),
            pl.BlockSpec((None, qrows, ATT_WIDTH), lambda b, i: (b, i, P_Z_C // ATT_WIDTH)),
            pl.BlockSpec((None, seq, ATT_KV_WIDTH), lambda b, i: (b, 0, P_K_C // ATT_KV_WIDTH)),
            pl.BlockSpec((None, seq, ATT_KV_WIDTH), lambda b, i: (b, 0, P_V_C // ATT_KV_WIDTH)),
            pl.BlockSpec((1, ATT_HEAD_DIM), lambda b, i: (0, 0)),
            pl.BlockSpec((1, ATT_HEAD_DIM), lambda b, i: (0, 0)),
            pl.BlockSpec((DSA_BIAS_KINDS, ATT_HEADS, kblk, blk), lambda b, i: (0, 0, 0, 0)),
        ],
        out_specs=pl.BlockSpec((None, qrows, ATT_WIDTH), lambda b, i: (b, i, 0)),
        out_shape=jax.ShapeDtypeStruct((bsz, seq, ATT_WIDTH), BF16),
        scratch_shapes=[
            pltpu.VMEM((seq, IDX_PAD), BF16),
            pltpu.VMEM((seq, ATT_KV_WIDTH), BF16),
            pltpu.VMEM((nkb, ATT_KV_HEADS, ATT_HEAD_DIM, kblk), BF16),
            pltpu.VMEM((IDX_HEADS * qrows, IDX_PAD), BF16),
            pltpu.VMEM((nkb, kblk, qrows), I32),
            pltpu.VMEM((nkb, kblk, qrows), F32),
            pltpu.VMEM((ATT_KV_HEADS, rows4, ATT_HEAD_DIM), BF16),
            pltpu.VMEM((nkb, ATT_KV_HEADS, kblk, rows4), F32),
            pltpu.VMEM((ATT_KV_HEADS, SUBLANES, rows4), F32),
            pltpu.VMEM((ATT_KV_HEADS, SUBLANES, rows4), F32),
            pltpu.VMEM((ATT_KV_HEADS, ATT_HEAD_DIM, rows4), F32),
        ],
        compiler_params=_cparams(("arbitrary", "arbitrary")),
    )(q3, k3, w_t, p3, p3, p3, p3, q_gain, k_gain, bias_tab)


def _merge_kernel(ya_ref, yb_ref, yc_ref, wa_ref, wb_ref, wc_ref, ga_ref, gb_ref, gc_ref,
                  ba_ref, bb_ref, bc_ref, o_ref):
    acc = _sigmoid(ga_ref[...].astype(F32) + ba_ref[...]) * _dg(ya_ref[...], wa_ref[...])
    acc = acc + _sigmoid(gb_ref[...].astype(F32) + bb_ref[...]) * _dg(yb_ref[...], wb_ref[...])
    acc = acc + _sigmoid(gc_ref[...].astype(F32) + bc_ref[...]) * _dg(yc_ref[...], wc_ref[...])
    o_ref[...] = acc.astype(o_ref.dtype)


def _merge(ya, yb, yc, wa, wb, wc, p2, b_gate):
    t = ya.shape[0]
    d = D_MODEL
    tm, tn = min(t, 1024), 512
    nj = d // tn
    g0 = P_GATE // tn
    yspec = pl.BlockSpec((tm, ya.shape[1]), lambda i, j: (i, 0))
    wspec = pl.BlockSpec((wa.shape[0], tn), lambda i, j: (0, j))
    gspec = lambda br: pl.BlockSpec((tm, tn), lambda i, j, br=br: (i, g0 + br * nj + j))
    bspec = lambda br: pl.BlockSpec((1, tn), lambda i, j, br=br: (0, br * nj + j))
    return pl.pallas_call(
        _merge_kernel,
        grid=(t // tm, nj),
        in_specs=[yspec, yspec, yspec, wspec, wspec, wspec, gspec(0), gspec(1), gspec(2),
                  bspec(0), bspec(1), bspec(2)],
        out_specs=pl.BlockSpec((tm, tn), lambda i, j: (i, j)),
        out_shape=jax.ShapeDtypeStruct((t, d), BF16),
        compiler_params=_cparams(("parallel", "parallel")),
    )(ya, yb, yc, wa, wb, wc, p2, p2, p2, b_gate, b_gate, b_gate)


def _out_kernel(m_ref, w_ref, x_ref, g_ref, o_ref):
    o_ref[...] = x_ref[...] + g_ref[...] * _dg(m_ref[...], w_ref[...])


def _out_proj(merged, w_out, x2, mod, seq):
    t, d = x2.shape
    tm, tn = min(seq, 2048), 512
    per_batch = seq // tm
    nj = d // tn
    return pl.pallas_call(
        _out_kernel,
        grid=(t // tm, nj),
        in_specs=[
            pl.BlockSpec((tm, d), lambda i, j: (i, 0)),
            pl.BlockSpec((d, tn), lambda i, j: (0, j)),
            pl.BlockSpec((tm, tn), lambda i, j: (i, j)),
            pl.BlockSpec((None, 1, tn), lambda i, j: (i // per_batch, 0, 2 * nj + j)),
        ],
        out_specs=pl.BlockSpec((tm, tn), lambda i, j: (i, j)),
        out_shape=jax.ShapeDtypeStruct((t, d), F32),
        compiler_params=_cparams(("parallel", "parallel")),
    )(merged, w_out, x2, mod)


def _wide_weight(w):
    cols = [(_OFF_QKV_A, 3 * GDN_WIDTH), (_OFF_Z_A, GDN_WIDTH), (_OFF_GLU, 2 * CONV_WIDTH),
            (_OFF_Z_B, CONV_WIDTH), (_OFF_Q_C, ATT_WIDTH), (_OFF_Z_C, ATT_WIDTH),
            (_OFF_GATE, N_BRANCH * D_MODEL), (_OFF_K_C, ATT_KV_WIDTH), (_OFF_V_C, ATT_KV_WIDTH)]
    return jnp.concatenate([w[:, o:o + n].astype(BF16) for o, n in cols], axis=1)


def _narrow_weight(w):
    cols = [(_OFF_Q_I, IDX_HEADS * IDX_HEAD_DIM), (_OFF_K_I, IDX_HEAD_DIM), (_OFF_W_I, IDX_HEADS),
            (_OFF_BETA, GDN_HEADS), (_OFF_ALPHA, GDN_HEADS)]
    used = sum(n for _, n in cols)
    parts = [w[:, o:o + n] for o, n in cols] + [jnp.zeros((w.shape[0], NS_NARROW - used), w.dtype)]
    return jnp.concatenate(parts, axis=1).astype(BF16)


def _expand3(x, width):
    pad = jnp.zeros(x.shape[:-1] + (IDX_PAD - 3 * width,), x.dtype)
    y = jnp.concatenate([x, x, x, pad], axis=-1)
    return y.reshape(y.shape[:-2] + (-1,))


def kernel(x, c, rel_bias, w_ada, b_ada, norm_g, w_in, b_gate, conv_a_w, a_log, dt_bias, gdn_norm_g,
           w_proj_a, conv_b_w, conv_b_b, ln_b_g, ln_b_b, w_proj_b, q_norm_g, k_norm_g, w_proj_c, w_out):
    bsz, seq, d = x.shape
    t = bsz * seq
    nch = seq // CHUNK
    mod_all = _ada_modulation(c, w_ada, b_ada)
    bias_tab = _bias_tiles(rel_bias)
    for layer in range(DEPTH):
        mod = mod_all[layer].reshape(bsz, 1, 3 * d)
        h = _norm_modulate(x, norm_g[layer].reshape(1, d), mod).reshape(t, d)
        p2 = _matmul(h, _wide_weight(w_in[layer]), BF16, min(t, 2048), 512)
        s2 = _matmul(h, _narrow_weight(w_in[layer]), F32, min(t, 1024), NS_NARROW)
        p3 = p2.reshape(bsz, seq, NP_WIDE)
        s3 = s2.reshape(bsz, seq, NS_NARROW)

        def head_rows(off):
            r = s3[:, :, off:off + GDN_HEADS]
            return jnp.swapaxes(r, 1, 2).reshape(bsz, GDN_HEADS, nch, CHUNK)

        head_params = jnp.stack([a_log[layer], dt_bias[layer]], axis=1)
        y_a = _gdn_mixer(p3, head_rows(S_BETA), head_rows(S_ALPHA), conv_a_w[layer], head_params,
                         gdn_norm_g[layer].reshape(1, GDN_HEAD_DIM))
        y_b = _conformer_mixer(p3, conv_b_w[layer], conv_b_b[layer].reshape(1, CONV_WIDTH),
                               ln_b_g[layer].reshape(1, CONV_WIDTH), ln_b_b[layer].reshape(1, CONV_WIDTH))
        q3 = _expand3(s3[:, :, S_Q_I:S_K_I].reshape(bsz, seq, IDX_HEADS, IDX_HEAD_DIM), IDX_HEAD_DIM)
        k3 = _expand3(s3[:, :, S_K_I:S_W_I].reshape(bsz, seq, 1, IDX_HEAD_DIM), IDX_HEAD_DIM)
        y_c = _dsa_mixer(p3, q3, k3, jnp.swapaxes(s3[:, :, S_W_I:S_W_I + IDX_HEADS], 1, 2),
                         q_norm_g[layer].reshape(1, ATT_HEAD_DIM), k_norm_g[layer].reshape(1, ATT_HEAD_DIM),
                         bias_tab)
        merged = _merge(y_a.reshape(t, GDN_WIDTH), y_b.reshape(t, CONV_WIDTH), y_c.reshape(t, ATT_WIDTH),
                        w_proj_a[layer].astype(BF16), w_proj_b[layer].astype(BF16),
                        w_proj_c[layer].astype(BF16), p2, b_gate[layer].reshape(1, N_BRANCH * d))
        x = _out_proj(merged, w_out[layer].astype(BF16), x.reshape(t, d), mod, seq).reshape(bsz, seq, d)
    return x
```

```python
import functools
import math

import jax
import jax.numpy as jnp
from jax import lax
from jax.experimental import pallas as pl
from jax.experimental.pallas import tpu as pltpu

F32 = jnp.float32
BF16 = jnp.bfloat16
I32 = jnp.int32

D_MODEL = 2048
DEPTH = 4
CHUNK = 64
GDN_HEADS = 8
GDN_HEAD_DIM = 128
GDN_WIDTH = GDN_HEADS * GDN_HEAD_DIM
GDN_CONV = 4
CONV_WIDTH = 1024
CONV_KERNEL = 31
ATT_HEADS = 8
ATT_KV_HEADS = 2
ATT_HEAD_DIM = 128
ATT_WIDTH = ATT_HEADS * ATT_HEAD_DIM
ATT_KV_WIDTH = ATT_KV_HEADS * ATT_HEAD_DIM
ATT_GROUP = ATT_HEADS // ATT_KV_HEADS
IDX_HEADS = 8
IDX_HEAD_DIM = 64
TOPK_MAX = 256
REL_BUCKETS = 32
N_BRANCH = 3
EPS = 1e-6

_OFF_QKV_A = 0
_OFF_Z_A = _OFF_QKV_A + 3 * GDN_WIDTH
_OFF_BETA = _OFF_Z_A + GDN_WIDTH
_OFF_ALPHA = _OFF_BETA + GDN_HEADS
_OFF_GLU = _OFF_ALPHA + GDN_HEADS
_OFF_Z_B = _OFF_GLU + 2 * CONV_WIDTH
_OFF_Q_C = _OFF_Z_B + CONV_WIDTH
_OFF_K_C = _OFF_Q_C + ATT_WIDTH
_OFF_V_C = _OFF_K_C + ATT_KV_WIDTH
_OFF_Z_C = _OFF_V_C + ATT_KV_WIDTH
_OFF_Q_I = _OFF_Z_C + ATT_WIDTH
_OFF_K_I = _OFF_Q_I + IDX_HEADS * IDX_HEAD_DIM
_OFF_W_I = _OFF_K_I + IDX_HEAD_DIM
_OFF_GATE = _OFF_W_I + IDX_HEADS
N_IN = _OFF_GATE + N_BRANCH * D_MODEL

P_QKV_A = 0
P_Z_A = 3072
P_GLU = 4096
P_Z_B = 6144
P_Q_C = 7168
P_Z_C = 8192
P_GATE = 9216
P_K_C = 15360
P_V_C = 15616
NP_WIDE = 15872
S_Q_I = 0
S_K_I = 512
S_W_I = 576
S_BETA = 584
S_ALPHA = 592
NS_NARROW = 640

LANES = 128
SUBLANES = 8
DSA_BLOCK = 128
DSA_QROWS = 256
DSA_KBLOCK = 256
DSA_BIAS_KINDS = DSA_KBLOCK // DSA_BLOCK + 2
IDX_PAD = 256
VMEM_LIMIT = 56 * 1024 * 1024
NEG_BIG = -1e30
INT_MIN = -(2 ** 31)

NN = (((1,), (0,)), ((), ()))
NT = (((1,), (1,)), ((), ()))
TN = (((0,), (0,)), ((), ()))


def _dg(a, b, dims=NN):
    return lax.dot_general(a, b, dims, preferred_element_type=F32)


def _split2(x):
    hi = x.astype(BF16)
    lo = (x - hi.astype(F32)).astype(BF16)
    return hi, lo


def _dot_b(a, b, dims=NN):
    return _dg(a.astype(BF16), b.astype(BF16), dims)


def _dot_hp(a, b, dims=NN):
    ah, al = _split2(a)
    bh, bl = _split2(b)
    return _dg(ah, bh, dims) + (_dg(al, bh, dims) + _dg(ah, bl, dims))


def _dot_hp_exact_rhs(a, b_bf16, dims=NN):
    a0 = a.astype(BF16)
    r = a - a0.astype(F32)
    a1 = r.astype(BF16)
    a2 = (r - a1.astype(F32)).astype(BF16)
    return _dg(a0, b_bf16, dims) + (_dg(a1, b_bf16, dims) + _dg(a2, b_bf16, dims))


def _sigmoid(x):
    return 1.0 / (1.0 + jnp.exp(-x))


def _silu(x):
    return x * _sigmoid(x)


def _softplus(x):
    return jnp.maximum(x, 0.0) + jnp.log(1.0 + jnp.exp(-jnp.abs(x)))


def _cparams(sem):
    return pltpu.CompilerParams(dimension_semantics=sem, vmem_limit_bytes=VMEM_LIMIT)


def _ada_kernel(c_ref, w_ref, b_ref, o_ref):
    cond = _silu(c_ref[...])
    o_ref[...] = _dot_hp(cond, w_ref[...]) + b_ref[...]


def _ada_modulation(c, w_ada, b_ada):
    depth, d, n = w_ada.shape
    bsz = c.shape[0]
    tn = 512
    return pl.pallas_call(
        _ada_kernel,
        grid=(depth, n // tn),
        in_specs=[
            pl.BlockSpec((bsz, d), lambda l, j: (0, 0)),
            pl.BlockSpec((None, d, tn), lambda l, j: (l, 0, j)),
            pl.BlockSpec((None, 1, tn), lambda l, j: (l, 0, j)),
        ],
        out_specs=pl.BlockSpec((None, bsz, tn), lambda l, j: (l, 0, j)),
        out_shape=jax.ShapeDtypeStruct((depth, bsz, n), F32),
        compiler_params=_cparams(("parallel", "parallel")),
    )(c, w_ada, b_ada.reshape(depth, 1, n))


def _norm_kernel(x_ref, g_ref, sc_ref, sh_ref, o_ref):
    x = x_ref[...]
    ms = jnp.mean(x * x, axis=-1, keepdims=True)
    y = x * lax.rsqrt(ms + EPS) * g_ref[...]
    o_ref[...] = (y * (1.0 + sc_ref[...]) + sh_ref[...]).astype(o_ref.dtype)


def _norm_modulate(x, g, mod):
    bsz, seq, d = x.shape
    tl = 512
    return pl.pallas_call(
        _norm_kernel,
        grid=(bsz, seq // tl),
        in_specs=[
            pl.BlockSpec((None, tl, d), lambda b, i: (b, i, 0)),
            pl.BlockSpec((1, d), lambda b, i: (0, 0)),
            pl.BlockSpec((None, 1, d), lambda b, i: (b, 0, 1)),
            pl.BlockSpec((None, 1, d), lambda b, i: (b, 0, 0)),
        ],
        out_specs=pl.BlockSpec((None, tl, d), lambda b, i: (b, i, 0)),
        out_shape=jax.ShapeDtypeStruct((bsz, seq, d), BF16),
        compiler_params=_cparams(("parallel", "parallel")),
    )(x, g, mod, mod)


def _mm_kernel(a_ref, b_ref, o_ref):
    o_ref[...] = _dg(a_ref[...], b_ref[...]).astype(o_ref.dtype)


def _matmul(a, b, out_dtype, tm, tn):
    m, k = a.shape
    n = b.shape[1]
    return pl.pallas_call(
        _mm_kernel,
        grid=(m // tm, n // tn),
        in_specs=[
            pl.BlockSpec((tm, k), lambda i, j: (i, 0)),
            pl.BlockSpec((k, tn), lambda i, j: (0, j)),
        ],
        out_specs=pl.BlockSpec((tm, tn), lambda i, j: (i, j)),
        out_shape=jax.ShapeDtypeStruct((m, n), out_dtype),
        compiler_params=_cparams(("parallel", "parallel")),
    )(a, b)


def _tri_inverse(mats, eye):
    n = mats[0].shape[0]
    levels = CHUNK.bit_length() - 1
    bs = [-a for a in mats]
    ss = [eye + b for b in bs]
    ps = [_dot_b(b, b) for b in bs]
    for _ in range(1, levels - 1):
        prods = [_dot_b(jnp.concatenate([s, p], axis=0), p) for s, p in zip(ss, ps)]
        ss = [s + pr[:n] for s, pr in zip(ss, prods)]
        ps = [pr[n:] for pr in prods]
    return [s + _dot_b(s, p) for s, p in zip(ss, ps)]


GDN_UNROLL = 8


def _gdn_kernel(hp_ref, q_ref, k_ref, v_ref, z_ref, cwq_ref, cwk_ref, cwv_ref, br_ref, ar_ref,
                gn_ref, o_ref, pad_scr, qn_scr, kn_scr, vn_scr, gc_scr, beta_scr,
                lhs_scr, cadd_scr, *, hb, seq):
    hd = GDN_HEAD_DIM
    nch = seq // CHUNK
    head0 = pl.program_id(1) * hb
    row_tile = min(seq, 256)
    unroll = math.gcd(GDN_UNROLL, nch)

    ii = lax.broadcasted_iota(I32, (CHUNK, CHUNK), 0)
    jj = lax.broadcasted_iota(I32, (CHUNK, CHUNK), 1)
    incl = ii >= jj
    strict = ii > jj
    eye = jnp.where(ii == jj, 1.0, 0.0).astype(F32)
    upper = jnp.where(ii <= jj, 1.0, 0.0).astype(BF16)

    def row_to_col(row):
        return jnp.sum(jnp.where(ii == jj, jnp.broadcast_to(row, (CHUNK, CHUNK)), 0.0),
                       axis=1, keepdims=True)

    for hh in range(hb):
        ls = slice(hh * hd, (hh + 1) * hd)
        for src, cw, dst, mode in ((q_ref, cwq_ref, qn_scr, "q"), (k_ref, cwk_ref, kn_scr, "k"),
                                   (v_ref, cwv_ref, vn_scr, "v")):
            pad_scr[0:8, :] = jnp.zeros((8, hd), F32)
            pad_scr[8:8 + seq, :] = src[:, ls].astype(F32)
            for rt in range(seq // row_tile):
                base = rt * row_tile
                acc = jnp.zeros((row_tile, hd), F32)
                for kk in range(GDN_CONV):
                    off = base + 8 - (GDN_CONV - 1) + kk
                    acc = acc + cw[kk:kk + 1, ls] * pad_scr[off:off + row_tile, :]
                y = _silu(acc)
                if mode != "v":
                    y = y * lax.rsqrt(jnp.sum(y * y, axis=-1, keepdims=True) + EPS)
                if mode == "q":
                    y = y * (hd ** -0.5)
                dst[hh, base:base + row_tile, :] = y
        a_log = hp_ref[head0 + hh, 0]
        dt_b = hp_ref[head0 + hh, 1]
        g2 = -jnp.exp(jnp.full((nch, CHUNK), a_log, F32)) * _softplus(ar_ref[hh] + dt_b)
        gc_scr[hh] = _dot_hp_exact_rhs(g2, upper)
        beta_scr[hh] = _sigmoid(br_ref[hh])

    i_hd = lax.broadcasted_iota(I32, (hd, hd), 0)
    j_hd = lax.broadcasted_iota(I32, (hd, hd), 1)
    eye_hd = jnp.where(i_hd == j_hd, 1.0, 0.0).astype(BF16)

    def load_chunk(n, hh):
        r0 = pl.multiple_of(n * CHUNK, CHUNK)
        return (qn_scr[hh, pl.ds(r0, CHUNK), :], kn_scr[hh, pl.ds(r0, CHUNK), :],
                vn_scr[hh, pl.ds(r0, CHUNK), :], gc_scr[hh, pl.ds(n, 1), :],
                beta_scr[hh, pl.ds(n, 1), :])

    def compute_chunks(loaded):
        qs, ks, vs, gc_rows, beta_rows = zip(*loaded)
        gc_cols = [row_to_col(r) for r in gc_rows]
        beta_cols = [row_to_col(r) for r in beta_rows]
        decays = [jnp.where(incl, jnp.exp(jnp.where(incl, c - r, 0.0)), 0.0)
                  for c, r in zip(gc_cols, gc_rows)]
        k16s = [k.astype(BF16) for k in ks]
        qk_kks = [_dg(jnp.concatenate([q.astype(BF16), k16], axis=0), k16, NT)
                  for q, k16 in zip(qs, k16s)]
        intras = [jnp.where(incl, m[:CHUNK] * d, 0.0).astype(BF16) for m, d in zip(qk_kks, decays)]
        t_invs = _tri_inverse([jnp.where(strict, b * m[CHUNK:] * d, 0.0)
                               for b, m, d in zip(beta_cols, qk_kks, decays)], eye)
        e_gcs = [jnp.exp(c) for c in gc_cols]
        uws = [_dot_b(t, jnp.concatenate([v * b, k * (b * e)], axis=1))
               for t, v, k, b, e in zip(t_invs, vs, ks, beta_cols, e_gcs)]
        kdts = [_dg(eye_hd, (k * jnp.exp(r[:, CHUNK - 1:CHUNK] - c)).astype(BF16), NT).astype(BF16)
                for k, r, c in zip(ks, gc_rows, gc_cols)]
        wu16s = [jnp.concatenate([uw[:, hd:], uw[:, :hd]], axis=1).astype(BF16) for uw in uws]
        kwcs = [_dg(kdt, wu) for kdt, wu in zip(kdts, wu16s)]
        iwus = [_dg(intra, wu) for intra, wu in zip(intras, wu16s)]
        lhs = [jnp.concatenate([kwc[:, :hd], q * e - iwu[:, :hd]], axis=0).astype(BF16)
               for kwc, q, e, iwu in zip(kwcs, qs, e_gcs, iwus)]
        return [(l, kwc[:, hd:], iwu[:, hd:]) for l, kwc, iwu in zip(lhs, kwcs, iwus)]

    def store_chunk(n, hh, lhs, c_add, o_add):
        lhs_scr[hh, n] = lhs
        cadd_scr[hh, n] = c_add
        vn_scr[hh, pl.ds(pl.multiple_of(n * CHUNK, CHUNK), CHUNK), :] = o_add

    def prepare_body(step, carry):
        jobs = [(step * unroll + cu, hh) for cu in range(unroll) for hh in range(hb)]
        results = compute_chunks([load_chunk(n, hh) for n, hh in jobs])
        for (n, hh), res in zip(jobs, results):
            store_chunk(n, hh, *res)
        return carry

    lax.fori_loop(0, nch // unroll, prepare_body, 0)

    def scan_body(n, states):
        r0 = pl.multiple_of(n * CHUNK, CHUNK)
        loaded = [(lhs_scr[hh, n], cadd_scr[hh, n], vn_scr[hh, pl.ds(r0, CHUNK), :],
                   gc_scr[hh, pl.ds(n, 1), CHUNK - 1:CHUNK]) for hh in range(hb)]
        lhss, c_adds, o_adds, g_lasts = zip(*loaded)
        prods = [_dg(l, s.astype(BF16)) for l, s in zip(lhss, states)]
        new_states = [s * jnp.exp(g) - p[:hd] + c for s, g, p, c in zip(states, g_lasts, prods, c_adds)]
        for hh in range(hb):
            qn_scr[hh, pl.ds(r0, CHUNK), :] = prods[hh][hd:] + o_adds[hh]
        return tuple(new_states)

    lax.fori_loop(0, nch, scan_body, tuple(jnp.zeros((hd, hd), F32) for _ in range(hb)))

    for hh in range(hb):
        ls = slice(hh * hd, (hh + 1) * hd)
        for rt in range(seq // row_tile):
            rs = slice(rt * row_tile, (rt + 1) * row_tile)
            o = qn_scr[hh, rs, :]
            on = o * lax.rsqrt(jnp.mean(o * o, axis=-1, keepdims=True) + EPS) * gn_ref[...]
            o_ref[rs, ls] = (on * _silu(z_ref[rs, ls].astype(F32))).astype(o_ref.dtype)


def _gdn_mixer(p3, beta_r, alpha_r, conv_w, head_params, gn, hb=2):
    bsz, seq, _ = p3.shape
    wb = hb * GDN_HEAD_DIM
    nhb = GDN_HEADS // hb
    nch = seq // CHUNK
    qb, kb_, vb, zb = (P_QKV_A // wb, (P_QKV_A + GDN_WIDTH) // wb, (P_QKV_A + 2 * GDN_WIDTH) // wb,
                       P_Z_A // wb)
    col = lambda off: pl.BlockSpec((None, seq, wb), lambda b, h, off=off: (b, 0, off + h))
    cwspec = lambda off: pl.BlockSpec((GDN_CONV, wb), lambda b, h, off=off: (0, off + h))
    rows = pl.BlockSpec((None, hb, nch, CHUNK), lambda b, h: (b, h, 0, 0))
    return pl.pallas_call(
        functools.partial(_gdn_kernel, hb=hb, seq=seq),
        grid=(bsz, nhb),
        in_specs=[
            pl.BlockSpec(memory_space=pltpu.SMEM),
            col(qb), col(kb_), col(vb), col(zb),
            cwspec(0), cwspec(nhb), cwspec(2 * nhb),
            rows, rows,
            pl.BlockSpec((1, GDN_HEAD_DIM), lambda b, h: (0, 0)),
        ],
        out_specs=pl.BlockSpec((None, seq, wb), lambda b, h: (b, 0, h)),
        out_shape=jax.ShapeDtypeStruct((bsz, seq, GDN_WIDTH), BF16),
        scratch_shapes=[
            pltpu.VMEM((seq + 8, GDN_HEAD_DIM), F32),
            pltpu.VMEM((hb, seq, GDN_HEAD_DIM), F32),
            pltpu.VMEM((hb, seq, GDN_HEAD_DIM), F32),
            pltpu.VMEM((hb, seq, GDN_HEAD_DIM), F32),
            pltpu.VMEM((hb, nch, CHUNK), F32),
            pltpu.VMEM((hb, nch, CHUNK), F32),
            pltpu.VMEM((hb, nch, GDN_HEAD_DIM + CHUNK, GDN_HEAD_DIM), BF16),
            pltpu.VMEM((hb, nch, GDN_HEAD_DIM, GDN_HEAD_DIM), F32),
        ],
        compiler_params=_cparams(("parallel", "parallel")),
    )(head_params, p3, p3, p3, p3, conv_w, conv_w, conv_w, beta_r, alpha_r, gn)


CONF_HALO = 32
CONF_ROWS = 32
CONF_COPY_ROWS = 64


def _conf_kernel(a_ref, b_ref, ap_ref, bp_ref, z_ref, w_ref, cb_ref, lg_ref, lb_ref, o_ref, glu_scr,
                 *, tl):
    i = pl.program_id(1)
    rows = CONF_HALO + tl
    for rt in range(tl // CONF_COPY_ROWS):
        rs = slice(rt * CONF_COPY_ROWS, (rt + 1) * CONF_COPY_ROWS)
        glu_scr[0, CONF_HALO + rs.start:CONF_HALO + rs.stop, :] = (
            a_ref[rs, :].astype(F32) * _sigmoid(b_ref[rs, :].astype(F32)))
    prev = ap_ref[...].astype(F32) * _sigmoid(bp_ref[...].astype(F32))
    glu_scr[0, 0:CONF_HALO, :] = jnp.where(i > 0, prev, 0.0)
    for s in range(1, SUBLANES):
        for r0 in range(0, rows - SUBLANES, CONF_COPY_ROWS):
            n = min(CONF_COPY_ROWS, rows - SUBLANES - r0)
            glu_scr[s, r0:r0 + n, :] = glu_scr[0, r0 + s:r0 + s + n, :]
    groups = CONF_ROWS // SUBLANES
    for rt in range(tl // CONF_ROWS):
        base = rt * CONF_ROWS
        acc = jnp.broadcast_to(cb_ref[...], (groups, SUBLANES, CONV_WIDTH))
        for kk in range(CONV_KERNEL):
            off = base + CONF_HALO - (CONV_KERNEL - 1) + kk
            s = off % SUBLANES
            win = glu_scr[s, off - s:off - s + CONF_ROWS, :].reshape(groups, SUBLANES, CONV_WIDTH)
            acc = acc + w_ref[kk][None] * win
        acc = acc.reshape(CONF_ROWS, CONV_WIDTH)
        mu = jnp.mean(acc, axis=-1, keepdims=True)
        cen = acc - mu
        var = jnp.mean(cen * cen, axis=-1, keepdims=True)
        y = cen * lax.rsqrt(var + EPS) * lg_ref[...] + lb_ref[...]
        z = z_ref[base:base + CONF_ROWS, :].astype(F32)
        o_ref[base:base + CONF_ROWS, :] = (_silu(y) * _silu(z)).astype(o_ref.dtype)


def _conformer_mixer(p3, conv_w, conv_b, ln_g, ln_b):
    bsz, seq, _ = p3.shape
    tl = 256
    cw = CONV_WIDTH
    ca, cb, cz = P_GLU // cw, P_GLU // cw + 1, P_Z_B // cw
    halo_per_tile = tl // CONF_HALO
    cur = lambda c: pl.BlockSpec((None, tl, cw), lambda b, i, c=c: (b, i, c))
    prev = lambda c: pl.BlockSpec(
        (None, CONF_HALO, cw), lambda b, i, c=c: (b, jnp.maximum(i * halo_per_tile - 1, 0), c))
    vec = pl.BlockSpec((1, cw), lambda b, i: (0, 0))
    return pl.pallas_call(
        functools.partial(_conf_kernel, tl=tl),
        grid=(bsz, seq // tl),
        in_specs=[cur(ca), cur(cb), prev(ca), prev(cb), cur(cz),
                  pl.BlockSpec((CONV_KERNEL, SUBLANES, cw), lambda b, i: (0, 0, 0)), vec, vec, vec],
        out_specs=pl.BlockSpec((None, tl, cw), lambda b, i: (b, i, 0)),
        out_shape=jax.ShapeDtypeStruct((bsz, seq, cw), BF16),
        scratch_shapes=[pltpu.VMEM((SUBLANES, CONF_HALO + tl, cw), F32)],
        compiler_params=_cparams(("parallel", "parallel")),
    )(p3, p3, p3, p3, p3, jnp.broadcast_to(conv_w[:, None, :], (CONV_KERNEL, SUBLANES, cw)), conv_b, ln_g, ln_b)


def _t5_bucket_int(rel):
    nb = REL_BUCKETS // 2
    max_exact = nb // 2
    n = jnp.abs(rel)
    large = jnp.full(rel.shape, max_exact, I32)
    for thr in (12, 16, 23, 32, 46, 64, 91):
        large = large + jnp.where(n >= thr, 1, 0)
    large = jnp.minimum(large, nb - 1)
    return jnp.where(rel > 0, nb, 0) + jnp.where(n < max_exact, n, large)


def _bias_kernel(rb_ref, o_ref):
    kpos = lax.broadcasted_iota(I32, (DSA_KBLOCK, DSA_BLOCK), 0)
    qpos = lax.broadcasted_iota(I32, (DSA_KBLOCK, DSA_BLOCK), 1)
    for kind in range(DSA_BIAS_KINDS):
        bucket = _t5_bucket_int(kpos - qpos + (kind - (DSA_BIAS_KINDS - 1)) * DSA_BLOCK)
        for h in range(ATT_HEADS):
            val = jnp.zeros((DSA_KBLOCK, DSA_BLOCK), F32)
            for b in range(REL_BUCKETS):
                val = jnp.where(bucket == b, rb_ref[b, h], val)
            o_ref[kind, h] = val


def _bias_tiles(rel_bias):
    return pl.pallas_call(
        _bias_kernel,
        in_specs=[pl.BlockSpec(memory_space=pltpu.SMEM)],
        out_shape=jax.ShapeDtypeStruct((DSA_BIAS_KINDS, ATT_HEADS, DSA_KBLOCK, DSA_BLOCK), F32),
    )(rel_bias)


def _for_block_pairs(n, fn, carry):
    def pair(p, c):
        return fn([2 * p, 2 * p + 1], c)

    c = lax.fori_loop(0, lax.shift_right_logical(n, 1), pair, carry)

    @pl.when((n & 1) == 1)
    def _last():
        fn([n - 1], c)


def _dsa_kernel(q3_ref, k3_ref, wt_ref, qc_ref, zc_ref, kc_ref, vc_ref, qg_ref, kg_ref, tab_ref,
                o_ref, kp_scr, kn_scr, vt_scr, qp_scr, key_scr, sel_scr, qs_scr, lg_scr, mx_scr, ls_scr,
                acc_scr, *, seq, topk):
    blk = DSA_BLOCK
    qrows = DSA_QROWS
    kblk = DSA_KBLOCK
    hd = ATT_HEAD_DIM
    i = pl.program_id(1)
    nkv = (i * qrows) // kblk + 1
    kf = float(topk)

    @pl.when(i == 0)
    def _prepare_keys():
        e_r = lax.broadcasted_iota(I32, (hd, hd), 0)
        e_c = lax.broadcasted_iota(I32, (hd, hd), 1)
        eye_hd = jnp.where(e_r == e_c, 1.0, 0.0).astype(BF16)
        for t in range(seq // kblk):
            rs = slice(t * kblk, (t + 1) * kblk)
            k3 = k3_ref[rs, :]
            lane = lax.broadcasted_iota(I32, k3.shape, 1)
            hi = k3.astype(BF16)
            lo = (k3 - hi.astype(F32)).astype(BF16)
            kp_scr[rs, :] = jnp.where(lane >= 2 * IDX_HEAD_DIM, lo, hi)
            kc = kc_ref[rs, :].astype(F32)
            for g in range(ATT_KV_HEADS):
                ls = slice(g * hd, (g + 1) * hd)
                kg = kc[:, ls]
                kgn = kg * lax.rsqrt(jnp.mean(kg * kg, axis=-1, keepdims=True) + EPS) * kg_ref[...]
                kn_scr[rs, ls] = kgn.astype(BF16)
                vt_scr[t, g] = _dg(eye_hd, vc_ref[rs, ls], NT).astype(BF16)

    for a in range(qrows // blk):
        q3 = q3_ref[a * blk:(a + 1) * blk, :]
        lane = lax.broadcasted_iota(I32, q3.shape, 1) & (IDX_PAD - 1)
        hi = q3.astype(BF16)
        lo = (q3 - hi.astype(F32)).astype(BF16)
        qp = jnp.where((lane >= IDX_HEAD_DIM) & (lane < 2 * IDX_HEAD_DIM), lo, hi)
        for h in range(IDX_HEADS):
            qp_scr[h * qrows + a * blk:h * qrows + (a + 1) * blk, :] = qp[:, h * IDX_PAD:(h + 1) * IDX_PAD]
    wsc = wt_ref[...] * ((IDX_HEAD_DIM ** -0.5) * (IDX_HEADS ** -0.5))

    kpos = lax.broadcasted_iota(I32, (kblk, qrows), 0)
    qpos = lax.broadcasted_iota(I32, (kblk, qrows), 1)
    chunk_gap = (kpos // CHUNK) - (qpos // CHUNK)
    q_chunk0 = i * (qrows // CHUNK)

    def fold_rows(x):
        return x.reshape(kblk // SUBLANES, SUBLANES, x.shape[-1])

    def score_blocks(js, carry):
        qp = qp_scr[...]
        s_alls = [_dg(kp_scr[pl.ds(pl.multiple_of(j * kblk, kblk), kblk), :], qp, NT) for j in js]
        for j, s_all in zip(js, s_alls):
            sc = jnp.zeros((kblk, qrows), F32)
            for h in range(IDX_HEADS):
                sc = sc + wsc[h:h + 1, :] * jnp.maximum(s_all[:, h * qrows:(h + 1) * qrows], 0.0)
            sc = sc + 0.0
            bits = pltpu.bitcast(sc, I32)
            key = jnp.where(bits < 0, bits ^ jnp.int32(0x7FFFFFFF), bits)
            slack = q_chunk0 - j * (kblk // CHUNK)
            key_scr[j] = jnp.where(chunk_gap <= slack, key, jnp.int32(INT_MIN))
        return carry

    _for_block_pairs(nkv, score_blocks, 0)

    def count_where(pred, ref):
        def body(j, acc):
            return acc + jnp.sum(fold_rows(jnp.where(pred(key_scr[j], ref), 1.0, 0.0)), axis=0)
        acc = lax.fori_loop(0, nkv, body, jnp.zeros((SUBLANES, qrows), F32))
        return jnp.sum(acc, axis=0, keepdims=True)

    ge = lambda kv, ref: kv >= ref
    cnt0 = count_where(ge, jnp.zeros((1, qrows), I32))
    cur0 = jnp.where(cnt0 >= kf, jnp.int32(0), jnp.int32(INT_MIN))

    def bit_body(t, cur):
        cand = cur | lax.shift_left(jnp.int32(1), jnp.int32(30) - t)
        return jnp.where(count_where(ge, cand) >= kf, cand, cur)

    thr = lax.fori_loop(0, 31, bit_body, cur0)
    need = kf - count_where(lambda kv, ref: kv > ref, thr)

    ku = lax.broadcasted_iota(I32, (kblk, kblk), 0)
    kv_ = lax.broadcasted_iota(I32, (kblk, kblk), 1)
    lower = jnp.where(ku >= kv_, 1.0, 0.0).astype(BF16)

    def sel_blocks(js, run):
        keys = [key_scr[j] for j in js]
        eqs = [key == thr for key in keys]
        pres = [_dg(lower, jnp.where(eq, 1.0, 0.0).astype(BF16)) for eq in eqs]
        for j, key, eq, pre in zip(js, keys, eqs, pres):
            take_tie = jnp.where(eq & (run + pre <= need), 0.0, NEG_BIG)
            take = jnp.where(key > thr, 0.0, take_tie)
            sel_scr[j] = jnp.where(key == jnp.int32(INT_MIN), NEG_BIG, take)
            run = run + pre[kblk - 1:kblk, :]
        return run

    _for_block_pairs(nkv, sel_blocks, jnp.zeros((1, qrows), F32))

    for a in range(qrows // blk):
        _dsa_attend(a, i * (qrows // blk) + a, nkv, qc_ref, zc_ref, qg_ref, tab_ref, o_ref,
                    kn_scr, vt_scr, sel_scr, qs_scr, lg_scr, mx_scr, ls_scr, acc_scr)


def _dsa_attend(a, qblock, nkv, qc_ref, zc_ref, qg_ref, tab_ref, o_ref,
                kn_scr, vt_scr, sel_scr, qs_scr, lg_scr, mx_scr, ls_scr, acc_scr):
    blk = DSA_BLOCK
    kblk = DSA_KBLOCK
    hd = ATT_HEAD_DIM
    cols4 = ATT_GROUP * blk
    rq = slice(a * blk, (a + 1) * blk)
    qc = qc_ref[rq, :].astype(F32)
    for h in range(ATT_HEADS):
        qh = qc[:, h * hd:(h + 1) * hd]
        qn = qh * lax.rsqrt(jnp.mean(qh * qh, axis=-1, keepdims=True) + EPS) * qg_ref[...]
        g, hh = divmod(h, ATT_GROUP)
        qs_scr[g, hh * blk:(hh + 1) * blk, :] = (qn * (hd ** -0.5)).astype(BF16)
    mx_scr[...] = jnp.full(mx_scr.shape, NEG_BIG, F32)

    def fold(x):
        return x.reshape(kblk // SUBLANES, SUBLANES, x.shape[-1])

    groups = range(ATT_KV_HEADS)

    def logit_blocks(js, carry):
        jg = [(n, g) for n in range(len(js)) for g in groups]
        kks = [kn_scr[pl.ds(pl.multiple_of(j * kblk, kblk), kblk), :] for j in js]
        raw = [_dg(kks[n][:, g * hd:(g + 1) * hd], qs_scr[g], NT) for n, g in jg]
        negs = [jnp.concatenate([sel_scr[j, :, rq]] * ATT_GROUP, axis=1) for j in js]
        kinds = [jnp.clip(j * (kblk // blk) - qblock + (DSA_BIAS_KINDS - 1), 0, DSA_BIAS_KINDS - 1)
                 for j in js]
        tops = [mx_scr[g] for g in groups]
        for (n, g), r in zip(jg, raw):
            bias = jnp.concatenate([tab_ref[kinds[n], g * ATT_GROUP + hh] for hh in range(ATT_GROUP)],
                                   axis=1)
            lg = r + bias + negs[n]
            lg_scr[js[n], g] = lg
            tops[g] = jnp.maximum(tops[g], jnp.max(fold(lg), axis=0))
        for g in groups:
            mx_scr[g] = tops[g]
        return carry

    _for_block_pairs(nkv, logit_blocks, 0)
    for g in groups:
        mx_scr[g] = jnp.broadcast_to(jnp.max(mx_scr[g], axis=0, keepdims=True), (SUBLANES, cols4))
    acc_scr[...] = jnp.zeros(acc_scr.shape, F32)
    ls_scr[...] = jnp.zeros(ls_scr.shape, F32)

    def pv_blocks(js, carry):
        jg = [(j, g) for j in js for g in groups]
        ps = [jnp.exp(lg_scr[j, g] - mx_scr[g, 0:1, :]) for j, g in jg]
        pvs = [_dg(vt_scr[j, g], p.astype(BF16)) for (j, g), p in zip(jg, ps)]
        for g in groups:
            acc_scr[g] += sum(pv for (_, gg), pv in zip(jg, pvs) if gg == g)
            ls_scr[g] += sum(jnp.sum(fold(p), axis=0) for (_, gg), p in zip(jg, ps) if gg == g)
        return carry

    _for_block_pairs(nkv, pv_blocks, 0)

    for h in range(ATT_HEADS):
        g, hh = divmod(h, ATT_GROUP)
        cs = slice(hh * blk, (hh + 1) * blk)
        ls = slice(h * hd, (h + 1) * hd)
        o_t = acc_scr[g, :, cs] / jnp.sum(ls_scr[g, :, cs], axis=0, keepdims=True)
        z = zc_ref[rq, ls].astype(F32)
        o_ref[rq, ls] = (o_t.T * _silu(z)).astype(o_ref.dtype)


def _dsa_mixer(p3, q3, k3, w_t, q_gain, k_gain, bias_tab):
    bsz, seq, _ = p3.shape
    blk = DSA_BLOCK
    kblk = DSA_KBLOCK
    qrows = DSA_QROWS
    nkb = seq // kblk
    rows4 = ATT_GROUP * blk
    topk = min(TOPK_MAX, seq // 4)
    return pl.pallas_call(
        functools.partial(_dsa_kernel, seq=seq, topk=topk),
        grid=(bsz, seq // qrows),
        in_specs=[
            pl.BlockSpec((None, qrows, IDX_HEADS * IDX_PAD), lambda b, i: (b, i, 0)),
            pl.BlockSpec((None, seq, IDX_PAD), lambda b, i: (b, 0, 0)),
            pl.BlockSpec((None, IDX_HEADS, qrows), lambda b, i: (b, 0, i)),
            pl.BlockSpec((None, qrows, ATT_WIDTH), lambda b, i: (b, i, P_Q_C // ATT_WIDTH)),
            pl.BlockSpec((None, qrows, ATT_WIDTH), lambda b, i: (b, i, P_Z_C // ATT_WIDTH)),
            pl.BlockSpec((None, seq, ATT_KV_WIDTH), lambda b, i: (b, 0, P_K_C // ATT_KV_WIDTH)),
            pl.BlockSpec((None, seq, ATT_KV_WIDTH), lambda b, i: (b, 0, P_V_C // ATT_KV_WIDTH)),
            pl.BlockSpec((1, ATT_HEAD_DIM), lambda b, i: (0, 0)),
            pl.BlockSpec((1, ATT_HEAD_DIM), lambda b, i: (0, 0)),
            pl.BlockSpec((DSA_BIAS_KINDS, ATT_HEADS, kblk, blk), lambda b, i: (0, 0, 0, 0)),
        ],
        out_specs=pl.BlockSpec((None, qrows, ATT_WIDTH), lambda b, i: (b, i, 0)),
        out_shape=jax.ShapeDtypeStruct((bsz, seq, ATT_WIDTH), BF16),
        scratch_shapes=[
            pltpu.VMEM((seq, IDX_PAD), BF16),
            pltpu.VMEM((seq, ATT_KV_WIDTH), BF16),
            pltpu.VMEM((nkb, ATT_KV_HEADS, ATT_HEAD_DIM, kblk), BF16),
            pltpu.VMEM((IDX_HEADS * qrows, IDX_PAD), BF16),
            pltpu.VMEM((nkb, kblk, qrows), I32),
            pltpu.VMEM((nkb, kblk, qrows), F32),
            pltpu.VMEM((ATT_KV_HEADS, rows4, ATT_HEAD_DIM), BF16),
            pltpu.VMEM((nkb, ATT_KV_HEADS, kblk, rows4), F32),
            pltpu.VMEM((ATT_KV_HEADS, SUBLANES, rows4), F32),
            pltpu.VMEM((ATT_KV_HEADS, SUBLANES, rows4), F32),
            pltpu.VMEM((ATT_KV_HEADS, ATT_HEAD_DIM, rows4), F32),
        ],
        compiler_params=_cparams(("arbitrary", "arbitrary")),
    )(q3, k3, w_t, p3, p3, p3, p3, q_gain, k_gain, bias_tab)


def _merge_kernel(ya_ref, yb_ref, yc_ref, wa_ref, wb_ref, wc_ref, ga_ref, gb_ref, gc_ref,
                  ba_ref, bb_ref, bc_ref, o_ref):
    acc = _sigmoid(ga_ref[...].astype(F32) + ba_ref[...]) * _dg(ya_ref[...], wa_ref[...])
    acc = acc + _sigmoid(gb_ref[...].astype(F32) + bb_ref[...]) * _dg(yb_ref[...], wb_ref[...])
    acc = acc + _sigmoid(gc_ref[...].astype(F32) + bc_ref[...]) * _dg(yc_ref[...], wc_ref[...])
    o_ref[...] = acc.astype(o_ref.dtype)


def _merge(ya, yb, yc, wa, wb, wc, p2, b_gate):
    t = ya.shape[0]
    d = D_MODEL
    tm, tn = min(t, 1024), 512
    nj = d // tn
    g0 = P_GATE // tn
    yspec = pl.BlockSpec((tm, ya.shape[1]), lambda i, j: (i, 0))
    wspec = pl.BlockSpec((wa.shape[0], tn), lambda i, j: (0, j))
    gspec = lambda br: pl.BlockSpec((tm, tn), lambda i, j, br=br: (i, g0 + br * nj + j))
    bspec = lambda br: pl.BlockSpec((1, tn), lambda i, j, br=br: (0, br * nj + j))
    return pl.pallas_call(
        _merge_kernel,
        grid=(t // tm, nj),
        in_specs=[yspec, yspec, yspec, wspec, wspec, wspec, gspec(0), gspec(1), gspec(2),
                  bspec(0), bspec(1), bspec(2)],
        out_specs=pl.BlockSpec((tm, tn), lambda i, j: (i, j)),
        out_shape=jax.ShapeDtypeStruct((t, d), BF16),
        compiler_params=_cparams(("parallel", "parallel")),
    )(ya, yb, yc, wa, wb, wc, p2, p2, p2, b_gate, b_gate, b_gate)


def _out_kernel(m_ref, w_ref, x_ref, g_ref, o_ref):
    o_ref[...] = x_ref[...] + g_ref[...] * _dg(m_ref[...], w_ref[...])


def _out_proj(merged, w_out, x2, mod, seq):
    t, d = x2.shape
    tm, tn = min(seq, 2048), 512
    per_batch = seq // tm
    nj = d // tn
    return pl.pallas_call(
        _out_kernel,
        grid=(t // tm, nj),
        in_specs=[
            pl.BlockSpec((tm, d), lambda i, j: (i, 0)),
            pl.BlockSpec((d, tn), lambda i, j: (0, j)),
            pl.BlockSpec((tm, tn), lambda i, j: (i, j)),
            pl.BlockSpec((None, 1, tn), lambda i, j: (i // per_batch, 0, 2 * nj + j)),
        ],
        out_specs=pl.BlockSpec((tm, tn), lambda i, j: (i, j)),
        out_shape=jax.ShapeDtypeStruct((t, d), F32),
        compiler_params=_cparams(("parallel", "parallel")),
    )(merged, w_out, x2, mod)


def _wide_weight(w_in, layer):
    cols = [(_OFF_QKV_A, 3 * GDN_WIDTH), (_OFF_Z_A, GDN_WIDTH), (_OFF_GLU, 2 * CONV_WIDTH),
            (_OFF_Z_B, CONV_WIDTH), (_OFF_Q_C, ATT_WIDTH), (_OFF_Z_C, ATT_WIDTH),
            (_OFF_GATE, N_BRANCH * D_MODEL), (_OFF_K_C, ATT_KV_WIDTH), (_OFF_V_C, ATT_KV_WIDTH)]
    return jnp.concatenate([w_in[layer, :, o:o + n].astype(BF16) for o, n in cols], axis=1)


def _narrow_weight(w_in, layer):
    cols = [(_OFF_Q_I, IDX_HEADS * IDX_HEAD_DIM), (_OFF_K_I, IDX_HEAD_DIM), (_OFF_W_I, IDX_HEADS),
            (_OFF_BETA, GDN_HEADS), (_OFF_ALPHA, GDN_HEADS)]
    used = sum(n for _, n in cols)
    parts = [w_in[layer, :, o:o + n].astype(BF16) for o, n in cols]
    parts.append(jnp.zeros((w_in.shape[1], NS_NARROW - used), BF16))
    return jnp.concatenate(parts, axis=1)


def _expand3(x, width):
    pad = jnp.zeros(x.shape[:-1] + (IDX_PAD - 3 * width,), x.dtype)
    y = jnp.concatenate([x, x, x, pad], axis=-1)
    return y.reshape(y.shape[:-2] + (-1,))


def kernel(x, c, rel_bias, w_ada, b_ada, norm_g, w_in, b_gate, conv_a_w, a_log, dt_bias, gdn_norm_g,
           w_proj_a, conv_b_w, conv_b_b, ln_b_g, ln_b_b, w_proj_b, q_norm_g, k_norm_g, w_proj_c, w_out):
    bsz, seq, d = x.shape
    t = bsz * seq
    nch = seq // CHUNK
    mod_all = _ada_modulation(c, w_ada, b_ada)
    bias_tab = _bias_tiles(rel_bias)
    for layer in range(DEPTH):
        mod = mod_all[layer].reshape(bsz, 1, 3 * d)
        h = _norm_modulate(x, norm_g[layer].reshape(1, d), mod).reshape(t, d)
        p2 = _matmul(h, _wide_weight(w_in, layer), BF16, min(t, 2048), 512)
        s2 = _matmul(h, _narrow_weight(w_in, layer), F32, min(t, 1024), NS_NARROW)
        p3 = p2.reshape(bsz, seq, NP_WIDE)
        s3 = s2.reshape(bsz, seq, NS_NARROW)

        def head_rows(off):
            r = s3[:, :, off:off + GDN_HEADS]
            return jnp.swapaxes(r, 1, 2).reshape(bsz, GDN_HEADS, nch, CHUNK)

        head_params = jnp.stack([a_log[layer], dt_bias[layer]], axis=1)
        y_a = _gdn_mixer(p3, head_rows(S_BETA), head_rows(S_ALPHA), conv_a_w[layer], head_params,
                         gdn_norm_g[layer].reshape(1, GDN_HEAD_DIM))
        y_b = _conformer_mixer(p3, conv_b_w[layer], conv_b_b[layer].reshape(1, CONV_WIDTH),
                               ln_b_g[layer].reshape(1, CONV_WIDTH), ln_b_b[layer].reshape(1, CONV_WIDTH))
        q3 = _expand3(s3[:, :, S_Q_I:S_K_I].reshape(bsz, seq, IDX_HEADS, IDX_HEAD_DIM), IDX_HEAD_DIM)
        k3 = _expand3(s3[:, :, S_K_I:S_W_I].reshape(bsz, seq, 1, IDX_HEAD_DIM), IDX_HEAD_DIM)
        y_c = _dsa_mixer(p3, q3, k3, jnp.swapaxes(s3[:, :, S_W_I:S_W_I + IDX_HEADS], 1, 2),
                         q_norm_g[layer].reshape(1, ATT_HEAD_DIM), k_norm_g[layer].reshape(1, ATT_HEAD_DIM),
                         bias_tab)
        merged = _merge(y_a.reshape(t, GDN_WIDTH), y_b.reshape(t, CONV_WIDTH), y_c.reshape(t, ATT_WIDTH),
                        w_proj_a[layer].astype(BF16), w_proj_b[layer].astype(BF16),
                        w_proj_c[layer].astype(BF16), p2, b_gate[layer].reshape(1, N_BRANCH * d))
        x = _out_proj(merged, w_out[layer].astype(BF16), x.reshape(t, d), mod, seq).reshape(bsz, seq, d)
    return x
```

```python
import functools
import math

import jax
import jax.numpy as jnp
from jax import lax
from jax.experimental import pallas as pl
from jax.experimental.pallas import tpu as pltpu

F32 = jnp.float32
BF16 = jnp.bfloat16
I32 = jnp.int32

D_MODEL = 2048
DEPTH = 4
CHUNK = 64
GDN_HEADS = 8
GDN_HEAD_DIM = 128
GDN_WIDTH = GDN_HEADS * GDN_HEAD_DIM
GDN_CONV = 4
CONV_WIDTH = 1024
CONV_KERNEL = 31
ATT_HEADS = 8
ATT_KV_HEADS = 2
ATT_HEAD_DIM = 128
ATT_WIDTH = ATT_HEADS * ATT_HEAD_DIM
ATT_KV_WIDTH = ATT_KV_HEADS * ATT_HEAD_DIM
ATT_GROUP = ATT_HEADS // ATT_KV_HEADS
IDX_HEADS = 8
IDX_HEAD_DIM = 64
TOPK_MAX = 256
REL_BUCKETS = 32
N_BRANCH = 3
EPS = 1e-6

_OFF_QKV_A = 0
_OFF_Z_A = _OFF_QKV_A + 3 * GDN_WIDTH
_OFF_BETA = _OFF_Z_A + GDN_WIDTH
_OFF_ALPHA = _OFF_BETA + GDN_HEADS
_OFF_GLU = _OFF_ALPHA + GDN_HEADS
_OFF_Z_B = _OFF_GLU + 2 * CONV_WIDTH
_OFF_Q_C = _OFF_Z_B + CONV_WIDTH
_OFF_K_C = _OFF_Q_C + ATT_WIDTH
_OFF_V_C = _OFF_K_C + ATT_KV_WIDTH
_OFF_Z_C = _OFF_V_C + ATT_KV_WIDTH
_OFF_Q_I = _OFF_Z_C + ATT_WIDTH
_OFF_K_I = _OFF_Q_I + IDX_HEADS * IDX_HEAD_DIM
_OFF_W_I = _OFF_K_I + IDX_HEAD_DIM
_OFF_GATE = _OFF_W_I + IDX_HEADS
N_IN = _OFF_GATE + N_BRANCH * D_MODEL

P_QKV_A = 0
P_Z_A = 3072
P_GLU = 4096
P_Z_B = 6144
P_Q_C = 7168
P_Z_C = 8192
P_GATE = 9216
P_K_C = 15360
P_V_C = 15616
NP_WIDE = 15872
S_Q_I = 0
S_K_I = 512
S_W_I = 576
S_BETA = 584
S_ALPHA = 592
NS_NARROW = 640

LANES = 128
SUBLANES = 8
DSA_BLOCK = 128
DSA_QROWS = 256
DSA_KBLOCK = 256
DSA_BIAS_KINDS = DSA_KBLOCK // DSA_BLOCK + 2
IDX_PAD = 256
VMEM_LIMIT = 56 * 1024 * 1024
NEG_BIG = -1e30
INT_MIN = -(2 ** 31)

NN = (((1,), (0,)), ((), ()))
NT = (((1,), (1,)), ((), ()))
TN = (((0,), (0,)), ((), ()))


def _dg(a, b, dims=NN):
    return lax.dot_general(a, b, dims, preferred_element_type=F32)


def _split2(x):
    hi = x.astype(BF16)
    lo = (x - hi.astype(F32)).astype(BF16)
    return hi, lo


def _dot_b(a, b, dims=NN):
    return _dg(a.astype(BF16), b.astype(BF16), dims)


def _dot_hp(a, b, dims=NN):
    ah, al = _split2(a)
    bh, bl = _split2(b)
    return _dg(ah, bh, dims) + (_dg(al, bh, dims) + _dg(ah, bl, dims))


def _dot_hp_exact_rhs(a, b_bf16, dims=NN):
    a0 = a.astype(BF16)
    r = a - a0.astype(F32)
    a1 = r.astype(BF16)
    a2 = (r - a1.astype(F32)).astype(BF16)
    return _dg(a0, b_bf16, dims) + (_dg(a1, b_bf16, dims) + _dg(a2, b_bf16, dims))


def _sigmoid(x):
    return 1.0 / (1.0 + jnp.exp(-x))


def _silu(x):
    return x * _sigmoid(x)


def _softplus(x):
    return jnp.maximum(x, 0.0) + jnp.log(1.0 + jnp.exp(-jnp.abs(x)))


def _cparams(sem):
    return pltpu.CompilerParams(dimension_semantics=sem, vmem_limit_bytes=VMEM_LIMIT)


def _ada_kernel(c_ref, w_ref, b_ref, o_ref):
    cond = _silu(c_ref[...])
    o_ref[...] = _dot_hp(cond, w_ref[...]) + b_ref[...]


def _ada_modulation(c, w_ada, b_ada):
    depth, d, n = w_ada.shape
    bsz = c.shape[0]
    tn = 512
    return pl.pallas_call(
        _ada_kernel,
        grid=(depth, n // tn),
        in_specs=[
            pl.BlockSpec((bsz, d), lambda l, j: (0, 0)),
            pl.BlockSpec((None, d, tn), lambda l, j: (l, 0, j)),
            pl.BlockSpec((None, 1, tn), lambda l, j: (l, 0, j)),
        ],
        out_specs=pl.BlockSpec((None, bsz, tn), lambda l, j: (l, 0, j)),
        out_shape=jax.ShapeDtypeStruct((depth, bsz, n), F32),
        compiler_params=_cparams(("parallel", "parallel")),
    )(c, w_ada, b_ada.reshape(depth, 1, n))


def _norm_kernel(x_ref, g_ref, sc_ref, sh_ref, o_ref):
    x = x_ref[...]
    ms = jnp.mean(x * x, axis=-1, keepdims=True)
    y = x * lax.rsqrt(ms + EPS) * g_ref[...]
    o_ref[...] = (y * (1.0 + sc_ref[...]) + sh_ref[...]).astype(o_ref.dtype)


def _norm_modulate(x, g, mod):
    bsz, seq, d = x.shape
    tl = 512
    return pl.pallas_call(
        _norm_kernel,
        grid=(bsz, seq // tl),
        in_specs=[
            pl.BlockSpec((None, tl, d), lambda b, i: (b, i, 0)),
            pl.BlockSpec((1, d), lambda b, i: (0, 0)),
            pl.BlockSpec((None, 1, d), lambda b, i: (b, 0, 1)),
            pl.BlockSpec((None, 1, d), lambda b, i: (b, 0, 0)),
        ],
        out_specs=pl.BlockSpec((None, tl, d), lambda b, i: (b, i, 0)),
        out_shape=jax.ShapeDtypeStruct((bsz, seq, d), BF16),
        compiler_params=_cparams(("parallel", "parallel")),
    )(x, g, mod, mod)


def _mm_kernel(a_ref, b_ref, o_ref):
    o_ref[...] = _dg(a_ref[...], b_ref[...]).astype(o_ref.dtype)


def _matmul(a, b, out_dtype, tm, tn):
    m, k = a.shape
    n = b.shape[1]
    return pl.pallas_call(
        _mm_kernel,
        grid=(m // tm, n // tn),
        in_specs=[
            pl.BlockSpec((tm, k), lambda i, j: (i, 0)),
            pl.BlockSpec((k, tn), lambda i, j: (0, j)),
        ],
        out_specs=pl.BlockSpec((tm, tn), lambda i, j: (i, j)),
        out_shape=jax.ShapeDtypeStruct((m, n), out_dtype),
        compiler_params=_cparams(("parallel", "parallel")),
    )(a, b)


def _tri_inverse(mats, eye):
    n = mats[0].shape[0]
    levels = CHUNK.bit_length() - 1
    bs = [-a for a in mats]
    ss = [eye + b for b in bs]
    ps = [_dot_b(b, b) for b in bs]
    for _ in range(1, levels - 1):
        prods = [_dot_b(jnp.concatenate([s, p], axis=0), p) for s, p in zip(ss, ps)]
        ss = [s + pr[:n] for s, pr in zip(ss, prods)]
        ps = [pr[n:] for pr in prods]
    return [s + _dot_b(s, p) for s, p in zip(ss, ps)]


GDN_UNROLL = 8


def _gdn_kernel(hp_ref, q_ref, k_ref, v_ref, z_ref, cwq_ref, cwk_ref, cwv_ref, br_ref, ar_ref,
                gn_ref, o_ref, pad_scr, qn_scr, kn_scr, vn_scr, gc_scr, beta_scr,
                lhs_scr, cadd_scr, *, hb, seq):
    hd = GDN_HEAD_DIM
    nch = seq // CHUNK
    head0 = pl.program_id(1) * hb
    row_tile = min(seq, 256)
    unroll = math.gcd(GDN_UNROLL, nch)

    ii = lax.broadcasted_iota(I32, (CHUNK, CHUNK), 0)
    jj = lax.broadcasted_iota(I32, (CHUNK, CHUNK), 1)
    incl = ii >= jj
    strict = ii > jj
    eye = jnp.where(ii == jj, 1.0, 0.0).astype(F32)
    upper = jnp.where(ii <= jj, 1.0, 0.0).astype(BF16)

    def row_to_col(row):
        return jnp.sum(jnp.where(ii == jj, jnp.broadcast_to(row, (CHUNK, CHUNK)), 0.0),
                       axis=1, keepdims=True)

    for hh in range(hb):
        ls = slice(hh * hd, (hh + 1) * hd)
        for src, cw, dst, mode in ((q_ref, cwq_ref, qn_scr, "q"), (k_ref, cwk_ref, kn_scr, "k"),
                                   (v_ref, cwv_ref, vn_scr, "v")):
            pad_scr[0:8, :] = jnp.zeros((8, hd), F32)
            pad_scr[8:8 + seq, :] = src[:, ls].astype(F32)
            for rt in range(seq // row_tile):
                base = rt * row_tile
                acc = jnp.zeros((row_tile, hd), F32)
                for kk in range(GDN_CONV):
                    off = base + 8 - (GDN_CONV - 1) + kk
                    acc = acc + cw[kk:kk + 1, ls] * pad_scr[off:off + row_tile, :]
                y = _silu(acc)
                if mode != "v":
                    y = y * lax.rsqrt(jnp.sum(y * y, axis=-1, keepdims=True) + EPS)
                if mode == "q":
                    y = y * (hd ** -0.5)
                dst[hh, base:base + row_tile, :] = y
        a_log = hp_ref[head0 + hh, 0]
        dt_b = hp_ref[head0 + hh, 1]
        g2 = -jnp.exp(jnp.full((nch, CHUNK), a_log, F32)) * _softplus(ar_ref[hh] + dt_b)
        gc_scr[hh] = _dot_hp_exact_rhs(g2, upper)
        beta_scr[hh] = _sigmoid(br_ref[hh])

    i_hd = lax.broadcasted_iota(I32, (hd, hd), 0)
    j_hd = lax.broadcasted_iota(I32, (hd, hd), 1)
    eye_hd = jnp.where(i_hd == j_hd, 1.0, 0.0).astype(BF16)

    def load_chunk(n, hh):
        r0 = pl.multiple_of(n * CHUNK, CHUNK)
        return (qn_scr[hh, pl.ds(r0, CHUNK), :], kn_scr[hh, pl.ds(r0, CHUNK), :],
                vn_scr[hh, pl.ds(r0, CHUNK), :], gc_scr[hh, pl.ds(n, 1), :],
                beta_scr[hh, pl.ds(n, 1), :])

    def compute_chunks(loaded):
        qs, ks, vs, gc_rows, beta_rows = zip(*loaded)
        gc_cols = [row_to_col(r) for r in gc_rows]
        beta_cols = [row_to_col(r) for r in beta_rows]
        decays = [jnp.where(incl, jnp.exp(jnp.where(incl, c - r, 0.0)), 0.0)
                  for c, r in zip(gc_cols, gc_rows)]
        k16s = [k.astype(BF16) for k in ks]
        qk_kks = [_dg(jnp.concatenate([q.astype(BF16), k16], axis=0), k16, NT)
                  for q, k16 in zip(qs, k16s)]
        intras = [jnp.where(incl, m[:CHUNK] * d, 0.0).astype(BF16) for m, d in zip(qk_kks, decays)]
        t_invs = _tri_inverse([jnp.where(strict, b * m[CHUNK:] * d, 0.0)
                               for b, m, d in zip(beta_cols, qk_kks, decays)], eye)
        e_gcs = [jnp.exp(c) for c in gc_cols]
        uws = [_dot_b(t, jnp.concatenate([v * b, k * (b * e)], axis=1))
               for t, v, k, b, e in zip(t_invs, vs, ks, beta_cols, e_gcs)]
        kdts = [_dg(eye_hd, (k * jnp.exp(r[:, CHUNK - 1:CHUNK] - c)).astype(BF16), NT).astype(BF16)
                for k, r, c in zip(ks, gc_rows, gc_cols)]
        wu16s = [jnp.concatenate([uw[:, hd:], uw[:, :hd]], axis=1).astype(BF16) for uw in uws]
        kwcs = [_dg(kdt, wu) for kdt, wu in zip(kdts, wu16s)]
        iwus = [_dg(intra, wu) for intra, wu in zip(intras, wu16s)]
        lhs = [jnp.concatenate([kwc[:, :hd], q * e - iwu[:, :hd]], axis=0).astype(BF16)
               for kwc, q, e, iwu in zip(kwcs, qs, e_gcs, iwus)]
        return [(l, kwc[:, hd:], iwu[:, hd:]) for l, kwc, iwu in zip(lhs, kwcs, iwus)]

    def store_chunk(n, hh, lhs, c_add, o_add):
        lhs_scr[hh, n] = lhs
        cadd_scr[hh, n] = c_add
        vn_scr[hh, pl.ds(pl.multiple_of(n * CHUNK, CHUNK), CHUNK), :] = o_add

    def prepare_body(step, carry):
        jobs = [(step * unroll + cu, hh) for cu in range(unroll) for hh in range(hb)]
        results = compute_chunks([load_chunk(n, hh) for n, hh in jobs])
        for (n, hh), res in zip(jobs, results):
            store_chunk(n, hh, *res)
        return carry

    lax.fori_loop(0, nch // unroll, prepare_body, 0)

    def scan_body(n, states):
        r0 = pl.multiple_of(n * CHUNK, CHUNK)
        loaded = [(lhs_scr[hh, n], cadd_scr[hh, n], vn_scr[hh, pl.ds(r0, CHUNK), :],
                   gc_scr[hh, pl.ds(n, 1), CHUNK - 1:CHUNK]) for hh in range(hb)]
        lhss, c_adds, o_adds, g_lasts = zip(*loaded)
        prods = [_dg(l, s.astype(BF16)) for l, s in zip(lhss, states)]
        new_states = [s * jnp.exp(g) - p[:hd] + c for s, g, p, c in zip(states, g_lasts, prods, c_adds)]
        for hh in range(hb):
            qn_scr[hh, pl.ds(r0, CHUNK), :] = prods[hh][hd:] + o_adds[hh]
        return tuple(new_states)

    lax.fori_loop(0, nch, scan_body, tuple(jnp.zeros((hd, hd), F32) for _ in range(hb)))

    for hh in range(hb):
        ls = slice(hh * hd, (hh + 1) * hd)
        for rt in range(seq // row_tile):
            rs = slice(rt * row_tile, (rt + 1) * row_tile)
            o = qn_scr[hh, rs, :]
            on = o * lax.rsqrt(jnp.mean(o * o, axis=-1, keepdims=True) + EPS) * gn_ref[...]
            o_ref[rs, ls] = (on * _silu(z_ref[rs, ls].astype(F32))).astype(o_ref.dtype)


def _gdn_mixer(p3, beta_r, alpha_r, conv_w, head_params, gn, hb=2):
    bsz, seq, _ = p3.shape
    wb = hb * GDN_HEAD_DIM
    nhb = GDN_HEADS // hb
    nch = seq // CHUNK
    qb, kb_, vb, zb = (P_QKV_A // wb, (P_QKV_A + GDN_WIDTH) // wb, (P_QKV_A + 2 * GDN_WIDTH) // wb,
                       P_Z_A // wb)
    col = lambda off: pl.BlockSpec((None, seq, wb), lambda b, h, off=off: (b, 0, off + h))
    cwspec = lambda off: pl.BlockSpec((GDN_CONV, wb), lambda b, h, off=off: (0, off + h))
    rows = pl.BlockSpec((None, hb, nch, CHUNK), lambda b, h: (b, h, 0, 0))
    return pl.pallas_call(
        functools.partial(_gdn_kernel, hb=hb, seq=seq),
        grid=(bsz, nhb),
        in_specs=[
            pl.BlockSpec(memory_space=pltpu.SMEM),
            col(qb), col(kb_), col(vb), col(zb),
            cwspec(0), cwspec(nhb), cwspec(2 * nhb),
            rows, rows,
            pl.BlockSpec((1, GDN_HEAD_DIM), lambda b, h: (0, 0)),
        ],
        out_specs=pl.BlockSpec((None, seq, wb), lambda b, h: (b, 0, h)),
        out_shape=jax.ShapeDtypeStruct((bsz, seq, GDN_WIDTH), BF16),
        scratch_shapes=[
            pltpu.VMEM((seq + 8, GDN_HEAD_DIM), F32),
            pltpu.VMEM((hb, seq, GDN_HEAD_DIM), F32),
            pltpu.VMEM((hb, seq, GDN_HEAD_DIM), F32),
            pltpu.VMEM((hb, seq, GDN_HEAD_DIM), F32),
            pltpu.VMEM((hb, nch, CHUNK), F32),
            pltpu.VMEM((hb, nch, CHUNK), F32),
            pltpu.VMEM((hb, nch, GDN_HEAD_DIM + CHUNK, GDN_HEAD_DIM), BF16),
            pltpu.VMEM((hb, nch, GDN_HEAD_DIM, GDN_HEAD_DIM), F32),
        ],
        compiler_params=_cparams(("parallel", "parallel")),
    )(head_params, p3, p3, p3, p3, conv_w, conv_w, conv_w, beta_r, alpha_r, gn)


CONF_HALO = 32
CONF_ROWS = 32
CONF_COPY_ROWS = 64


def _conf_kernel(a_ref, b_ref, ap_ref, bp_ref, z_ref, w_ref, cb_ref, lg_ref, lb_ref, o_ref, glu_scr,
                 *, tl):
    i = pl.program_id(1)
    rows = CONF_HALO + tl
    for rt in range(tl // CONF_COPY_ROWS):
        rs = slice(rt * CONF_COPY_ROWS, (rt + 1) * CONF_COPY_ROWS)
        glu_scr[0, CONF_HALO + rs.start:CONF_HALO + rs.stop, :] = (
            a_ref[rs, :].astype(F32) * _sigmoid(b_ref[rs, :].astype(F32)))
    prev = ap_ref[...].astype(F32) * _sigmoid(bp_ref[...].astype(F32))
    glu_scr[0, 0:CONF_HALO, :] = jnp.where(i > 0, prev, 0.0)
    for s in range(1, SUBLANES):
        for r0 in range(0, rows - SUBLANES, CONF_COPY_ROWS):
            n = min(CONF_COPY_ROWS, rows - SUBLANES - r0)
            glu_scr[s, r0:r0 + n, :] = glu_scr[0, r0 + s:r0 + s + n, :]
    groups = CONF_ROWS // SUBLANES
    for rt in range(tl // CONF_ROWS):
        base = rt * CONF_ROWS
        acc = jnp.broadcast_to(cb_ref[...], (groups, SUBLANES, CONV_WIDTH))
        for kk in range(CONV_KERNEL):
            off = base + CONF_HALO - (CONV_KERNEL - 1) + kk
            s = off % SUBLANES
            win = glu_scr[s, off - s:off - s + CONF_ROWS, :].reshape(groups, SUBLANES, CONV_WIDTH)
            acc = acc + w_ref[kk][None] * win
        acc = acc.reshape(CONF_ROWS, CONV_WIDTH)
        mu = jnp.mean(acc, axis=-1, keepdims=True)
        cen = acc - mu
        var = jnp.mean(cen * cen, axis=-1, keepdims=True)
        y = cen * lax.rsqrt(var + EPS) * lg_ref[...] + lb_ref[...]
        z = z_ref[base:base + CONF_ROWS, :].astype(F32)
        o_ref[base:base + CONF_ROWS, :] = (_silu(y) * _silu(z)).astype(o_ref.dtype)


def _conformer_mixer(p3, conv_w, conv_b, ln_g, ln_b):
    bsz, seq, _ = p3.shape
    tl = 256
    cw = CONV_WIDTH
    ca, cb, cz = P_GLU // cw, P_GLU // cw + 1, P_Z_B // cw
    halo_per_tile = tl // CONF_HALO
    cur = lambda c: pl.BlockSpec((None, tl, cw), lambda b, i, c=c: (b, i, c))
    prev = lambda c: pl.BlockSpec(
        (None, CONF_HALO, cw), lambda b, i, c=c: (b, jnp.maximum(i * halo_per_tile - 1, 0), c))
    vec = pl.BlockSpec((1, cw), lambda b, i: (0, 0))
    return pl.pallas_call(
        functools.partial(_conf_kernel, tl=tl),
        grid=(bsz, seq // tl),
        in_specs=[cur(ca), cur(cb), prev(ca), prev(cb), cur(cz),
                  pl.BlockSpec((CONV_KERNEL, SUBLANES, cw), lambda b, i: (0, 0, 0)), vec, vec, vec],
        out_specs=pl.BlockSpec((None, tl, cw), lambda b, i: (b, i, 0)),
        out_shape=jax.ShapeDtypeStruct((bsz, seq, cw), BF16),
        scratch_shapes=[pltpu.VMEM((SUBLANES, CONF_HALO + tl, cw), F32)],
        compiler_params=_cparams(("parallel", "parallel")),
    )(p3, p3, p3, p3, p3, jnp.broadcast_to(conv_w[:, None, :], (CONV_KERNEL, SUBLANES, cw)), conv_b, ln_g, ln_b)


def _t5_bucket_int(rel):
    nb = REL_BUCKETS // 2
    max_exact = nb // 2
    n = jnp.abs(rel)
    large = jnp.full(rel.shape, max_exact, I32)
    for thr in (12, 16, 23, 32, 46, 64, 91):
        large = large + jnp.where(n >= thr, 1, 0)
    large = jnp.minimum(large, nb - 1)
    return jnp.where(rel > 0, nb, 0) + jnp.where(n < max_exact, n, large)


def _bias_kernel(rb_ref, o_ref):
    kpos = lax.broadcasted_iota(I32, (DSA_KBLOCK, DSA_BLOCK), 0)
    qpos = lax.broadcasted_iota(I32, (DSA_KBLOCK, DSA_BLOCK), 1)
    for kind in range(DSA_BIAS_KINDS):
        bucket = _t5_bucket_int(kpos - qpos + (kind - (DSA_BIAS_KINDS - 1)) * DSA_BLOCK)
        for h in range(ATT_HEADS):
            val = jnp.zeros((DSA_KBLOCK, DSA_BLOCK), F32)
            for b in range(REL_BUCKETS):
                val = jnp.where(bucket == b, rb_ref[b, h], val)
            o_ref[kind, h] = val


def _bias_tiles(rel_bias):
    return pl.pallas_call(
        _bias_kernel,
        in_specs=[pl.BlockSpec(memory_space=pltpu.SMEM)],
        out_shape=jax.ShapeDtypeStruct((DSA_BIAS_KINDS, ATT_HEADS, DSA_KBLOCK, DSA_BLOCK), F32),
    )(rel_bias)


def _for_block_pairs(n, fn, carry):
    def pair(p, c):
        return fn([2 * p, 2 * p + 1], c)

    c = lax.fori_loop(0, lax.shift_right_logical(n, 1), pair, carry)

    @pl.when((n & 1) == 1)
    def _last():
        fn([n - 1], c)


def _dsa_kernel(qi_ref, ki_ref, wt_ref, qc_ref, zc_ref, kc_ref, vc_ref, qg_ref, kg_ref, tab_ref,
                o_ref, kp_scr, kn_scr, vt_scr, qp_scr, key_scr, sel_scr, qs_scr, lg_scr, mx_scr, ls_scr,
                acc_scr, *, seq, topk):
    blk = DSA_BLOCK
    qrows = DSA_QROWS
    kblk = DSA_KBLOCK
    hd = ATT_HEAD_DIM
    i = pl.program_id(1)
    nkv = (i * qrows) // kblk + 1
    kf = float(topk)

    @pl.when(i == 0)
    def _prepare_keys():
        e_r = lax.broadcasted_iota(I32, (hd, hd), 0)
        e_c = lax.broadcasted_iota(I32, (hd, hd), 1)
        eye_hd = jnp.where(e_r == e_c, 1.0, 0.0).astype(BF16)
        low_k = lax.broadcasted_iota(I32, (kblk, LANES), 1) < IDX_HEAD_DIM
        for t in range(seq // kblk):
            rs = slice(t * kblk, (t + 1) * kblk)
            kt = ki_ref[rs, :]
            hi = kt.astype(BF16).astype(F32)
            lo = kt - hi
            kp_scr[rs, 0:LANES] = jnp.where(low_k, hi, pltpu.roll(hi, IDX_HEAD_DIM, 1)).astype(BF16)
            kp_scr[rs, LANES:2 * LANES] = jnp.where(low_k, lo, 0.0).astype(BF16)
            kc = kc_ref[rs, :].astype(F32)
            for g in range(ATT_KV_HEADS):
                ls = slice(g * hd, (g + 1) * hd)
                kg = kc[:, ls]
                kgn = kg * lax.rsqrt(jnp.mean(kg * kg, axis=-1, keepdims=True) + EPS) * kg_ref[...]
                kn_scr[rs, ls] = kgn.astype(BF16)
                vt_scr[t, g] = _dg(eye_hd, vc_ref[rs, ls], NT).astype(BF16)

    low_q = lax.broadcasted_iota(I32, (qrows, LANES), 1) < IDX_HEAD_DIM
    for c in range(IDX_HEADS * IDX_HEAD_DIM // LANES):
        qt = qi_ref[:, c * LANES:(c + 1) * LANES]
        hi = qt.astype(BF16).astype(F32)
        lo = qt - hi
        hi_sw = pltpu.roll(hi, IDX_HEAD_DIM, 1)
        lo_sw = pltpu.roll(lo, IDX_HEAD_DIM, 1)
        for h, first, second in ((2 * c, jnp.where(low_q, hi, lo_sw), jnp.where(low_q, hi, 0.0)),
                                 (2 * c + 1, jnp.where(low_q, hi_sw, lo), jnp.where(low_q, hi_sw, 0.0))):
            qp_scr[h * qrows:(h + 1) * qrows, 0:LANES] = first.astype(BF16)
            qp_scr[h * qrows:(h + 1) * qrows, LANES:2 * LANES] = second.astype(BF16)
    wsc = wt_ref[...] * ((IDX_HEAD_DIM ** -0.5) * (IDX_HEADS ** -0.5))

    kpos = lax.broadcasted_iota(I32, (kblk, qrows), 0)
    qpos = lax.broadcasted_iota(I32, (kblk, qrows), 1)
    chunk_gap = (kpos // CHUNK) - (qpos // CHUNK)
    q_chunk0 = i * (qrows // CHUNK)

    def fold_rows(x):
        return x.reshape(kblk // SUBLANES, SUBLANES, x.shape[-1])

    def score_blocks(js, carry):
        qp = qp_scr[...]
        s_alls = [_dg(kp_scr[pl.ds(pl.multiple_of(j * kblk, kblk), kblk), :], qp, NT) for j in js]
        for j, s_all in zip(js, s_alls):
            sc = jnp.zeros((kblk, qrows), F32)
            for h in range(IDX_HEADS):
                sc = sc + wsc[h:h + 1, :] * jnp.maximum(s_all[:, h * qrows:(h + 1) * qrows], 0.0)
            sc = sc + 0.0
            bits = pltpu.bitcast(sc, I32)
            key = jnp.where(bits < 0, bits ^ jnp.int32(0x7FFFFFFF), bits)
            slack = q_chunk0 - j * (kblk // CHUNK)
            key_scr[j] = jnp.where(chunk_gap <= slack, key, jnp.int32(INT_MIN))
        return carry

    _for_block_pairs(nkv, score_blocks, 0)

    def count_where(pred, ref):
        def block(j):
            return jnp.sum(fold_rows(jnp.where(pred(key_scr[j], ref), 1.0, 0.0)), axis=0)

        acc = lax.fori_loop(0, lax.shift_right_logical(nkv, 1),
                            lambda p, a: a + (block(2 * p) + block(2 * p + 1)),
                            jnp.zeros((SUBLANES, qrows), F32))
        acc = lax.cond((nkv & 1) == 1, lambda a: a + block(nkv - 1), lambda a: a, acc)
        return jnp.sum(acc, axis=0, keepdims=True)

    ge = lambda kv, ref: kv >= ref
    cnt0 = count_where(ge, jnp.zeros((1, qrows), I32))
    cur0 = jnp.where(cnt0 >= kf, jnp.int32(0), jnp.int32(INT_MIN))

    def bit_body(t, cur):
        cand = cur | lax.shift_left(jnp.int32(1), jnp.int32(30) - t)
        return jnp.where(count_where(ge, cand) >= kf, cand, cur)

    thr = lax.fori_loop(0, 31, bit_body, cur0)
    need = kf - count_where(lambda kv, ref: kv > ref, thr)

    ku = lax.broadcasted_iota(I32, (kblk, kblk), 0)
    kv_ = lax.broadcasted_iota(I32, (kblk, kblk), 1)
    lower = jnp.where(ku >= kv_, 1.0, 0.0).astype(BF16)

    def sel_blocks(js, run):
        keys = [key_scr[j] for j in js]
        eqs = [key == thr for key in keys]
        pres = [_dg(lower, jnp.where(eq, 1.0, 0.0).astype(BF16)) for eq in eqs]
        for j, key, eq, pre in zip(js, keys, eqs, pres):
            take_tie = jnp.where(eq & (run + pre <= need), 0.0, NEG_BIG)
            take = jnp.where(key > thr, 0.0, take_tie)
            sel_scr[j] = jnp.where(key == jnp.int32(INT_MIN), NEG_BIG, take)
            run = run + pre[kblk - 1:kblk, :]
        return run

    _for_block_pairs(nkv, sel_blocks, jnp.zeros((1, qrows), F32))

    for a in range(qrows // blk):
        _dsa_attend(a, i * (qrows // blk) + a, nkv, qc_ref, zc_ref, qg_ref, tab_ref, o_ref,
                    kn_scr, vt_scr, sel_scr, qs_scr, lg_scr, mx_scr, ls_scr, acc_scr)


def _dsa_attend(a, qblock, nkv, qc_ref, zc_ref, qg_ref, tab_ref, o_ref,
                kn_scr, vt_scr, sel_scr, qs_scr, lg_scr, mx_scr, ls_scr, acc_scr):
    blk = DSA_BLOCK
    kblk = DSA_KBLOCK
    hd = ATT_HEAD_DIM
    cols4 = ATT_GROUP * blk
    rq = slice(a * blk, (a + 1) * blk)
    qc = qc_ref[rq, :].astype(F32)
    for h in range(ATT_HEADS):
        qh = qc[:, h * hd:(h + 1) * hd]
        qn = qh * lax.rsqrt(jnp.mean(qh * qh, axis=-1, keepdims=True) + EPS) * qg_ref[...]
        g, hh = divmod(h, ATT_GROUP)
        qs_scr[g, hh * blk:(hh + 1) * blk, :] = (qn * (hd ** -0.5)).astype(BF16)
    mx_scr[...] = jnp.full(mx_scr.shape, NEG_BIG, F32)

    def fold(x):
        return x.reshape(kblk // SUBLANES, SUBLANES, x.shape[-1])

    groups = range(ATT_KV_HEADS)

    def logit_blocks(js, carry):
        jg = [(n, g) for n in range(len(js)) for g in groups]
        kks = [kn_scr[pl.ds(pl.multiple_of(j * kblk, kblk), kblk), :] for j in js]
        raw = [_dg(kks[n][:, g * hd:(g + 1) * hd], qs_scr[g], NT) for n, g in jg]
        negs = [jnp.concatenate([sel_scr[j, :, rq]] * ATT_GROUP, axis=1) for j in js]
        kinds = [jnp.clip(j * (kblk // blk) - qblock + (DSA_BIAS_KINDS - 1), 0, DSA_BIAS_KINDS - 1)
                 for j in js]
        tops = [mx_scr[g] for g in groups]
        for (n, g), r in zip(jg, raw):
            bias = jnp.concatenate([tab_ref[kinds[n], g * ATT_GROUP + hh] for hh in range(ATT_GROUP)],
                                   axis=1)
            lg = r + bias + negs[n]
            lg_scr[js[n], g] = lg
            tops[g] = jnp.maximum(tops[g], jnp.max(fold(lg), axis=0))
        for g in groups:
            mx_scr[g] = tops[g]
        return carry

    _for_block_pairs(nkv, logit_blocks, 0)
    for g in groups:
        mx_scr[g] = jnp.broadcast_to(jnp.max(mx_scr[g], axis=0, keepdims=True), (SUBLANES, cols4))
    acc_scr[...] = jnp.zeros(acc_scr.shape, F32)
    ls_scr[...] = jnp.zeros(ls_scr.shape, F32)

    def pv_blocks(js, carry):
        jg = [(j, g) for j in js for g in groups]
        ps = [jnp.exp(lg_scr[j, g] - mx_scr[g, 0:1, :]) for j, g in jg]
        pvs = [_dg(vt_scr[j, g], p.astype(BF16)) for (j, g), p in zip(jg, ps)]
        for g in groups:
            acc_scr[g] += sum(pv for (_, gg), pv in zip(jg, pvs) if gg == g)
            ls_scr[g] += sum(jnp.sum(fold(p), axis=0) for (_, gg), p in zip(jg, ps) if gg == g)
        return carry

    _for_block_pairs(nkv, pv_blocks, 0)

    for h in range(ATT_HEADS):
        g, hh = divmod(h, ATT_GROUP)
        cs = slice(hh * blk, (hh + 1) * blk)
        ls = slice(h * hd, (h + 1) * hd)
        o_t = acc_scr[g, :, cs] / jnp.sum(ls_scr[g, :, cs], axis=0, keepdims=True)
        z = zc_ref[rq, ls].astype(F32)
        o_ref[rq, ls] = (o_t.T * _silu(z)).astype(o_ref.dtype)


def _dsa_mixer(p3, s3, w_t, q_gain, k_gain, bias_tab):
    bsz, seq, _ = p3.shape
    blk = DSA_BLOCK
    kblk = DSA_KBLOCK
    qrows = DSA_QROWS
    nkb = seq // kblk
    rows4 = ATT_GROUP * blk
    topk = min(TOPK_MAX, seq // 4)
    return pl.pallas_call(
        functools.partial(_dsa_kernel, seq=seq, topk=topk),
        grid=(bsz, seq // qrows),
        in_specs=[
            pl.BlockSpec((None, qrows, IDX_HEADS * IDX_HEAD_DIM), lambda b, i: (b, i, S_Q_I // (IDX_HEADS * IDX_HEAD_DIM))),
            pl.BlockSpec((None, seq, LANES), lambda b, i: (b, 0, S_K_I // LANES)),
            pl.BlockSpec((None, IDX_HEADS, qrows), lambda b, i: (b, 0, i)),
            pl.BlockSpec((None, qrows, ATT_WIDTH), lambda b, i: (b, i, P_Q_C // ATT_WIDTH)),
            pl.BlockSpec((None, qrows, ATT_WIDTH), lambda b, i: (b, i, P_Z_C // ATT_WIDTH)),
            pl.BlockSpec((None, seq, ATT_KV_WIDTH), lambda b, i: (b, 0, P_K_C // ATT_KV_WIDTH)),
            pl.BlockSpec((None, seq, ATT_KV_WIDTH), lambda b, i: (b, 0, P_V_C // ATT_KV_WIDTH)),
            pl.BlockSpec((1, ATT_HEAD_DIM), lambda b, i: (0, 0)),
            pl.BlockSpec((1, ATT_HEAD_DIM), lambda b, i: (0, 0)),
            pl.BlockSpec((DSA_BIAS_KINDS, ATT_HEADS, kblk, blk), lambda b, i: (0, 0, 0, 0)),
        ],
        out_specs=pl.BlockSpec((None, qrows, ATT_WIDTH), lambda b, i: (b, i, 0)),
        out_shape=jax.ShapeDtypeStruct((bsz, seq, ATT_WIDTH), BF16),
        scratch_shapes=[
            pltpu.VMEM((seq, IDX_PAD), BF16),
            pltpu.VMEM((seq, ATT_KV_WIDTH), BF16),
            pltpu.VMEM((nkb, ATT_KV_HEADS, ATT_HEAD_DIM, kblk), BF16),
            pltpu.VMEM((IDX_HEADS * qrows, IDX_PAD), BF16),
            pltpu.VMEM((nkb, kblk, qrows), I32),
            pltpu.VMEM((nkb, kblk, qrows), F32),
            pltpu.VMEM((ATT_KV_HEADS, rows4, ATT_HEAD_DIM), BF16),
            pltpu.VMEM((nkb, ATT_KV_HEADS, kblk, rows4), F32),
            pltpu.VMEM((ATT_KV_HEADS, SUBLANES, rows4), F32),
            pltpu.VMEM((ATT_KV_HEADS, SUBLANES, rows4), F32),
            pltpu.VMEM((ATT_KV_HEADS, ATT_HEAD_DIM, rows4), F32),
        ],
        compiler_params=_cparams(("arbitrary", "arbitrary")),
    )(s3, s3, w_t, p3, p3, p3, p3, q_gain, k_gain, bias_tab)


def _merge_kernel(ya_ref, yb_ref, yc_ref, wa_ref, wb_ref, wc_ref, ga_ref, gb_ref, gc_ref,
                  ba_ref, bb_ref, bc_ref, o_ref):
    acc = _sigmoid(ga_ref[...].astype(F32) + ba_ref[...]) * _dg(ya_ref[...], wa_ref[...])
    acc = acc + _sigmoid(gb_ref[...].astype(F32) + bb_ref[...]) * _dg(yb_ref[...], wb_ref[...])
    acc = acc + _sigmoid(gc_ref[...].astype(F32) + bc_ref[...]) * _dg(yc_ref[...], wc_ref[...])
    o_ref[...] = acc.astype(o_ref.dtype)


def _merge(ya, yb, yc, wa, wb, wc, p2, b_gate):
    t = ya.shape[0]
    d = D_MODEL
    tm, tn = min(t, 1024), 512
    nj = d // tn
    g0 = P_GATE // tn
    yspec = pl.BlockSpec((tm, ya.shape[1]), lambda i, j: (i, 0))
    wspec = pl.BlockSpec((wa.shape[0], tn), lambda i, j: (0, j))
    gspec = lambda br: pl.BlockSpec((tm, tn), lambda i, j, br=br: (i, g0 + br * nj + j))
    bspec = lambda br: pl.BlockSpec((1, tn), lambda i, j, br=br: (0, br * nj + j))
    return pl.pallas_call(
        _merge_kernel,
        grid=(t // tm, nj),
        in_specs=[yspec, yspec, yspec, wspec, wspec, wspec, gspec(0), gspec(1), gspec(2),
                  bspec(0), bspec(1), bspec(2)],
        out_specs=pl.BlockSpec((tm, tn), lambda i, j: (i, j)),
        out_shape=jax.ShapeDtypeStruct((t, d), BF16),
        compiler_params=_cparams(("parallel", "parallel")),
    )(ya, yb, yc, wa, wb, wc, p2, p2, p2, b_gate, b_gate, b_gate)


def _out_kernel(m_ref, w_ref, x_ref, g_ref, o_ref):
    o_ref[...] = x_ref[...] + g_ref[...] * _dg(m_ref[...], w_ref[...])


def _out_proj(merged, w_out, x2, mod, seq):
    t, d = x2.shape
    tm, tn = min(seq, 2048), 512
    per_batch = seq // tm
    nj = d // tn
    return pl.pallas_call(
        _out_kernel,
        grid=(t // tm, nj),
        in_specs=[
            pl.BlockSpec((tm, d), lambda i, j: (i, 0)),
            pl.BlockSpec((d, tn), lambda i, j: (0, j)),
            pl.BlockSpec((tm, tn), lambda i, j: (i, j)),
            pl.BlockSpec((None, 1, tn), lambda i, j: (i // per_batch, 0, 2 * nj + j)),
        ],
        out_specs=pl.BlockSpec((tm, tn), lambda i, j: (i, j)),
        out_shape=jax.ShapeDtypeStruct((t, d), F32),
        compiler_params=_cparams(("parallel", "parallel")),
    )(merged, w_out, x2, mod)


def _wide_weight(w_in, layer):
    cols = [(_OFF_QKV_A, 3 * GDN_WIDTH), (_OFF_Z_A, GDN_WIDTH), (_OFF_GLU, 2 * CONV_WIDTH),
            (_OFF_Z_B, CONV_WIDTH), (_OFF_Q_C, ATT_WIDTH), (_OFF_Z_C, ATT_WIDTH),
            (_OFF_GATE, N_BRANCH * D_MODEL), (_OFF_K_C, ATT_KV_WIDTH), (_OFF_V_C, ATT_KV_WIDTH)]
    return jnp.concatenate([w_in[layer, :, o:o + n].astype(BF16) for o, n in cols], axis=1)


def _narrow_weight(w_in, layer):
    cols = [(_OFF_Q_I, IDX_HEADS * IDX_HEAD_DIM), (_OFF_K_I, IDX_HEAD_DIM), (_OFF_W_I, IDX_HEADS),
            (_OFF_BETA, GDN_HEADS), (_OFF_ALPHA, GDN_HEADS)]
    used = sum(n for _, n in cols)
    parts = [w_in[layer, :, o:o + n].astype(BF16) for o, n in cols]
    parts.append(jnp.zeros((w_in.shape[1], NS_NARROW - used), BF16))
    return jnp.concatenate(parts, axis=1)


def kernel(x, c, rel_bias, w_ada, b_ada, norm_g, w_in, b_gate, conv_a_w, a_log, dt_bias, gdn_norm_g,
           w_proj_a, conv_b_w, conv_b_b, ln_b_g, ln_b_b, w_proj_b, q_norm_g, k_norm_g, w_proj_c, w_out):
    bsz, seq, d = x.shape
    t = bsz * seq
    nch = seq // CHUNK
    mod_all = _ada_modulation(c, w_ada, b_ada)
    bias_tab = _bias_tiles(rel_bias)
    for layer in range(DEPTH):
        mod = mod_all[layer].reshape(bsz, 1, 3 * d)
        h = _norm_modulate(x, norm_g[layer].reshape(1, d), mod).reshape(t, d)
        p2 = _matmul(h, _wide_weight(w_in, layer), BF16, min(t, 2048), 512)
        s2 = _matmul(h, _narrow_weight(w_in, layer), F32, min(t, 1024), NS_NARROW)
        p3 = p2.reshape(bsz, seq, NP_WIDE)
        s3 = s2.reshape(bsz, seq, NS_NARROW)

        def head_rows(off):
            r = s3[:, :, off:off + GDN_HEADS]
            return jnp.swapaxes(r, 1, 2).reshape(bsz, GDN_HEADS, nch, CHUNK)

        head_params = jnp.stack([a_log[layer], dt_bias[layer]], axis=1)
        y_a = _gdn_mixer(p3, head_rows(S_BETA), head_rows(S_ALPHA), conv_a_w[layer], head_params,
                         gdn_norm_g[layer].reshape(1, GDN_HEAD_DIM))
        y_b = _conformer_mixer(p3, conv_b_w[layer], conv_b_b[layer].reshape(1, CONV_WIDTH),
                               ln_b_g[layer].reshape(1, CONV_WIDTH), ln_b_b[layer].reshape(1, CONV_WIDTH))
        y_c = _dsa_mixer(p3, s3, jnp.swapaxes(s3[:, :, S_W_I:S_W_I + IDX_HEADS], 1, 2),
                         q_norm_g[layer].reshape(1, ATT_HEAD_DIM), k_norm_g[layer].reshape(1, ATT_HEAD_DIM),
                         bias_tab)
        merged = _merge(y_a.reshape(t, GDN_WIDTH), y_b.reshape(t, CONV_WIDTH), y_c.reshape(t, ATT_WIDTH),
                        w_proj_a[layer].astype(BF16), w_proj_b[layer].astype(BF16),
                        w_proj_c[layer].astype(BF16), p2, b_gate[layer].reshape(1, N_BRANCH * d))
        x = _out_proj(merged, w_out[layer].astype(BF16), x.reshape(t, d), mod, seq).reshape(bsz, seq, d)
    return x
```

```python
import functools
import math

import jax
import jax.numpy as jnp
from jax import lax
from jax.experimental import pallas as pl
from jax.experimental.pallas import tpu as pltpu

F32 = jnp.float32
BF16 = jnp.bfloat16
I32 = jnp.int32

D_MODEL = 2048
DEPTH = 4
CHUNK = 64
GDN_HEADS = 8
GDN_HEAD_DIM = 128
GDN_WIDTH = GDN_HEADS * GDN_HEAD_DIM
GDN_CONV = 4
CONV_WIDTH = 1024
CONV_KERNEL = 31
ATT_HEADS = 8
ATT_KV_HEADS = 2
ATT_HEAD_DIM = 128
ATT_WIDTH = ATT_HEADS * ATT_HEAD_DIM
ATT_KV_WIDTH = ATT_KV_HEADS * ATT_HEAD_DIM
ATT_GROUP = ATT_HEADS // ATT_KV_HEADS
IDX_HEADS = 8
IDX_HEAD_DIM = 64
TOPK_MAX = 256
REL_BUCKETS = 32
N_BRANCH = 3
EPS = 1e-6

_OFF_QKV_A = 0
_OFF_Z_A = _OFF_QKV_A + 3 * GDN_WIDTH
_OFF_BETA = _OFF_Z_A + GDN_WIDTH
_OFF_ALPHA = _OFF_BETA + GDN_HEADS
_OFF_GLU = _OFF_ALPHA + GDN_HEADS
_OFF_Z_B = _OFF_GLU + 2 * CONV_WIDTH
_OFF_Q_C = _OFF_Z_B + CONV_WIDTH
_OFF_K_C = _OFF_Q_C + ATT_WIDTH
_OFF_V_C = _OFF_K_C + ATT_KV_WIDTH
_OFF_Z_C = _OFF_V_C + ATT_KV_WIDTH
_OFF_Q_I = _OFF_Z_C + ATT_WIDTH
_OFF_K_I = _OFF_Q_I + IDX_HEADS * IDX_HEAD_DIM
_OFF_W_I = _OFF_K_I + IDX_HEAD_DIM
_OFF_GATE = _OFF_W_I + IDX_HEADS
N_IN = _OFF_GATE + N_BRANCH * D_MODEL

P_QKV_A = 0
P_Z_A = 3072
P_GLU = 4096
P_Z_B = 6144
P_Q_C = 7168
P_Z_C = 8192
P_GATE = 9216
P_K_C = 15360
P_V_C = 15616
NP_WIDE = 15872
S_Q_I = 0
S_K_I = 512
S_W_I = 576
S_BETA = 584
S_ALPHA = 592
NS_NARROW = 640

LANES = 128
SUBLANES = 8
DSA_BLOCK = 128
DSA_QROWS = 256
DSA_KBLOCK = 256
DSA_SUM_ROWS = 16
DSA_BIAS_KINDS = DSA_KBLOCK // DSA_BLOCK + 2
IDX_PAD = 256
VMEM_LIMIT = 56 * 1024 * 1024
NEG_BIG = -1e30
INT_MIN = -(2 ** 31)

NN = (((1,), (0,)), ((), ()))
NT = (((1,), (1,)), ((), ()))
TN = (((0,), (0,)), ((), ()))


def _dg(a, b, dims=NN):
    return lax.dot_general(a, b, dims, preferred_element_type=F32)


def _split2(x):
    hi = x.astype(BF16)
    lo = (x - hi.astype(F32)).astype(BF16)
    return hi, lo


def _dot_b(a, b, dims=NN):
    return _dg(a.astype(BF16), b.astype(BF16), dims)


def _dot_hp(a, b, dims=NN):
    ah, al = _split2(a)
    bh, bl = _split2(b)
    return _dg(ah, bh, dims) + (_dg(al, bh, dims) + _dg(ah, bl, dims))


def _dot_hp_exact_rhs(a, b_bf16, dims=NN):
    a0 = a.astype(BF16)
    r = a - a0.astype(F32)
    a1 = r.astype(BF16)
    a2 = (r - a1.astype(F32)).astype(BF16)
    return _dg(a0, b_bf16, dims) + (_dg(a1, b_bf16, dims) + _dg(a2, b_bf16, dims))


def _sigmoid(x):
    return 1.0 / (1.0 + jnp.exp(-x))


def _silu(x):
    return x * _sigmoid(x)


def _softplus(x):
    return jnp.maximum(x, 0.0) + jnp.log(1.0 + jnp.exp(-jnp.abs(x)))


def _cparams(sem):
    return pltpu.CompilerParams(dimension_semantics=sem, vmem_limit_bytes=VMEM_LIMIT)


def _ada_kernel(c_ref, w_ref, b_ref, o_ref):
    cond = _silu(c_ref[...])
    o_ref[...] = _dot_hp(cond, w_ref[...]) + b_ref[...]


def _ada_modulation(c, w_ada, b_ada):
    depth, d, n = w_ada.shape
    bsz = c.shape[0]
    tn = 512
    return pl.pallas_call(
        _ada_kernel,
        grid=(depth, n // tn),
        in_specs=[
            pl.BlockSpec((bsz, d), lambda l, j: (0, 0)),
            pl.BlockSpec((None, d, tn), lambda l, j: (l, 0, j)),
            pl.BlockSpec((None, 1, tn), lambda l, j: (l, 0, j)),
        ],
        out_specs=pl.BlockSpec((None, bsz, tn), lambda l, j: (l, 0, j)),
        out_shape=jax.ShapeDtypeStruct((depth, bsz, n), F32),
        compiler_params=_cparams(("parallel", "parallel")),
    )(c, w_ada, b_ada.reshape(depth, 1, n))


def _norm_kernel(x_ref, g_ref, sc_ref, sh_ref, o_ref):
    x = x_ref[...]
    ms = jnp.mean(x * x, axis=-1, keepdims=True)
    y = x * lax.rsqrt(ms + EPS) * g_ref[...]
    o_ref[...] = (y * (1.0 + sc_ref[...]) + sh_ref[...]).astype(o_ref.dtype)


def _norm_modulate(x, g, mod):
    bsz, seq, d = x.shape
    tl = 512
    return pl.pallas_call(
        _norm_kernel,
        grid=(bsz, seq // tl),
        in_specs=[
            pl.BlockSpec((None, tl, d), lambda b, i: (b, i, 0)),
            pl.BlockSpec((1, d), lambda b, i: (0, 0)),
            pl.BlockSpec((None, 1, d), lambda b, i: (b, 0, 1)),
            pl.BlockSpec((None, 1, d), lambda b, i: (b, 0, 0)),
        ],
        out_specs=pl.BlockSpec((None, tl, d), lambda b, i: (b, i, 0)),
        out_shape=jax.ShapeDtypeStruct((bsz, seq, d), BF16),
        compiler_params=_cparams(("parallel", "parallel")),
    )(x, g, mod, mod)


def _mm_kernel(a_ref, b_ref, o_ref):
    o_ref[...] = _dg(a_ref[...], b_ref[...]).astype(o_ref.dtype)


def _matmul(a, b, out_dtype, tm, tn):
    m, k = a.shape
    n = b.shape[1]
    return pl.pallas_call(
        _mm_kernel,
        grid=(m // tm, n // tn),
        in_specs=[
            pl.BlockSpec((tm, k), lambda i, j: (i, 0)),
            pl.BlockSpec((k, tn), lambda i, j: (0, j)),
        ],
        out_specs=pl.BlockSpec((tm, tn), lambda i, j: (i, j)),
        out_shape=jax.ShapeDtypeStruct((m, n), out_dtype),
        compiler_params=_cparams(("parallel", "parallel")),
    )(a, b)


def _tri_inverse(mats, eye):
    n = mats[0].shape[0]
    levels = CHUNK.bit_length() - 1
    bs = [-a for a in mats]
    ss = [eye + b for b in bs]
    ps = [_dot_b(b, b) for b in bs]
    for _ in range(1, levels - 1):
        prods = [_dot_b(jnp.concatenate([s, p], axis=0), p) for s, p in zip(ss, ps)]
        ss = [s + pr[:n] for s, pr in zip(ss, prods)]
        ps = [pr[n:] for pr in prods]
    return [s + _dot_b(s, p) for s, p in zip(ss, ps)]


GDN_UNROLL = 8


def _gdn_kernel(hp_ref, q_ref, k_ref, v_ref, z_ref, cwq_ref, cwk_ref, cwv_ref, br_ref, ar_ref,
                gn_ref, o_ref, pad_scr, qn_scr, kn_scr, vn_scr, gc_scr, beta_scr,
                lhs_scr, cadd_scr, *, hb, seq):
    hd = GDN_HEAD_DIM
    nch = seq // CHUNK
    head0 = pl.program_id(1) * hb
    row_tile = min(seq, 256)
    unroll = math.gcd(GDN_UNROLL, nch)

    ii = lax.broadcasted_iota(I32, (CHUNK, CHUNK), 0)
    jj = lax.broadcasted_iota(I32, (CHUNK, CHUNK), 1)
    incl = ii >= jj
    strict = ii > jj
    eye = jnp.where(ii == jj, 1.0, 0.0).astype(F32)
    upper = jnp.where(ii <= jj, 1.0, 0.0).astype(BF16)

    def row_to_col(row):
        return jnp.sum(jnp.where(ii == jj, jnp.broadcast_to(row, (CHUNK, CHUNK)), 0.0),
                       axis=1, keepdims=True)

    for hh in range(hb):
        ls = slice(hh * hd, (hh + 1) * hd)
        for src, cw, dst, mode in ((q_ref, cwq_ref, qn_scr, "q"), (k_ref, cwk_ref, kn_scr, "k"),
                                   (v_ref, cwv_ref, vn_scr, "v")):
            pad_scr[0:8, :] = jnp.zeros((8, hd), F32)
            pad_scr[8:8 + seq, :] = src[:, ls].astype(F32)
            for rt in range(seq // row_tile):
                base = rt * row_tile
                acc = jnp.zeros((row_tile, hd), F32)
                for kk in range(GDN_CONV):
                    off = base + 8 - (GDN_CONV - 1) + kk
                    acc = acc + cw[kk:kk + 1, ls] * pad_scr[off:off + row_tile, :]
                y = _silu(acc)
                if mode != "v":
                    y = y * lax.rsqrt(jnp.sum(y * y, axis=-1, keepdims=True) + EPS)
                if mode == "q":
                    y = y * (hd ** -0.5)
                dst[hh, base:base + row_tile, :] = y
        a_log = hp_ref[head0 + hh, 0]
        dt_b = hp_ref[head0 + hh, 1]
        g2 = -jnp.exp(jnp.full((nch, CHUNK), a_log, F32)) * _softplus(ar_ref[hh] + dt_b)
        gc_scr[hh] = _dot_hp_exact_rhs(g2, upper)
        beta_scr[hh] = _sigmoid(br_ref[hh])

    i_hd = lax.broadcasted_iota(I32, (hd, hd), 0)
    j_hd = lax.broadcasted_iota(I32, (hd, hd), 1)
    eye_hd = jnp.where(i_hd == j_hd, 1.0, 0.0).astype(BF16)

    def load_chunk(n, hh):
        r0 = pl.multiple_of(n * CHUNK, CHUNK)
        return (qn_scr[hh, pl.ds(r0, CHUNK), :], kn_scr[hh, pl.ds(r0, CHUNK), :],
                vn_scr[hh, pl.ds(r0, CHUNK), :], gc_scr[hh, pl.ds(n, 1), :],
                beta_scr[hh, pl.ds(n, 1), :])

    def compute_chunks(loaded):
        qs, ks, vs, gc_rows, beta_rows = zip(*loaded)
        gc_cols = [row_to_col(r) for r in gc_rows]
        beta_cols = [row_to_col(r) for r in beta_rows]
        decays = [jnp.where(incl, jnp.exp(jnp.where(incl, c - r, 0.0)), 0.0)
                  for c, r in zip(gc_cols, gc_rows)]
        k16s = [k.astype(BF16) for k in ks]
        qk_kks = [_dg(jnp.concatenate([q.astype(BF16), k16], axis=0), k16, NT)
                  for q, k16 in zip(qs, k16s)]
        intras = [jnp.where(incl, m[:CHUNK] * d, 0.0).astype(BF16) for m, d in zip(qk_kks, decays)]
        t_invs = _tri_inverse([jnp.where(strict, b * m[CHUNK:] * d, 0.0)
                               for b, m, d in zip(beta_cols, qk_kks, decays)], eye)
        e_gcs = [jnp.exp(c) for c in gc_cols]
        uws = [_dot_b(t, jnp.concatenate([v * b, k * (b * e)], axis=1))
               for t, v, k, b, e in zip(t_invs, vs, ks, beta_cols, e_gcs)]
        kdts = [(k * jnp.exp(r[:, CHUNK - 1:CHUNK] - c)).T.astype(BF16)
                for k, r, c in zip(ks, gc_rows, gc_cols)]
        wu16s = [jnp.concatenate([uw[:, hd:], uw[:, :hd]], axis=1).astype(BF16) for uw in uws]
        kwcs = [_dg(kdt, wu) for kdt, wu in zip(kdts, wu16s)]
        iwus = [_dg(intra, wu) for intra, wu in zip(intras, wu16s)]
        lhs = [jnp.concatenate([kwc[:, :hd], q * e - iwu[:, :hd]], axis=0).astype(BF16)
               for kwc, q, e, iwu in zip(kwcs, qs, e_gcs, iwus)]
        return [(l, kwc[:, hd:], iwu[:, hd:]) for l, kwc, iwu in zip(lhs, kwcs, iwus)]

    def store_chunk(n, hh, lhs, c_add, o_add):
        lhs_scr[hh, n] = lhs
        cadd_scr[hh, n] = c_add
        vn_scr[hh, pl.ds(pl.multiple_of(n * CHUNK, CHUNK), CHUNK), :] = o_add

    def prepare_body(step, carry):
        jobs = [(step * unroll + cu, hh) for cu in range(unroll) for hh in range(hb)]
        results = compute_chunks([load_chunk(n, hh) for n, hh in jobs])
        for (n, hh), res in zip(jobs, results):
            store_chunk(n, hh, *res)
        return carry

    lax.fori_loop(0, nch // unroll, prepare_body, 0)

    def scan_body(n, states):
        r0 = pl.multiple_of(n * CHUNK, CHUNK)
        loaded = [(lhs_scr[hh, n], cadd_scr[hh, n], vn_scr[hh, pl.ds(r0, CHUNK), :],
                   gc_scr[hh, pl.ds(n, 1), CHUNK - 1:CHUNK]) for hh in range(hb)]
        lhss, c_adds, o_adds, g_lasts = zip(*loaded)
        prods = [_dg(l, s.astype(BF16)) for l, s in zip(lhss, states)]
        new_states = [s * jnp.exp(g) - p[:hd] + c for s, g, p, c in zip(states, g_lasts, prods, c_adds)]
        for hh in range(hb):
            qn_scr[hh, pl.ds(r0, CHUNK), :] = prods[hh][hd:] + o_adds[hh]
        return tuple(new_states)

    lax.fori_loop(0, nch, scan_body, tuple(jnp.zeros((hd, hd), F32) for _ in range(hb)))

    for hh in range(hb):
        ls = slice(hh * hd, (hh + 1) * hd)
        for rt in range(seq // row_tile):
            rs = slice(rt * row_tile, (rt + 1) * row_tile)
            o = qn_scr[hh, rs, :]
            on = o * lax.rsqrt(jnp.mean(o * o, axis=-1, keepdims=True) + EPS) * gn_ref[...]
            o_ref[rs, ls] = (on * _silu(z_ref[rs, ls].astype(F32))).astype(o_ref.dtype)


def _gdn_mixer(p3, beta_r, alpha_r, conv_w, head_params, gn, hb=2):
    bsz, seq, _ = p3.shape
    wb = hb * GDN_HEAD_DIM
    nhb = GDN_HEADS // hb
    nch = seq // CHUNK
    qb, kb_, vb, zb = (P_QKV_A // wb, (P_QKV_A + GDN_WIDTH) // wb, (P_QKV_A + 2 * GDN_WIDTH) // wb,
                       P_Z_A // wb)
    col = lambda off: pl.BlockSpec((None, seq, wb), lambda b, h, off=off: (b, 0, off + h))
    cwspec = lambda off: pl.BlockSpec((GDN_CONV, wb), lambda b, h, off=off: (0, off + h))
    rows = pl.BlockSpec((None, hb, nch, CHUNK), lambda b, h: (b, h, 0, 0))
    return pl.pallas_call(
        functools.partial(_gdn_kernel, hb=hb, seq=seq),
        grid=(bsz, nhb),
        in_specs=[
            pl.BlockSpec(memory_space=pltpu.SMEM),
            col(qb), col(kb_), col(vb), col(zb),
            cwspec(0), cwspec(nhb), cwspec(2 * nhb),
            rows, rows,
            pl.BlockSpec((1, GDN_HEAD_DIM), lambda b, h: (0, 0)),
        ],
        out_specs=pl.BlockSpec((None, seq, wb), lambda b, h: (b, 0, h)),
        out_shape=jax.ShapeDtypeStruct((bsz, seq, GDN_WIDTH), BF16),
        scratch_shapes=[
            pltpu.VMEM((seq + 8, GDN_HEAD_DIM), F32),
            pltpu.VMEM((hb, seq, GDN_HEAD_DIM), F32),
            pltpu.VMEM((hb, seq, GDN_HEAD_DIM), F32),
            pltpu.VMEM((hb, seq, GDN_HEAD_DIM), F32),
            pltpu.VMEM((hb, nch, CHUNK), F32),
            pltpu.VMEM((hb, nch, CHUNK), F32),
            pltpu.VMEM((hb, nch, GDN_HEAD_DIM + CHUNK, GDN_HEAD_DIM), BF16),
            pltpu.VMEM((hb, nch, GDN_HEAD_DIM, GDN_HEAD_DIM), F32),
        ],
        compiler_params=_cparams(("parallel", "parallel")),
    )(head_params, p3, p3, p3, p3, conv_w, conv_w, conv_w, beta_r, alpha_r, gn)


CONF_HALO = 32
CONF_ROWS = 32
CONF_COPY_ROWS = 64


def _conf_kernel(a_ref, b_ref, ap_ref, bp_ref, z_ref, w_ref, cb_ref, lg_ref, lb_ref, o_ref, glu_scr,
                 *, tl):
    i = pl.program_id(1)
    rows = CONF_HALO + tl
    for rt in range(tl // CONF_COPY_ROWS):
        rs = slice(rt * CONF_COPY_ROWS, (rt + 1) * CONF_COPY_ROWS)
        glu_scr[0, CONF_HALO + rs.start:CONF_HALO + rs.stop, :] = (
            a_ref[rs, :].astype(F32) * _sigmoid(b_ref[rs, :].astype(F32)))
    prev = ap_ref[...].astype(F32) * _sigmoid(bp_ref[...].astype(F32))
    glu_scr[0, 0:CONF_HALO, :] = jnp.where(i > 0, prev, 0.0)
    for s in range(1, SUBLANES):
        for r0 in range(0, rows - SUBLANES, CONF_COPY_ROWS):
            n = min(CONF_COPY_ROWS, rows - SUBLANES - r0)
            glu_scr[s, r0:r0 + n, :] = glu_scr[0, r0 + s:r0 + s + n, :]
    groups = CONF_ROWS // SUBLANES
    for rt in range(tl // CONF_ROWS):
        base = rt * CONF_ROWS
        acc = jnp.broadcast_to(cb_ref[...], (groups, SUBLANES, CONV_WIDTH))
        for kk in range(CONV_KERNEL):
            off = base + CONF_HALO - (CONV_KERNEL - 1) + kk
            s = off % SUBLANES
            win = glu_scr[s, off - s:off - s + CONF_ROWS, :].reshape(groups, SUBLANES, CONV_WIDTH)
            acc = acc + w_ref[kk][None] * win
        acc = acc.reshape(CONF_ROWS, CONV_WIDTH)
        mu = jnp.mean(acc, axis=-1, keepdims=True)
        cen = acc - mu
        var = jnp.mean(cen * cen, axis=-1, keepdims=True)
        y = cen * lax.rsqrt(var + EPS) * lg_ref[...] + lb_ref[...]
        z = z_ref[base:base + CONF_ROWS, :].astype(F32)
        o_ref[base:base + CONF_ROWS, :] = (_silu(y) * _silu(z)).astype(o_ref.dtype)


def _conformer_mixer(p3, conv_w, conv_b, ln_g, ln_b):
    bsz, seq, _ = p3.shape
    tl = 256
    cw = CONV_WIDTH
    ca, cb, cz = P_GLU // cw, P_GLU // cw + 1, P_Z_B // cw
    halo_per_tile = tl // CONF_HALO
    cur = lambda c: pl.BlockSpec((None, tl, cw), lambda b, i, c=c: (b, i, c))
    prev = lambda c: pl.BlockSpec(
        (None, CONF_HALO, cw), lambda b, i, c=c: (b, jnp.maximum(i * halo_per_tile - 1, 0), c))
    vec = pl.BlockSpec((1, cw), lambda b, i: (0, 0))
    return pl.pallas_call(
        functools.partial(_conf_kernel, tl=tl),
        grid=(bsz, seq // tl),
        in_specs=[cur(ca), cur(cb), prev(ca), prev(cb), cur(cz),
                  pl.BlockSpec((CONV_KERNEL, SUBLANES, cw), lambda b, i: (0, 0, 0)), vec, vec, vec],
        out_specs=pl.BlockSpec((None, tl, cw), lambda b, i: (b, i, 0)),
        out_shape=jax.ShapeDtypeStruct((bsz, seq, cw), BF16),
        scratch_shapes=[pltpu.VMEM((SUBLANES, CONF_HALO + tl, cw), F32)],
        compiler_params=_cparams(("parallel", "parallel")),
    )(p3, p3, p3, p3, p3, jnp.broadcast_to(conv_w[:, None, :], (CONV_KERNEL, SUBLANES, cw)), conv_b, ln_g, ln_b)


def _t5_bucket_int(rel):
    nb = REL_BUCKETS // 2
    max_exact = nb // 2
    n = jnp.abs(rel)
    large = jnp.full(rel.shape, max_exact, I32)
    for thr in (12, 16, 23, 32, 46, 64, 91):
        large = large + jnp.where(n >= thr, 1, 0)
    large = jnp.minimum(large, nb - 1)
    return jnp.where(rel > 0, nb, 0) + jnp.where(n < max_exact, n, large)


def _bias_kernel(rb_ref, o_ref):
    kpos = lax.broadcasted_iota(I32, (DSA_KBLOCK, DSA_BLOCK), 0)
    qpos = lax.broadcasted_iota(I32, (DSA_KBLOCK, DSA_BLOCK), 1)
    for kind in range(DSA_BIAS_KINDS):
        bucket = _t5_bucket_int(kpos - qpos + (kind - (DSA_BIAS_KINDS - 1)) * DSA_BLOCK)
        for h in range(ATT_HEADS):
            val = jnp.zeros((DSA_KBLOCK, DSA_BLOCK), F32)
            for b in range(REL_BUCKETS):
                val = jnp.where(bucket == b, rb_ref[b, h], val)
            o_ref[kind, h] = val


def _bias_tiles(rel_bias):
    return pl.pallas_call(
        _bias_kernel,
        in_specs=[pl.BlockSpec(memory_space=pltpu.SMEM)],
        out_shape=jax.ShapeDtypeStruct((DSA_BIAS_KINDS, ATT_HEADS, DSA_KBLOCK, DSA_BLOCK), F32),
    )(rel_bias)


def _for_block_pairs(n, fn, carry):
    def pair(p, c):
        return fn([2 * p, 2 * p + 1], c)

    c = lax.fori_loop(0, lax.shift_right_logical(n, 1), pair, carry)

    @pl.when((n & 1) == 1)
    def _last():
        fn([n - 1], c)


def _dsa_kernel(qi_ref, ki_ref, wt_ref, qc_ref, zc_ref, kc_ref, vc_ref, qg_ref, kg_ref, tab_ref,
                o_ref, kp_scr, kn_scr, vt_scr, qp_scr, key_scr, sel_scr, qs_scr, lg_scr, mx_scr,
                acc_scr, *, seq, topk):
    blk = DSA_BLOCK
    qrows = DSA_QROWS
    kblk = DSA_KBLOCK
    hd = ATT_HEAD_DIM
    i = pl.program_id(1)
    nkv = (i * qrows) // kblk + 1
    kf = float(topk)

    @pl.when(i == 0)
    def _prepare_keys():
        e_r = lax.broadcasted_iota(I32, (hd, hd), 0)
        e_c = lax.broadcasted_iota(I32, (hd, hd), 1)
        eye_hd = jnp.where(e_r == e_c, 1.0, 0.0).astype(BF16)
        low_k = lax.broadcasted_iota(I32, (kblk, LANES), 1) < IDX_HEAD_DIM
        for t in range(seq // kblk):
            rs = slice(t * kblk, (t + 1) * kblk)
            kt = ki_ref[rs, :]
            hi = kt.astype(BF16).astype(F32)
            lo = kt - hi
            kp_scr[rs, 0:LANES] = jnp.where(low_k, hi, pltpu.roll(hi, IDX_HEAD_DIM, 1)).astype(BF16)
            kp_scr[rs, LANES:2 * LANES] = jnp.where(low_k, lo, 0.0).astype(BF16)
            kc = kc_ref[rs, :].astype(F32)
            for g in range(ATT_KV_HEADS):
                ls = slice(g * hd, (g + 1) * hd)
                kg = kc[:, ls]
                kgn = kg * lax.rsqrt(jnp.mean(kg * kg, axis=-1, keepdims=True) + EPS) * kg_ref[...]
                kn_scr[rs, ls] = kgn.astype(BF16)
                vt_scr[t, g, 0:hd, :] = _dg(eye_hd, vc_ref[rs, ls], NT).astype(BF16)
                vt_scr[t, g, hd:, :] = jnp.ones((DSA_SUM_ROWS, kblk), BF16)

    low_q = lax.broadcasted_iota(I32, (qrows, LANES), 1) < IDX_HEAD_DIM
    for c in range(IDX_HEADS * IDX_HEAD_DIM // LANES):
        qt = qi_ref[:, c * LANES:(c + 1) * LANES]
        hi = qt.astype(BF16).astype(F32)
        lo = qt - hi
        hi_sw = pltpu.roll(hi, IDX_HEAD_DIM, 1)
        lo_sw = pltpu.roll(lo, IDX_HEAD_DIM, 1)
        for h, first, second in ((2 * c, jnp.where(low_q, hi, lo_sw), jnp.where(low_q, hi, 0.0)),
                                 (2 * c + 1, jnp.where(low_q, hi_sw, lo), jnp.where(low_q, hi_sw, 0.0))):
            qp_scr[h * qrows:(h + 1) * qrows, 0:LANES] = first.astype(BF16)
            qp_scr[h * qrows:(h + 1) * qrows, LANES:2 * LANES] = second.astype(BF16)
    wsc = wt_ref[...] * ((IDX_HEAD_DIM ** -0.5) * (IDX_HEADS ** -0.5))

    kpos = lax.broadcasted_iota(I32, (kblk, qrows), 0)
    qpos = lax.broadcasted_iota(I32, (kblk, qrows), 1)
    chunk_gap = (kpos // CHUNK) - (qpos // CHUNK)
    q_chunk0 = i * (qrows // CHUNK)

    def fold_rows(x):
        return x.reshape(kblk // SUBLANES, SUBLANES, x.shape[-1])

    def score_blocks(js, carry):
        qp = qp_scr[...]
        s_alls = [_dg(kp_scr[pl.ds(pl.multiple_of(j * kblk, kblk), kblk), :], qp, NT) for j in js]
        for j, s_all in zip(js, s_alls):
            sc = jnp.zeros((kblk, qrows), F32)
            for h in range(IDX_HEADS):
                sc = sc + wsc[h:h + 1, :] * jnp.maximum(s_all[:, h * qrows:(h + 1) * qrows], 0.0)
            sc = sc + 0.0
            bits = pltpu.bitcast(sc, I32)
            key = jnp.where(bits < 0, bits ^ jnp.int32(0x7FFFFFFF), bits)
            slack = q_chunk0 - j * (kblk // CHUNK)
            key_scr[j] = jnp.where(chunk_gap <= slack, key, jnp.int32(INT_MIN))
        return carry

    _for_block_pairs(nkv, score_blocks, 0)

    def count_where(pred, ref):
        def block(j):
            return jnp.sum(fold_rows(jnp.where(pred(key_scr[j], ref), 1.0, 0.0)), axis=0)

        acc = lax.fori_loop(0, lax.shift_right_logical(nkv, 1),
                            lambda p, a: a + (block(2 * p) + block(2 * p + 1)),
                            jnp.zeros((SUBLANES, qrows), F32))
        acc = lax.cond((nkv & 1) == 1, lambda a: a + block(nkv - 1), lambda a: a, acc)
        return jnp.sum(acc, axis=0, keepdims=True)

    ge = lambda kv, ref: kv >= ref
    cnt0 = count_where(ge, jnp.zeros((1, qrows), I32))
    cur0 = jnp.where(cnt0 >= kf, jnp.int32(0), jnp.int32(INT_MIN))

    def bit_body(t, cur):
        cand = cur | lax.shift_left(jnp.int32(1), jnp.int32(30) - t)
        return jnp.where(count_where(ge, cand) >= kf, cand, cur)

    thr = lax.fori_loop(0, 31, bit_body, cur0)
    need = kf - count_where(lambda kv, ref: kv > ref, thr)

    ku = lax.broadcasted_iota(I32, (kblk, kblk), 0)
    kv_ = lax.broadcasted_iota(I32, (kblk, kblk), 1)
    lower = jnp.where(ku >= kv_, 1.0, 0.0).astype(BF16)

    def sel_blocks(js, run):
        keys = [key_scr[j] for j in js]
        eqs = [key == thr for key in keys]
        pres = [_dg(lower, jnp.where(eq, 1.0, 0.0).astype(BF16)) for eq in eqs]
        for j, key, eq, pre in zip(js, keys, eqs, pres):
            take_tie = jnp.where(eq & (run + pre <= need), 0.0, NEG_BIG)
            take = jnp.where(key > thr, 0.0, take_tie)
            sel_scr[j] = jnp.where(key == jnp.int32(INT_MIN), NEG_BIG, take)
            run = run + pre[kblk - 1:kblk, :]
        return run

    _for_block_pairs(nkv, sel_blocks, jnp.zeros((1, qrows), F32))

    for a in range(qrows // blk):
        _dsa_attend(a, i * (qrows // blk) + a, nkv, qc_ref, zc_ref, qg_ref, tab_ref, o_ref,
                    kn_scr, vt_scr, sel_scr, qs_scr, lg_scr, mx_scr, acc_scr)


def _dsa_attend(a, qblock, nkv, qc_ref, zc_ref, qg_ref, tab_ref, o_ref,
                kn_scr, vt_scr, sel_scr, qs_scr, lg_scr, mx_scr, acc_scr):
    blk = DSA_BLOCK
    kblk = DSA_KBLOCK
    hd = ATT_HEAD_DIM
    cols4 = ATT_GROUP * blk
    rq = slice(a * blk, (a + 1) * blk)
    qc = qc_ref[rq, :].astype(F32)
    for h in range(ATT_HEADS):
        qh = qc[:, h * hd:(h + 1) * hd]
        qn = qh * lax.rsqrt(jnp.mean(qh * qh, axis=-1, keepdims=True) + EPS) * qg_ref[...]
        g, hh = divmod(h, ATT_GROUP)
        qs_scr[g, hh * blk:(hh + 1) * blk, :] = (qn * (hd ** -0.5)).astype(BF16)
    mx_scr[...] = jnp.full(mx_scr.shape, NEG_BIG, F32)

    def fold(x):
        return x.reshape(kblk // SUBLANES, SUBLANES, x.shape[-1])

    groups = range(ATT_KV_HEADS)

    def logit_blocks(js, carry):
        jg = [(n, g) for n in range(len(js)) for g in groups]
        kks = [kn_scr[pl.ds(pl.multiple_of(j * kblk, kblk), kblk), :] for j in js]
        raw = [_dg(kks[n][:, g * hd:(g + 1) * hd], qs_scr[g], NT) for n, g in jg]
        negs = [jnp.concatenate([sel_scr[j, :, rq]] * ATT_GROUP, axis=1) for j in js]
        kinds = [jnp.clip(j * (kblk // blk) - qblock + (DSA_BIAS_KINDS - 1), 0, DSA_BIAS_KINDS - 1)
                 for j in js]
        tops = [mx_scr[g] for g in groups]
        for (n, g), r in zip(jg, raw):
            bias = jnp.concatenate([tab_ref[kinds[n], g * ATT_GROUP + hh] for hh in range(ATT_GROUP)],
                                   axis=1)
            lg = r + bias + negs[n]
            lg_scr[js[n], g] = lg
            tops[g] = jnp.maximum(tops[g], jnp.max(fold(lg), axis=0))
        for g in groups:
            mx_scr[g] = tops[g]
        return carry

    _for_block_pairs(nkv, logit_blocks, 0)
    for g in groups:
        mx_scr[g] = jnp.broadcast_to(jnp.max(mx_scr[g], axis=0, keepdims=True), (SUBLANES, cols4))
    acc_scr[...] = jnp.zeros(acc_scr.shape, F32)

    def pv_blocks(js, carry):
        jg = [(j, g) for j in js for g in groups]
        ps = [jnp.exp(lg_scr[j, g] - mx_scr[g, 0:1, :]).astype(BF16) for j, g in jg]
        pvs = [_dg(vt_scr[j, g], p) for (j, g), p in zip(jg, ps)]
        for g in groups:
            acc_scr[g] += sum(pv for (_, gg), pv in zip(jg, pvs) if gg == g)
        return carry

    _for_block_pairs(nkv, pv_blocks, 0)

    for h in range(ATT_HEADS):
        g, hh = divmod(h, ATT_GROUP)
        cs = slice(hh * blk, (hh + 1) * blk)
        ls = slice(h * hd, (h + 1) * hd)
        o_t = acc_scr[g, 0:hd, cs] / acc_scr[g, hd:hd + 1, cs]
        z = zc_ref[rq, ls].astype(F32)
        o_ref[rq, ls] = (o_t.T * _silu(z)).astype(o_ref.dtype)


def _dsa_mixer(p3, s3, w_t, q_gain, k_gain, bias_tab):
    bsz, seq, _ = p3.shape
    blk = DSA_BLOCK
    kblk = DSA_KBLOCK
    qrows = DSA_QROWS
    nkb = seq // kblk
    rows4 = ATT_GROUP * blk
    topk = min(TOPK_MAX, seq // 4)
    return pl.pallas_call(
        functools.partial(_dsa_kernel, seq=seq, topk=topk),
        grid=(bsz, seq // qrows),
        in_specs=[
            pl.BlockSpec((None, qrows, IDX_HEADS * IDX_HEAD_DIM), lambda b, i: (b, i, S_Q_I // (IDX_HEADS * IDX_HEAD_DIM))),
            pl.BlockSpec((None, seq, LANES), lambda b, i: (b, 0, S_K_I // LANES)),
            pl.BlockSpec((None, IDX_HEADS, qrows), lambda b, i: (b, 0, i)),
            pl.BlockSpec((None, qrows, ATT_WIDTH), lambda b, i: (b, i, P_Q_C // ATT_WIDTH)),
            pl.BlockSpec((None, qrows, ATT_WIDTH), lambda b, i: (b, i, P_Z_C // ATT_WIDTH)),
            pl.BlockSpec((None, seq, ATT_KV_WIDTH), lambda b, i: (b, 0, P_K_C // ATT_KV_WIDTH)),
            pl.BlockSpec((None, seq, ATT_KV_WIDTH), lambda b, i: (b, 0, P_V_C // ATT_KV_WIDTH)),
            pl.BlockSpec((1, ATT_HEAD_DIM), lambda b, i: (0, 0)),
            pl.BlockSpec((1, ATT_HEAD_DIM), lambda b, i: (0, 0)),
            pl.BlockSpec((DSA_BIAS_KINDS, ATT_HEADS, kblk, blk), lambda b, i: (0, 0, 0, 0)),
        ],
        out_specs=pl.BlockSpec((None, qrows, ATT_WIDTH), lambda b, i: (b, i, 0)),
        out_shape=jax.ShapeDtypeStruct((bsz, seq, ATT_WIDTH), BF16),
        scratch_shapes=[
            pltpu.VMEM((seq, IDX_PAD), BF16),
            pltpu.VMEM((seq, ATT_KV_WIDTH), BF16),
            pltpu.VMEM((nkb, ATT_KV_HEADS, ATT_HEAD_DIM + DSA_SUM_ROWS, kblk), BF16),
            pltpu.VMEM((IDX_HEADS * qrows, IDX_PAD), BF16),
            pltpu.VMEM((nkb, kblk, qrows), I32),
            pltpu.VMEM((nkb, kblk, qrows), F32),
            pltpu.VMEM((ATT_KV_HEADS, rows4, ATT_HEAD_DIM), BF16),
            pltpu.VMEM((nkb, ATT_KV_HEADS, kblk, rows4), F32),
            pltpu.VMEM((ATT_KV_HEADS, SUBLANES, rows4), F32),
            pltpu.VMEM((ATT_KV_HEADS, ATT_HEAD_DIM + DSA_SUM_ROWS, rows4), F32),
        ],
        compiler_params=_cparams(("arbitrary", "arbitrary")),
    )(s3, s3, w_t, p3, p3, p3, p3, q_gain, k_gain, bias_tab)


def _merge_kernel(ya_ref, yb_ref, yc_ref, wa_ref, wb_ref, wc_ref, ga_ref, gb_ref, gc_ref,
                  ba_ref, bb_ref, bc_ref, o_ref):
    acc = _sigmoid(ga_ref[...].astype(F32) + ba_ref[...]) * _dg(ya_ref[...], wa_ref[...])
    acc = acc + _sigmoid(gb_ref[...].astype(F32) + bb_ref[...]) * _dg(yb_ref[...], wb_ref[...])
    acc = acc + _sigmoid(gc_ref[...].astype(F32) + bc_ref[...]) * _dg(yc_ref[...], wc_ref[...])
    o_ref[...] = acc.astype(o_ref.dtype)


def _merge(ya, yb, yc, wa, wb, wc, p2, b_gate):
    t = ya.shape[0]
    d = D_MODEL
    tm, tn = min(t, 1024), 512
    nj = d // tn
    g0 = P_GATE // tn
    yspec = pl.BlockSpec((tm, ya.shape[1]), lambda i, j: (i, 0))
    wspec = pl.BlockSpec((wa.shape[0], tn), lambda i, j: (0, j))
    gspec = lambda br: pl.BlockSpec((tm, tn), lambda i, j, br=br: (i, g0 + br * nj + j))
    bspec = lambda br: pl.BlockSpec((1, tn), lambda i, j, br=br: (0, br * nj + j))
    return pl.pallas_call(
        _merge_kernel,
        grid=(t // tm, nj),
        in_specs=[yspec, yspec, yspec, wspec, wspec, wspec, gspec(0), gspec(1), gspec(2),
                  bspec(0), bspec(1), bspec(2)],
        out_specs=pl.BlockSpec((tm, tn), lambda i, j: (i, j)),
        out_shape=jax.ShapeDtypeStruct((t, d), BF16),
        compiler_params=_cparams(("parallel", "parallel")),
    )(ya, yb, yc, wa, wb, wc, p2, p2, p2, b_gate, b_gate, b_gate)


def _out_kernel(m_ref, w_ref, x_ref, g_ref, o_ref):
    o_ref[...] = x_ref[...] + g_ref[...] * _dg(m_ref[...], w_ref[...])


def _out_proj(merged, w_out, x2, mod, seq):
    t, d = x2.shape
    tm, tn = min(seq, 2048), 512
    per_batch = seq // tm
    nj = d // tn
    return pl.pallas_call(
        _out_kernel,
        grid=(t // tm, nj),
        in_specs=[
            pl.BlockSpec((tm, d), lambda i, j: (i, 0)),
            pl.BlockSpec((d, tn), lambda i, j: (0, j)),
            pl.BlockSpec((tm, tn), lambda i, j: (i, j)),
            pl.BlockSpec((None, 1, tn), lambda i, j: (i // per_batch, 0, 2 * nj + j)),
        ],
        out_specs=pl.BlockSpec((tm, tn), lambda i, j: (i, j)),
        out_shape=jax.ShapeDtypeStruct((t, d), F32),
        compiler_params=_cparams(("parallel", "parallel")),
    )(merged, w_out, x2, mod)


def _layer_weight_bf16(w_in, layer):
    cuts = [0, _OFF_Z_A + GDN_WIDTH, (_OFF_GATE // LANES) * LANES, w_in.shape[2]]
    spans = [(a, b, w_in[layer, :, a:b].astype(BF16)) for a, b in zip(cuts[:-1], cuts[1:])]

    def cols(o, n):
        for a, b, w in spans:
            if a <= o and o + n <= b:
                return w[:, o - a:o - a + n]
        left = [(a, b, w) for a, b, w in spans if a <= o < b][0]
        right = [(a, b, w) for a, b, w in spans if a < o + n <= b][0]
        return jnp.concatenate([left[2][:, o - left[0]:], right[2][:, :o + n - right[0]]], axis=1)

    return cols


def _wide_weight(cols):
    pieces = [(_OFF_QKV_A, 3 * GDN_WIDTH), (_OFF_Z_A, GDN_WIDTH), (_OFF_GLU, 2 * CONV_WIDTH),
              (_OFF_Z_B, CONV_WIDTH), (_OFF_Q_C, ATT_WIDTH), (_OFF_Z_C, ATT_WIDTH),
              (_OFF_GATE, N_BRANCH * D_MODEL), (_OFF_K_C, ATT_KV_WIDTH), (_OFF_V_C, ATT_KV_WIDTH)]
    return jnp.concatenate([cols(o, n) for o, n in pieces], axis=1)


def _narrow_weight(cols, rows):
    pieces = [(_OFF_Q_I, IDX_HEADS * IDX_HEAD_DIM), (_OFF_K_I, IDX_HEAD_DIM), (_OFF_W_I, IDX_HEADS),
              (_OFF_BETA, GDN_HEADS), (_OFF_ALPHA, GDN_HEADS)]
    used = sum(n for _, n in pieces)
    parts = [cols(o, n) for o, n in pieces]
    parts.append(jnp.zeros((rows, NS_NARROW - used), BF16))
    return jnp.concatenate(parts, axis=1)


def kernel(x, c, rel_bias, w_ada, b_ada, norm_g, w_in, b_gate, conv_a_w, a_log, dt_bias, gdn_norm_g,
           w_proj_a, conv_b_w, conv_b_b, ln_b_g, ln_b_b, w_proj_b, q_norm_g, k_norm_g, w_proj_c, w_out):
    bsz, seq, d = x.shape
    t = bsz * seq
    nch = seq // CHUNK
    mod_all = _ada_modulation(c, w_ada, b_ada)
    bias_tab = _bias_tiles(rel_bias)
    for layer in range(DEPTH):
        mod = mod_all[layer].reshape(bsz, 1, 3 * d)
        h = _norm_modulate(x, norm_g[layer].reshape(1, d), mod).reshape(t, d)
        w_cols = _layer_weight_bf16(w_in, layer)
        p2 = _matmul(h, _wide_weight(w_cols), BF16, min(t, 2048), 512)
        s2 = _matmul(h, _narrow_weight(w_cols, d), F32, min(t, 1024), NS_NARROW)
        p3 = p2.reshape(bsz, seq, NP_WIDE)
        s3 = s2.reshape(bsz, seq, NS_NARROW)

        def head_rows(off):
            r = s3[:, :, off:off + GDN_HEADS]
            return jnp.swapaxes(r, 1, 2).reshape(bsz, GDN_HEADS, nch, CHUNK)

        head_params = jnp.stack([a_log[layer], dt_bias[layer]], axis=1)
        y_a = _gdn_mixer(p3, head_rows(S_BETA), head_rows(S_ALPHA), conv_a_w[layer], head_params,
                         gdn_norm_g[layer].reshape(1, GDN_HEAD_DIM))
        y_b = _conformer_mixer(p3, conv_b_w[layer], conv_b_b[layer].reshape(1, CONV_WIDTH),
                               ln_b_g[layer].reshape(1, CONV_WIDTH), ln_b_b[layer].reshape(1, CONV_WIDTH))
        y_c = _dsa_mixer(p3, s3, jnp.swapaxes(s3[:, :, S_W_I:S_W_I + IDX_HEADS], 1, 2),
                         q_norm_g[layer].reshape(1, ATT_HEAD_DIM), k_norm_g[layer].reshape(1, ATT_HEAD_DIM),
                         bias_tab)
        merged = _merge(y_a.reshape(t, GDN_WIDTH), y_b.reshape(t, CONV_WIDTH), y_c.reshape(t, ATT_WIDTH),
                        w_proj_a[layer].astype(BF16), w_proj_b[layer].astype(BF16),
                        w_proj_c[layer].astype(BF16), p2, b_gate[layer].reshape(1, N_BRANCH * d))
        x = _out_proj(merged, w_out[layer].astype(BF16), x.reshape(t, d), mod, seq).reshape(bsz, seq, d)
    return x
```

```python
import functools
import math

import jax
import jax.numpy as jnp
from jax import lax
from jax.experimental import pallas as pl
from jax.experimental.pallas import tpu as pltpu

F32 = jnp.float32
BF16 = jnp.bfloat16
I32 = jnp.int32

D_MODEL = 2048
DEPTH = 4
CHUNK = 64
GDN_HEADS = 8
GDN_HEAD_DIM = 128
GDN_WIDTH = GDN_HEADS * GDN_HEAD_DIM
GDN_CONV = 4
CONV_WIDTH = 1024
CONV_KERNEL = 31
ATT_HEADS = 8
ATT_KV_HEADS = 2
ATT_HEAD_DIM = 128
ATT_WIDTH = ATT_HEADS * ATT_HEAD_DIM
ATT_KV_WIDTH = ATT_KV_HEADS * ATT_HEAD_DIM
ATT_GROUP = ATT_HEADS // ATT_KV_HEADS
IDX_HEADS = 8
IDX_HEAD_DIM = 64
TOPK_MAX = 256
REL_BUCKETS = 32
N_BRANCH = 3
EPS = 1e-6

_OFF_QKV_A = 0
_OFF_Z_A = _OFF_QKV_A + 3 * GDN_WIDTH
_OFF_BETA = _OFF_Z_A + GDN_WIDTH
_OFF_ALPHA = _OFF_BETA + GDN_HEADS
_OFF_GLU = _OFF_ALPHA + GDN_HEADS
_OFF_Z_B = _OFF_GLU + 2 * CONV_WIDTH
_OFF_Q_C = _OFF_Z_B + CONV_WIDTH
_OFF_K_C = _OFF_Q_C + ATT_WIDTH
_OFF_V_C = _OFF_K_C + ATT_KV_WIDTH
_OFF_Z_C = _OFF_V_C + ATT_KV_WIDTH
_OFF_Q_I = _OFF_Z_C + ATT_WIDTH
_OFF_K_I = _OFF_Q_I + IDX_HEADS * IDX_HEAD_DIM
_OFF_W_I = _OFF_K_I + IDX_HEAD_DIM
_OFF_GATE = _OFF_W_I + IDX_HEADS
N_IN = _OFF_GATE + N_BRANCH * D_MODEL

P_QKV_A = 0
P_Z_A = 3072
P_GLU = 4096
P_Z_B = 6144
P_Q_C = 7168
P_Z_C = 8192
P_GATE = 9216
P_K_C = 15360
P_V_C = 15616
NP_WIDE = 15872
S_Q_I = 0
S_K_I = 512
S_W_I = 576
S_BETA = 640
S_ALPHA = 648
NS_NARROW = 768

LANES = 128
SUBLANES = 8
DSA_BLOCK = 128
DSA_QROWS = 256
DSA_KBLOCK = 256
DSA_SUM_ROWS = 16
DSA_BIAS_KINDS = DSA_KBLOCK // DSA_BLOCK + 2
IDX_PAD = 256
VMEM_LIMIT = 56 * 1024 * 1024
NEG_BIG = -1e30
INT_MIN = -(2 ** 31)

NN = (((1,), (0,)), ((), ()))
NT = (((1,), (1,)), ((), ()))
TN = (((0,), (0,)), ((), ()))


def _dg(a, b, dims=NN):
    return lax.dot_general(a, b, dims, preferred_element_type=F32)


def _split2(x):
    hi = x.astype(BF16)
    lo = (x - hi.astype(F32)).astype(BF16)
    return hi, lo


def _dot_b(a, b, dims=NN):
    return _dg(a.astype(BF16), b.astype(BF16), dims)


def _dot_hp(a, b, dims=NN):
    ah, al = _split2(a)
    bh, bl = _split2(b)
    return _dg(ah, bh, dims) + (_dg(al, bh, dims) + _dg(ah, bl, dims))


def _dot_hp_exact_rhs(a, b_bf16, dims=NN):
    a0 = a.astype(BF16)
    r = a - a0.astype(F32)
    a1 = r.astype(BF16)
    a2 = (r - a1.astype(F32)).astype(BF16)
    return _dg(a0, b_bf16, dims) + (_dg(a1, b_bf16, dims) + _dg(a2, b_bf16, dims))


def _sigmoid(x):
    return 1.0 / (1.0 + jnp.exp(-x))


def _silu(x):
    return x * _sigmoid(x)


def _softplus(x):
    return jnp.maximum(x, 0.0) + jnp.log(1.0 + jnp.exp(-jnp.abs(x)))


def _cparams(sem):
    return pltpu.CompilerParams(dimension_semantics=sem, vmem_limit_bytes=VMEM_LIMIT)


def _ada_kernel(c_ref, w_ref, b_ref, o_ref):
    cond = _silu(c_ref[...])
    o_ref[...] = _dot_hp(cond, w_ref[...]) + b_ref[...]


def _ada_modulation(c, w_ada, b_ada):
    depth, d, n = w_ada.shape
    bsz = c.shape[0]
    tn = 512
    return pl.pallas_call(
        _ada_kernel,
        grid=(depth, n // tn),
        in_specs=[
            pl.BlockSpec((bsz, d), lambda l, j: (0, 0)),
            pl.BlockSpec((None, d, tn), lambda l, j: (l, 0, j)),
            pl.BlockSpec((None, 1, tn), lambda l, j: (l, 0, j)),
        ],
        out_specs=pl.BlockSpec((None, bsz, tn), lambda l, j: (l, 0, j)),
        out_shape=jax.ShapeDtypeStruct((depth, bsz, n), F32),
        compiler_params=_cparams(("parallel", "parallel")),
    )(c, w_ada, b_ada.reshape(depth, 1, n))


def _norm_kernel(x_ref, g_ref, sc_ref, sh_ref, o_ref):
    x = x_ref[...]
    ms = jnp.mean(x * x, axis=-1, keepdims=True)
    y = x * lax.rsqrt(ms + EPS) * g_ref[...]
    o_ref[...] = (y * (1.0 + sc_ref[...]) + sh_ref[...]).astype(o_ref.dtype)


def _norm_modulate(x, g, mod):
    bsz, seq, d = x.shape
    tl = 512
    return pl.pallas_call(
        _norm_kernel,
        grid=(bsz, seq // tl),
        in_specs=[
            pl.BlockSpec((None, tl, d), lambda b, i: (b, i, 0)),
            pl.BlockSpec((1, d), lambda b, i: (0, 0)),
            pl.BlockSpec((None, 1, d), lambda b, i: (b, 0, 1)),
            pl.BlockSpec((None, 1, d), lambda b, i: (b, 0, 0)),
        ],
        out_specs=pl.BlockSpec((None, tl, d), lambda b, i: (b, i, 0)),
        out_shape=jax.ShapeDtypeStruct((bsz, seq, d), BF16),
        compiler_params=_cparams(("parallel", "parallel")),
    )(x, g, mod, mod)


def _mm_kernel(a_ref, b_ref, o_ref):
    o_ref[...] = _dg(a_ref[...], b_ref[...]).astype(o_ref.dtype)


def _matmul(a, w_all, layer, col0, n, out_dtype, tm, tn):
    m, k = a.shape
    c0 = col0 // tn
    return pl.pallas_call(
        _mm_kernel,
        grid=(m // tm, n // tn),
        in_specs=[
            pl.BlockSpec((tm, k), lambda i, j: (i, 0)),
            pl.BlockSpec((None, k, tn), lambda i, j: (layer, 0, c0 + j)),
        ],
        out_specs=pl.BlockSpec((tm, tn), lambda i, j: (i, j)),
        out_shape=jax.ShapeDtypeStruct((m, n), out_dtype),
        compiler_params=_cparams(("parallel", "parallel")),
    )(a, w_all)


def _weight_blocks():
    pieces = [(_OFF_QKV_A, 3 * GDN_WIDTH), (_OFF_Z_A, GDN_WIDTH), (_OFF_GLU, 2 * CONV_WIDTH),
              (_OFF_Z_B, CONV_WIDTH), (_OFF_Q_C, ATT_WIDTH), (_OFF_Z_C, ATT_WIDTH),
              (_OFF_GATE, N_BRANCH * D_MODEL), (_OFF_K_C, ATT_KV_WIDTH), (_OFF_V_C, ATT_KV_WIDTH),
              (_OFF_Q_I, IDX_HEADS * IDX_HEAD_DIM), (_OFF_K_I, LANES), (_OFF_BETA, LANES)]
    return [o + c for o, n in pieces for c in range(0, n, LANES)]


def _wprep_kernel(tab_ref, a_ref, b_ref, o_ref, *, shifts):
    code = tab_ref[1, pl.program_id(1)]
    lane = lax.broadcasted_iota(I32, a_ref.shape, 1)
    for c, sh in enumerate(shifts):
        @pl.when(code == c)
        def _(sh=sh):
            if sh == 0:
                o_ref[...] = a_ref[...].astype(o_ref.dtype)
            else:
                o_ref[...] = jnp.where(lane < LANES - sh, pltpu.roll(a_ref[...], LANES - sh, 1),
                                       pltpu.roll(b_ref[...], LANES - sh, 1)).astype(o_ref.dtype)


def _prepare_weights(w_in):
    depth, k, n_in = w_in.shape
    starts = _weight_blocks()
    shifts = sorted({s % LANES for s in starts})
    last = (n_in - 1) // LANES
    table = jnp.array([[s // LANES for s in starts], [shifts.index(s % LANES) for s in starts]], I32)
    nblk = len(starts)
    grid_spec = pltpu.PrefetchScalarGridSpec(
        num_scalar_prefetch=1,
        grid=(depth, nblk),
        in_specs=[
            pl.BlockSpec((None, k, LANES), lambda l, j, tab: (l, 0, tab[0, j])),
            pl.BlockSpec((None, k, LANES), lambda l, j, tab: (l, 0, jnp.minimum(tab[0, j] + 1, last))),
        ],
        out_specs=pl.BlockSpec((None, k, LANES), lambda l, j, tab: (l, 0, j)),
    )
    return pl.pallas_call(
        functools.partial(_wprep_kernel, shifts=tuple(shifts)),
        grid_spec=grid_spec,
        out_shape=jax.ShapeDtypeStruct((depth, k, nblk * LANES), BF16),
        compiler_params=_cparams(("parallel", "parallel")),
    )(table, w_in, w_in)


def _tri_inverse(mats, eye):
    n = mats[0].shape[0]
    levels = CHUNK.bit_length() - 1
    bs = [-a for a in mats]
    ss = [eye + b for b in bs]
    ps = [_dot_b(b, b) for b in bs]
    for _ in range(1, levels - 1):
        prods = [_dot_b(jnp.concatenate([s, p], axis=0), p) for s, p in zip(ss, ps)]
        ss = [s + pr[:n] for s, pr in zip(ss, prods)]
        ps = [pr[n:] for pr in prods]
    return [s + _dot_b(s, p) for s, p in zip(ss, ps)]


GDN_UNROLL = 8


def _gdn_kernel(hp_ref, q_ref, k_ref, v_ref, z_ref, cwq_ref, cwk_ref, cwv_ref, br_ref, ar_ref,
                gn_ref, o_ref, pad_scr, qn_scr, kn_scr, vn_scr, gc_scr, beta_scr,
                lhs_scr, cadd_scr, *, hb, seq):
    hd = GDN_HEAD_DIM
    nch = seq // CHUNK
    head0 = pl.program_id(1) * hb
    row_tile = min(seq, 256)
    unroll = math.gcd(GDN_UNROLL, nch)

    ii = lax.broadcasted_iota(I32, (CHUNK, CHUNK), 0)
    jj = lax.broadcasted_iota(I32, (CHUNK, CHUNK), 1)
    incl = ii >= jj
    strict = ii > jj
    eye = jnp.where(ii == jj, 1.0, 0.0).astype(F32)
    upper = jnp.where(ii <= jj, 1.0, 0.0).astype(BF16)

    def row_to_col(row):
        return jnp.sum(jnp.where(ii == jj, jnp.broadcast_to(row, (CHUNK, CHUNK)), 0.0),
                       axis=1, keepdims=True)

    for hh in range(hb):
        ls = slice(hh * hd, (hh + 1) * hd)
        for src, cw, dst, mode in ((q_ref, cwq_ref, qn_scr, "q"), (k_ref, cwk_ref, kn_scr, "k"),
                                   (v_ref, cwv_ref, vn_scr, "v")):
            pad_scr[0:8, :] = jnp.zeros((8, hd), F32)
            pad_scr[8:8 + seq, :] = src[:, ls].astype(F32)
            for rt in range(seq // row_tile):
                base = rt * row_tile
                acc = jnp.zeros((row_tile, hd), F32)
                for kk in range(GDN_CONV):
                    off = base + 8 - (GDN_CONV - 1) + kk
                    acc = acc + cw[kk:kk + 1, ls] * pad_scr[off:off + row_tile, :]
                y = _silu(acc)
                if mode != "v":
                    y = y * lax.rsqrt(jnp.sum(y * y, axis=-1, keepdims=True) + EPS)
                if mode == "q":
                    y = y * (hd ** -0.5)
                dst[hh, base:base + row_tile, :] = y
        a_log = hp_ref[head0 + hh, 0]
        dt_b = hp_ref[head0 + hh, 1]
        g2 = -jnp.exp(jnp.full((nch, CHUNK), a_log, F32)) * _softplus(ar_ref[hh] + dt_b)
        gc_scr[hh] = _dot_hp_exact_rhs(g2, upper)
        beta_scr[hh] = _sigmoid(br_ref[hh])

    i_hd = lax.broadcasted_iota(I32, (hd, hd), 0)
    j_hd = lax.broadcasted_iota(I32, (hd, hd), 1)
    eye_hd = jnp.where(i_hd == j_hd, 1.0, 0.0).astype(BF16)

    def load_chunk(n, hh):
        r0 = pl.multiple_of(n * CHUNK, CHUNK)
        return (qn_scr[hh, pl.ds(r0, CHUNK), :], kn_scr[hh, pl.ds(r0, CHUNK), :],
                vn_scr[hh, pl.ds(r0, CHUNK), :], gc_scr[hh, pl.ds(n, 1), :],
                beta_scr[hh, pl.ds(n, 1), :])

    def compute_chunks(loaded):
        qs, ks, vs, gc_rows, beta_rows = zip(*loaded)
        gc_cols = [row_to_col(r) for r in gc_rows]
        beta_cols = [row_to_col(r) for r in beta_rows]
        decays = [jnp.where(incl, jnp.exp(jnp.where(incl, c - r, 0.0)), 0.0)
                  for c, r in zip(gc_cols, gc_rows)]
        k16s = [k.astype(BF16) for k in ks]
        qk_kks = [_dg(jnp.concatenate([q.astype(BF16), k16], axis=0), k16, NT)
                  for q, k16 in zip(qs, k16s)]
        intras = [jnp.where(incl, m[:CHUNK] * d, 0.0).astype(BF16) for m, d in zip(qk_kks, decays)]
        t_invs = _tri_inverse([jnp.where(strict, b * m[CHUNK:] * d, 0.0)
                               for b, m, d in zip(beta_cols, qk_kks, decays)], eye)
        e_gcs = [jnp.exp(c) for c in gc_cols]
        uws = [_dot_b(t, jnp.concatenate([v * b, k * (b * e)], axis=1))
               for t, v, k, b, e in zip(t_invs, vs, ks, beta_cols, e_gcs)]
        kdts = [(k * jnp.exp(r[:, CHUNK - 1:CHUNK] - c)).T.astype(BF16)
                for k, r, c in zip(ks, gc_rows, gc_cols)]
        wu16s = [jnp.concatenate([uw[:, hd:], uw[:, :hd]], axis=1).astype(BF16) for uw in uws]
        kwcs = [_dg(kdt, wu) for kdt, wu in zip(kdts, wu16s)]
        iwus = [_dg(intra, wu) for intra, wu in zip(intras, wu16s)]
        lhs = [jnp.concatenate([kwc[:, :hd], q * e - iwu[:, :hd]], axis=0).astype(BF16)
               for kwc, q, e, iwu in zip(kwcs, qs, e_gcs, iwus)]
        return [(l, kwc[:, hd:], iwu[:, hd:]) for l, kwc, iwu in zip(lhs, kwcs, iwus)]

    def store_chunk(n, hh, lhs, c_add, o_add):
        lhs_scr[hh, n] = lhs
        cadd_scr[hh, n] = c_add
        vn_scr[hh, pl.ds(pl.multiple_of(n * CHUNK, CHUNK), CHUNK), :] = o_add

    def prepare_body(step, carry):
        jobs = [(step * unroll + cu, hh) for cu in range(unroll) for hh in range(hb)]
        results = compute_chunks([load_chunk(n, hh) for n, hh in jobs])
        for (n, hh), res in zip(jobs, results):
            store_chunk(n, hh, *res)
        return carry

    lax.fori_loop(0, nch // unroll, prepare_body, 0)

    def scan_body(n, states):
        r0 = pl.multiple_of(n * CHUNK, CHUNK)
        loaded = [(lhs_scr[hh, n], cadd_scr[hh, n], vn_scr[hh, pl.ds(r0, CHUNK), :],
                   gc_scr[hh, pl.ds(n, 1), CHUNK - 1:CHUNK]) for hh in range(hb)]
        lhss, c_adds, o_adds, g_lasts = zip(*loaded)
        prods = [_dg(l, s.astype(BF16)) for l, s in zip(lhss, states)]
        new_states = [s * jnp.exp(g) - p[:hd] + c for s, g, p, c in zip(states, g_lasts, prods, c_adds)]
        for hh in range(hb):
            qn_scr[hh, pl.ds(r0, CHUNK), :] = prods[hh][hd:] + o_adds[hh]
        return tuple(new_states)

    lax.fori_loop(0, nch, scan_body, tuple(jnp.zeros((hd, hd), F32) for _ in range(hb)))

    for hh in range(hb):
        ls = slice(hh * hd, (hh + 1) * hd)
        for rt in range(seq // row_tile):
            rs = slice(rt * row_tile, (rt + 1) * row_tile)
            o = qn_scr[hh, rs, :]
            on = o * lax.rsqrt(jnp.mean(o * o, axis=-1, keepdims=True) + EPS) * gn_ref[...]
            o_ref[rs, ls] = (on * _silu(z_ref[rs, ls].astype(F32))).astype(o_ref.dtype)


def _gdn_mixer(p3, beta_r, alpha_r, conv_w, head_params, gn, hb=2):
    bsz, seq, _ = p3.shape
    wb = hb * GDN_HEAD_DIM
    nhb = GDN_HEADS // hb
    nch = seq // CHUNK
    qb, kb_, vb, zb = (P_QKV_A // wb, (P_QKV_A + GDN_WIDTH) // wb, (P_QKV_A + 2 * GDN_WIDTH) // wb,
                       P_Z_A // wb)
    col = lambda off: pl.BlockSpec((None, seq, wb), lambda b, h, off=off: (b, 0, off + h))
    cwspec = lambda off: pl.BlockSpec((GDN_CONV, wb), lambda b, h, off=off: (0, off + h))
    rows = pl.BlockSpec((None, hb, nch, CHUNK), lambda b, h: (b, h, 0, 0))
    return pl.pallas_call(
        functools.partial(_gdn_kernel, hb=hb, seq=seq),
        grid=(bsz, nhb),
        in_specs=[
            pl.BlockSpec(memory_space=pltpu.SMEM),
            col(qb), col(kb_), col(vb), col(zb),
            cwspec(0), cwspec(nhb), cwspec(2 * nhb),
            rows, rows,
            pl.BlockSpec((1, GDN_HEAD_DIM), lambda b, h: (0, 0)),
        ],
        out_specs=pl.BlockSpec((None, seq, wb), lambda b, h: (b, 0, h)),
        out_shape=jax.ShapeDtypeStruct((bsz, seq, GDN_WIDTH), BF16),
        scratch_shapes=[
            pltpu.VMEM((seq + 8, GDN_HEAD_DIM), F32),
            pltpu.VMEM((hb, seq, GDN_HEAD_DIM), F32),
            pltpu.VMEM((hb, seq, GDN_HEAD_DIM), F32),
            pltpu.VMEM((hb, seq, GDN_HEAD_DIM), F32),
            pltpu.VMEM((hb, nch, CHUNK), F32),
            pltpu.VMEM((hb, nch, CHUNK), F32),
            pltpu.VMEM((hb, nch, GDN_HEAD_DIM + CHUNK, GDN_HEAD_DIM), BF16),
            pltpu.VMEM((hb, nch, GDN_HEAD_DIM, GDN_HEAD_DIM), F32),
        ],
        compiler_params=_cparams(("parallel", "parallel")),
    )(head_params, p3, p3, p3, p3, conv_w, conv_w, conv_w, beta_r, alpha_r, gn)


CONF_HALO = 32
CONF_ROWS = 32
CONF_COPY_ROWS = 64


def _conf_kernel(a_ref, b_ref, ap_ref, bp_ref, z_ref, w_ref, cb_ref, lg_ref, lb_ref, o_ref, glu_scr,
                 *, tl):
    i = pl.program_id(1)
    rows = CONF_HALO + tl
    for rt in range(tl // CONF_COPY_ROWS):
        rs = slice(rt * CONF_COPY_ROWS, (rt + 1) * CONF_COPY_ROWS)
        glu_scr[0, CONF_HALO + rs.start:CONF_HALO + rs.stop, :] = (
            a_ref[rs, :].astype(F32) * _sigmoid(b_ref[rs, :].astype(F32)))
    prev = ap_ref[...].astype(F32) * _sigmoid(bp_ref[...].astype(F32))
    glu_scr[0, 0:CONF_HALO, :] = jnp.where(i > 0, prev, 0.0)
    for s in range(1, SUBLANES):
        for r0 in range(0, rows - SUBLANES, CONF_COPY_ROWS):
            n = min(CONF_COPY_ROWS, rows - SUBLANES - r0)
            glu_scr[s, r0:r0 + n, :] = glu_scr[0, r0 + s:r0 + s + n, :]
    groups = CONF_ROWS // SUBLANES
    for rt in range(tl // CONF_ROWS):
        base = rt * CONF_ROWS
        acc = jnp.broadcast_to(cb_ref[...], (groups, SUBLANES, CONV_WIDTH))
        for kk in range(CONV_KERNEL):
            off = base + CONF_HALO - (CONV_KERNEL - 1) + kk
            s = off % SUBLANES
            win = glu_scr[s, off - s:off - s + CONF_ROWS, :].reshape(groups, SUBLANES, CONV_WIDTH)
            acc = acc + w_ref[kk][None] * win
        acc = acc.reshape(CONF_ROWS, CONV_WIDTH)
        mu = jnp.mean(acc, axis=-1, keepdims=True)
        cen = acc - mu
        var = jnp.mean(cen * cen, axis=-1, keepdims=True)
        y = cen * lax.rsqrt(var + EPS) * lg_ref[...] + lb_ref[...]
        z = z_ref[base:base + CONF_ROWS, :].astype(F32)
        o_ref[base:base + CONF_ROWS, :] = (_silu(y) * _silu(z)).astype(o_ref.dtype)


def _conformer_mixer(p3, conv_w, conv_b, ln_g, ln_b):
    bsz, seq, _ = p3.shape
    tl = 256
    cw = CONV_WIDTH
    ca, cb, cz = P_GLU // cw, P_GLU // cw + 1, P_Z_B // cw
    halo_per_tile = tl // CONF_HALO
    cur = lambda c: pl.BlockSpec((None, tl, cw), lambda b, i, c=c: (b, i, c))
    prev = lambda c: pl.BlockSpec(
        (None, CONF_HALO, cw), lambda b, i, c=c: (b, jnp.maximum(i * halo_per_tile - 1, 0), c))
    vec = pl.BlockSpec((1, cw), lambda b, i: (0, 0))
    return pl.pallas_call(
        functools.partial(_conf_kernel, tl=tl),
        grid=(bsz, seq // tl),
        in_specs=[cur(ca), cur(cb), prev(ca), prev(cb), cur(cz),
                  pl.BlockSpec((CONV_KERNEL, SUBLANES, cw), lambda b, i: (0, 0, 0)), vec, vec, vec],
        out_specs=pl.BlockSpec((None, tl, cw), lambda b, i: (b, i, 0)),
        out_shape=jax.ShapeDtypeStruct((bsz, seq, cw), BF16),
        scratch_shapes=[pltpu.VMEM((SUBLANES, CONF_HALO + tl, cw), F32)],
        compiler_params=_cparams(("parallel", "parallel")),
    )(p3, p3, p3, p3, p3, jnp.broadcast_to(conv_w[:, None, :], (CONV_KERNEL, SUBLANES, cw)), conv_b, ln_g, ln_b)


def _t5_bucket_int(rel):
    nb = REL_BUCKETS // 2
    max_exact = nb // 2
    n = jnp.abs(rel)
    large = jnp.full(rel.shape, max_exact, I32)
    for thr in (12, 16, 23, 32, 46, 64, 91):
        large = large + jnp.where(n >= thr, 1, 0)
    large = jnp.minimum(large, nb - 1)
    return jnp.where(rel > 0, nb, 0) + jnp.where(n < max_exact, n, large)


def _bias_kernel(rb_ref, o_ref):
    kpos = lax.broadcasted_iota(I32, (DSA_KBLOCK, DSA_BLOCK), 0)
    qpos = lax.broadcasted_iota(I32, (DSA_KBLOCK, DSA_BLOCK), 1)
    for kind in range(DSA_BIAS_KINDS):
        bucket = _t5_bucket_int(kpos - qpos + (kind - (DSA_BIAS_KINDS - 1)) * DSA_BLOCK)
        for h in range(ATT_HEADS):
            val = jnp.zeros((DSA_KBLOCK, DSA_BLOCK), F32)
            for b in range(REL_BUCKETS):
                val = jnp.where(bucket == b, rb_ref[b, h], val)
            o_ref[kind, h] = val


def _bias_tiles(rel_bias):
    return pl.pallas_call(
        _bias_kernel,
        in_specs=[pl.BlockSpec(memory_space=pltpu.SMEM)],
        out_shape=jax.ShapeDtypeStruct((DSA_BIAS_KINDS, ATT_HEADS, DSA_KBLOCK, DSA_BLOCK), F32),
    )(rel_bias)


def _for_block_pairs(n, fn, carry):
    def pair(p, c):
        return fn([2 * p, 2 * p + 1], c)

    c = lax.fori_loop(0, lax.shift_right_logical(n, 1), pair, carry)

    @pl.when((n & 1) == 1)
    def _last():
        fn([n - 1], c)


def _dsa_kernel(qi_ref, ki_ref, wt_ref, qc_ref, zc_ref, kc_ref, vc_ref, qg_ref, kg_ref, tab_ref,
                o_ref, kp_scr, kn_scr, vt_scr, qp_scr, key_scr, sel_scr, qs_scr, lg_scr, mx_scr,
                acc_scr, *, seq, topk):
    blk = DSA_BLOCK
    qrows = DSA_QROWS
    kblk = DSA_KBLOCK
    hd = ATT_HEAD_DIM
    i = pl.program_id(1)
    nkv = (i * qrows) // kblk + 1
    kf = float(topk)

    @pl.when(i == 0)
    def _prepare_keys():
        e_r = lax.broadcasted_iota(I32, (hd, hd), 0)
        e_c = lax.broadcasted_iota(I32, (hd, hd), 1)
        eye_hd = jnp.where(e_r == e_c, 1.0, 0.0).astype(BF16)
        low_k = lax.broadcasted_iota(I32, (kblk, LANES), 1) < IDX_HEAD_DIM
        for t in range(seq // kblk):
            rs = slice(t * kblk, (t + 1) * kblk)
            kt = ki_ref[rs, :]
            hi = kt.astype(BF16).astype(F32)
            lo = kt - hi
            kp_scr[rs, 0:LANES] = jnp.where(low_k, hi, pltpu.roll(hi, IDX_HEAD_DIM, 1)).astype(BF16)
            kp_scr[rs, LANES:2 * LANES] = jnp.where(low_k, lo, 0.0).astype(BF16)
            kc = kc_ref[rs, :].astype(F32)
            for g in range(ATT_KV_HEADS):
                ls = slice(g * hd, (g + 1) * hd)
                kg = kc[:, ls]
                kgn = kg * lax.rsqrt(jnp.mean(kg * kg, axis=-1, keepdims=True) + EPS) * kg_ref[...]
                kn_scr[rs, ls] = kgn.astype(BF16)
                vt_scr[t, g, 0:hd, :] = _dg(eye_hd, vc_ref[rs, ls], NT).astype(BF16)
                vt_scr[t, g, hd:, :] = jnp.ones((DSA_SUM_ROWS, kblk), BF16)

    low_q = lax.broadcasted_iota(I32, (qrows, LANES), 1) < IDX_HEAD_DIM
    for c in range(IDX_HEADS * IDX_HEAD_DIM // LANES):
        qt = qi_ref[:, c * LANES:(c + 1) * LANES]
        hi = qt.astype(BF16).astype(F32)
        lo = qt - hi
        hi_sw = pltpu.roll(hi, IDX_HEAD_DIM, 1)
        lo_sw = pltpu.roll(lo, IDX_HEAD_DIM, 1)
        for h, first, second in ((2 * c, jnp.where(low_q, hi, lo_sw), jnp.where(low_q, hi, 0.0)),
                                 (2 * c + 1, jnp.where(low_q, hi_sw, lo), jnp.where(low_q, hi_sw, 0.0))):
            qp_scr[h * qrows:(h + 1) * qrows, 0:LANES] = first.astype(BF16)
            qp_scr[h * qrows:(h + 1) * qrows, LANES:2 * LANES] = second.astype(BF16)
    wsc = wt_ref[...] * ((IDX_HEAD_DIM ** -0.5) * (IDX_HEADS ** -0.5))

    kpos = lax.broadcasted_iota(I32, (kblk, qrows), 0)
    qpos = lax.broadcasted_iota(I32, (kblk, qrows), 1)
    chunk_gap = (kpos // CHUNK) - (qpos // CHUNK)
    q_chunk0 = i * (qrows // CHUNK)

    def fold_rows(x):
        return x.reshape(kblk // SUBLANES, SUBLANES, x.shape[-1])

    def score_blocks(js, carry):
        qp = qp_scr[...]
        s_alls = [_dg(kp_scr[pl.ds(pl.multiple_of(j * kblk, kblk), kblk), :], qp, NT) for j in js]
        for j, s_all in zip(js, s_alls):
            sc = jnp.zeros((kblk, qrows), F32)
            for h in range(IDX_HEADS):
                sc = sc + wsc[h:h + 1, :] * jnp.maximum(s_all[:, h * qrows:(h + 1) * qrows], 0.0)
            sc = sc + 0.0
            bits = pltpu.bitcast(sc, I32)
            key = jnp.where(bits < 0, bits ^ jnp.int32(0x7FFFFFFF), bits)
            slack = q_chunk0 - j * (kblk // CHUNK)
            key_scr[j] = jnp.where(chunk_gap <= slack, key, jnp.int32(INT_MIN))
        return carry

    _for_block_pairs(nkv, score_blocks, 0)

    def count_where(pred, ref):
        def block(j):
            return jnp.sum(fold_rows(jnp.where(pred(key_scr[j], ref), 1.0, 0.0)), axis=0)

        acc = lax.fori_loop(0, lax.shift_right_logical(nkv, 1),
                            lambda p, a: a + (block(2 * p) + block(2 * p + 1)),
                            jnp.zeros((SUBLANES, qrows), F32))
        acc = lax.cond((nkv & 1) == 1, lambda a: a + block(nkv - 1), lambda a: a, acc)
        return jnp.sum(acc, axis=0, keepdims=True)

    ge = lambda kv, ref: kv >= ref
    cnt0 = count_where(ge, jnp.zeros((1, qrows), I32))
    cur0 = jnp.where(cnt0 >= kf, jnp.int32(0), jnp.int32(INT_MIN))

    def bit_body(t, cur):
        cand = cur | lax.shift_left(jnp.int32(1), jnp.int32(30) - t)
        return jnp.where(count_where(ge, cand) >= kf, cand, cur)

    thr = lax.fori_loop(0, 31, bit_body, cur0)
    need = kf - count_where(lambda kv, ref: kv > ref, thr)

    ku = lax.broadcasted_iota(I32, (kblk, kblk), 0)
    kv_ = lax.broadcasted_iota(I32, (kblk, kblk), 1)
    lower = jnp.where(ku >= kv_, 1.0, 0.0).astype(BF16)

    def sel_blocks(js, run):
        keys = [key_scr[j] for j in js]
        eqs = [key == thr for key in keys]
        pres = [_dg(lower, jnp.where(eq, 1.0, 0.0).astype(BF16)) for eq in eqs]
        for j, key, eq, pre in zip(js, keys, eqs, pres):
            take_tie = jnp.where(eq & (run + pre <= need), 0.0, NEG_BIG)
            take = jnp.where(key > thr, 0.0, take_tie)
            sel_scr[j] = jnp.where(key == jnp.int32(INT_MIN), NEG_BIG, take)
            run = run + pre[kblk - 1:kblk, :]
        return run

    _for_block_pairs(nkv, sel_blocks, jnp.zeros((1, qrows), F32))

    for a in range(qrows // blk):
        _dsa_attend(a, i * (qrows // blk) + a, nkv, qc_ref, zc_ref, qg_ref, tab_ref, o_ref,
                    kn_scr, vt_scr, sel_scr, qs_scr, lg_scr, mx_scr, acc_scr)


def _dsa_attend(a, qblock, nkv, qc_ref, zc_ref, qg_ref, tab_ref, o_ref,
                kn_scr, vt_scr, sel_scr, qs_scr, lg_scr, mx_scr, acc_scr):
    blk = DSA_BLOCK
    kblk = DSA_KBLOCK
    hd = ATT_HEAD_DIM
    cols4 = ATT_GROUP * blk
    rq = slice(a * blk, (a + 1) * blk)
    qc = qc_ref[rq, :].astype(F32)
    for h in range(ATT_HEADS):
        qh = qc[:, h * hd:(h + 1) * hd]
        qn = qh * lax.rsqrt(jnp.mean(qh * qh, axis=-1, keepdims=True) + EPS) * qg_ref[...]
        g, hh = divmod(h, ATT_GROUP)
        qs_scr[g, hh * blk:(hh + 1) * blk, :] = (qn * (hd ** -0.5)).astype(BF16)
    mx_scr[...] = jnp.full(mx_scr.shape, NEG_BIG, F32)

    def fold(x):
        return x.reshape(kblk // SUBLANES, SUBLANES, x.shape[-1])

    groups = range(ATT_KV_HEADS)

    def logit_blocks(js, carry):
        jg = [(n, g) for n in range(len(js)) for g in groups]
        kks = [kn_scr[pl.ds(pl.multiple_of(j * kblk, kblk), kblk), :] for j in js]
        raw = [_dg(kks[n][:, g * hd:(g + 1) * hd], qs_scr[g], NT) for n, g in jg]
        negs = [jnp.concatenate([sel_scr[j, :, rq]] * ATT_GROUP, axis=1) for j in js]
        kinds = [jnp.clip(j * (kblk // blk) - qblock + (DSA_BIAS_KINDS - 1), 0, DSA_BIAS_KINDS - 1)
                 for j in js]
        tops = [mx_scr[g] for g in groups]
        for (n, g), r in zip(jg, raw):
            bias = jnp.concatenate([tab_ref[kinds[n], g * ATT_GROUP + hh] for hh in range(ATT_GROUP)],
                                   axis=1)
            lg = r + bias + negs[n]
            lg_scr[js[n], g] = lg
            tops[g] = jnp.maximum(tops[g], jnp.max(fold(lg), axis=0))
        for g in groups:
            mx_scr[g] = tops[g]
        return carry

    _for_block_pairs(nkv, logit_blocks, 0)
    for g in groups:
        mx_scr[g] = jnp.broadcast_to(jnp.max(mx_scr[g], axis=0, keepdims=True), (SUBLANES, cols4))
    acc_scr[...] = jnp.zeros(acc_scr.shape, F32)

    def pv_blocks(js, carry):
        jg = [(j, g) for j in js for g in groups]
        ps = [jnp.exp(lg_scr[j, g] - mx_scr[g, 0:1, :]).astype(BF16) for j, g in jg]
        pvs = [_dg(vt_scr[j, g], p) for (j, g), p in zip(jg, ps)]
        for g in groups:
            acc_scr[g] += sum(pv for (_, gg), pv in zip(jg, pvs) if gg == g)
        return carry

    _for_block_pairs(nkv, pv_blocks, 0)

    for h in range(ATT_HEADS):
        g, hh = divmod(h, ATT_GROUP)
        cs = slice(hh * blk, (hh + 1) * blk)
        ls = slice(h * hd, (h + 1) * hd)
        o_t = acc_scr[g, 0:hd, cs] / acc_scr[g, hd:hd + 1, cs]
        z = zc_ref[rq, ls].astype(F32)
        o_ref[rq, ls] = (o_t.T * _silu(z)).astype(o_ref.dtype)


def _dsa_mixer(p3, s3, w_t, q_gain, k_gain, bias_tab):
    bsz, seq, _ = p3.shape
    blk = DSA_BLOCK
    kblk = DSA_KBLOCK
    qrows = DSA_QROWS
    nkb = seq // kblk
    rows4 = ATT_GROUP * blk
    topk = min(TOPK_MAX, seq // 4)
    return pl.pallas_call(
        functools.partial(_dsa_kernel, seq=seq, topk=topk),
        grid=(bsz, seq // qrows),
        in_specs=[
            pl.BlockSpec((None, qrows, IDX_HEADS * IDX_HEAD_DIM), lambda b, i: (b, i, S_Q_I // (IDX_HEADS * IDX_HEAD_DIM))),
            pl.BlockSpec((None, seq, LANES), lambda b, i: (b, 0, S_K_I // LANES)),
            pl.BlockSpec((None, IDX_HEADS, qrows), lambda b, i: (b, 0, i)),
            pl.BlockSpec((None, qrows, ATT_WIDTH), lambda b, i: (b, i, P_Q_C // ATT_WIDTH)),
            pl.BlockSpec((None, qrows, ATT_WIDTH), lambda b, i: (b, i, P_Z_C // ATT_WIDTH)),
            pl.BlockSpec((None, seq, ATT_KV_WIDTH), lambda b, i: (b, 0, P_K_C // ATT_KV_WIDTH)),
            pl.BlockSpec((None, seq, ATT_KV_WIDTH), lambda b, i: (b, 0, P_V_C // ATT_KV_WIDTH)),
            pl.BlockSpec((1, ATT_HEAD_DIM), lambda b, i: (0, 0)),
            pl.BlockSpec((1, ATT_HEAD_DIM), lambda b, i: (0, 0)),
            pl.BlockSpec((DSA_BIAS_KINDS, ATT_HEADS, kblk, blk), lambda b, i: (0, 0, 0, 0)),
        ],
        out_specs=pl.BlockSpec((None, qrows, ATT_WIDTH), lambda b, i: (b, i, 0)),
        out_shape=jax.ShapeDtypeStruct((bsz, seq, ATT_WIDTH), BF16),
        scratch_shapes=[
            pltpu.VMEM((seq, IDX_PAD), BF16),
            pltpu.VMEM((seq, ATT_KV_WIDTH), BF16),
            pltpu.VMEM((nkb, ATT_KV_HEADS, ATT_HEAD_DIM + DSA_SUM_ROWS, kblk), BF16),
            pltpu.VMEM((IDX_HEADS * qrows, IDX_PAD), BF16),
            pltpu.VMEM((nkb, kblk, qrows), I32),
            pltpu.VMEM((nkb, kblk, qrows), F32),
            pltpu.VMEM((ATT_KV_HEADS, rows4, ATT_HEAD_DIM), BF16),
            pltpu.VMEM((nkb, ATT_KV_HEADS, kblk, rows4), F32),
            pltpu.VMEM((ATT_KV_HEADS, SUBLANES, rows4), F32),
            pltpu.VMEM((ATT_KV_HEADS, ATT_HEAD_DIM + DSA_SUM_ROWS, rows4), F32),
        ],
        compiler_params=_cparams(("arbitrary", "arbitrary")),
    )(s3, s3, w_t, p3, p3, p3, p3, q_gain, k_gain, bias_tab)


def _merge_kernel(ya_ref, yb_ref, yc_ref, wa_ref, wb_ref, wc_ref, ga_ref, gb_ref, gc_ref,
                  ba_ref, bb_ref, bc_ref, o_ref):
    acc = _sigmoid(ga_ref[...].astype(F32) + ba_ref[...]) * _dg(ya_ref[...], wa_ref[...])
    acc = acc + _sigmoid(gb_ref[...].astype(F32) + bb_ref[...]) * _dg(yb_ref[...], wb_ref[...])
    acc = acc + _sigmoid(gc_ref[...].astype(F32) + bc_ref[...]) * _dg(yc_ref[...], wc_ref[...])
    o_ref[...] = acc.astype(o_ref.dtype)


def _merge(ya, yb, yc, wa, wb, wc, p2, b_gate):
    t = ya.shape[0]
    d = D_MODEL
    tm, tn = min(t, 1024), 512
    nj = d // tn
    g0 = P_GATE // tn
    yspec = pl.BlockSpec((tm, ya.shape[1]), lambda i, j: (i, 0))
    wspec = pl.BlockSpec((wa.shape[0], tn), lambda i, j: (0, j))
    gspec = lambda br: pl.BlockSpec((tm, tn), lambda i, j, br=br: (i, g0 + br * nj + j))
    bspec = lambda br: pl.BlockSpec((1, tn), lambda i, j, br=br: (0, br * nj + j))
    return pl.pallas_call(
        _merge_kernel,
        grid=(t // tm, nj),
        in_specs=[yspec, yspec, yspec, wspec, wspec, wspec, gspec(0), gspec(1), gspec(2),
                  bspec(0), bspec(1), bspec(2)],
        out_specs=pl.BlockSpec((tm, tn), lambda i, j: (i, j)),
        out_shape=jax.ShapeDtypeStruct((t, d), BF16),
        compiler_params=_cparams(("parallel", "parallel")),
    )(ya, yb, yc, wa, wb, wc, p2, p2, p2, b_gate, b_gate, b_gate)


def _out_kernel(m_ref, w_ref, x_ref, g_ref, o_ref):
    o_ref[...] = x_ref[...] + g_ref[...] * _dg(m_ref[...], w_ref[...])


def _out_proj(merged, w_out, x2, mod, seq):
    t, d = x2.shape
    tm, tn = min(seq, 2048), 512
    per_batch = seq // tm
    nj = d // tn
    return pl.pallas_call(
        _out_kernel,
        grid=(t // tm, nj),
        in_specs=[
            pl.BlockSpec((tm, d), lambda i, j: (i, 0)),
            pl.BlockSpec((d, tn), lambda i, j: (0, j)),
            pl.BlockSpec((tm, tn), lambda i, j: (i, j)),
            pl.BlockSpec((None, 1, tn), lambda i, j: (i // per_batch, 0, 2 * nj + j)),
        ],
        out_specs=pl.BlockSpec((tm, tn), lambda i, j: (i, j)),
        out_shape=jax.ShapeDtypeStruct((t, d), F32),
        compiler_params=_cparams(("parallel", "parallel")),
    )(merged, w_out, x2, mod)


def kernel(x, c, rel_bias, w_ada, b_ada, norm_g, w_in, b_gate, conv_a_w, a_log, dt_bias, gdn_norm_g,
           w_proj_a, conv_b_w, conv_b_b, ln_b_g, ln_b_b, w_proj_b, q_norm_g, k_norm_g, w_proj_c, w_out):
    bsz, seq, d = x.shape
    t = bsz * seq
    nch = seq // CHUNK
    mod_all = _ada_modulation(c, w_ada, b_ada)
    bias_tab = _bias_tiles(rel_bias)
    w_all = _prepare_weights(w_in)
    for layer in range(DEPTH):
        mod = mod_all[layer].reshape(bsz, 1, 3 * d)
        h = _norm_modulate(x, norm_g[layer].reshape(1, d), mod).reshape(t, d)
        p2 = _matmul(h, w_all, layer, 0, NP_WIDE, BF16, min(t, 2048), 512)
        s2 = _matmul(h, w_all, layer, NP_WIDE, NS_NARROW, F32, min(t, 1024), 256)
        p3 = p2.reshape(bsz, seq, NP_WIDE)
        s3 = s2.reshape(bsz, seq, NS_NARROW)

        def head_rows(off):
            r = s3[:, :, off:off + GDN_HEADS]
            return jnp.swapaxes(r, 1, 2).reshape(bsz, GDN_HEADS, nch, CHUNK)

        head_params = jnp.stack([a_log[layer], dt_bias[layer]], axis=1)
        y_a = _gdn_mixer(p3, head_rows(S_BETA), head_rows(S_ALPHA), conv_a_w[layer], head_params,
                         gdn_norm_g[layer].reshape(1, GDN_HEAD_DIM))
        y_b = _conformer_mixer(p3, conv_b_w[layer], conv_b_b[layer].reshape(1, CONV_WIDTH),
                               ln_b_g[layer].reshape(1, CONV_WIDTH), ln_b_b[layer].reshape(1, CONV_WIDTH))
        y_c = _dsa_mixer(p3, s3, jnp.swapaxes(s3[:, :, S_W_I:S_W_I + IDX_HEADS], 1, 2),
                         q_norm_g[layer].reshape(1, ATT_HEAD_DIM), k_norm_g[layer].reshape(1, ATT_HEAD_DIM),
                         bias_tab)
        merged = _merge(y_a.reshape(t, GDN_WIDTH), y_b.reshape(t, CONV_WIDTH), y_c.reshape(t, ATT_WIDTH),
                        w_proj_a[layer].astype(BF16), w_proj_b[layer].astype(BF16),
                        w_proj_c[layer].astype(BF16), p2, b_gate[layer].reshape(1, N_BRANCH * d))
        x = _out_proj(merged, w_out[layer].astype(BF16), x.reshape(t, d), mod, seq).reshape(bsz, seq, d)
    return x
```

```python
import functools
import math

import jax
import jax.numpy as jnp
from jax import lax
from jax.experimental import pallas as pl
from jax.experimental.pallas import tpu as pltpu

F32 = jnp.float32
BF16 = jnp.bfloat16
I32 = jnp.int32

D_MODEL = 2048
DEPTH = 4
CHUNK = 64
GDN_HEADS = 8
GDN_HEAD_DIM = 128
GDN_WIDTH = GDN_HEADS * GDN_HEAD_DIM
GDN_CONV = 4
CONV_WIDTH = 1024
CONV_KERNEL = 31
ATT_HEADS = 8
ATT_KV_HEADS = 2
ATT_HEAD_DIM = 128
ATT_WIDTH = ATT_HEADS * ATT_HEAD_DIM
ATT_KV_WIDTH = ATT_KV_HEADS * ATT_HEAD_DIM
ATT_GROUP = ATT_HEADS // ATT_KV_HEADS
IDX_HEADS = 8
IDX_HEAD_DIM = 64
TOPK_MAX = 256
REL_BUCKETS = 32
N_BRANCH = 3
EPS = 1e-6

_OFF_QKV_A = 0
_OFF_Z_A = _OFF_QKV_A + 3 * GDN_WIDTH
_OFF_BETA = _OFF_Z_A + GDN_WIDTH
_OFF_ALPHA = _OFF_BETA + GDN_HEADS
_OFF_GLU = _OFF_ALPHA + GDN_HEADS
_OFF_Z_B = _OFF_GLU + 2 * CONV_WIDTH
_OFF_Q_C = _OFF_Z_B + CONV_WIDTH
_OFF_K_C = _OFF_Q_C + ATT_WIDTH
_OFF_V_C = _OFF_K_C + ATT_KV_WIDTH
_OFF_Z_C = _OFF_V_C + ATT_KV_WIDTH
_OFF_Q_I = _OFF_Z_C + ATT_WIDTH
_OFF_K_I = _OFF_Q_I + IDX_HEADS * IDX_HEAD_DIM
_OFF_W_I = _OFF_K_I + IDX_HEAD_DIM
_OFF_GATE = _OFF_W_I + IDX_HEADS
N_IN = _OFF_GATE + N_BRANCH * D_MODEL

P_QKV_A = 0
P_Z_A = 3072
P_GLU = 4096
P_Z_B = 6144
P_Q_C = 7168
P_Z_C = 8192
P_GATE = 9216
P_K_C = 15360
P_V_C = 15616
NP_WIDE = 15872
S_Q_I = 0
S_K_I = 512
S_W_I = 576
S_BETA = 584
S_ALPHA = 592
NS_NARROW = 640

LANES = 128
SUBLANES = 8
DSA_BLOCK = 128
DSA_QROWS = 256
DSA_KBLOCK = 256
DSA_SUM_ROWS = 16
DSA_BIAS_KINDS = DSA_KBLOCK // DSA_BLOCK + 2
IDX_PAD = 256
VMEM_LIMIT = 56 * 1024 * 1024
NEG_BIG = -1e30
INT_MIN = -(2 ** 31)

NN = (((1,), (0,)), ((), ()))
NT = (((1,), (1,)), ((), ()))
TN = (((0,), (0,)), ((), ()))


def _dg(a, b, dims=NN):
    return lax.dot_general(a, b, dims, preferred_element_type=F32)


def _split2(x):
    hi = x.astype(BF16)
    lo = (x - hi.astype(F32)).astype(BF16)
    return hi, lo


def _dot_b(a, b, dims=NN):
    return _dg(a.astype(BF16), b.astype(BF16), dims)


def _dot_hp(a, b, dims=NN):
    ah, al = _split2(a)
    bh, bl = _split2(b)
    return _dg(ah, bh, dims) + (_dg(al, bh, dims) + _dg(ah, bl, dims))


def _dot_hp_exact_rhs(a, b_bf16, dims=NN):
    a0 = a.astype(BF16)
    r = a - a0.astype(F32)
    a1 = r.astype(BF16)
    a2 = (r - a1.astype(F32)).astype(BF16)
    return _dg(a0, b_bf16, dims) + (_dg(a1, b_bf16, dims) + _dg(a2, b_bf16, dims))


def _sigmoid(x):
    return 1.0 / (1.0 + jnp.exp(-x))


def _silu(x):
    return x * _sigmoid(x)


def _softplus(x):
    return jnp.maximum(x, 0.0) + jnp.log(1.0 + jnp.exp(-jnp.abs(x)))


def _cparams(sem):
    return pltpu.CompilerParams(dimension_semantics=sem, vmem_limit_bytes=VMEM_LIMIT)


def _ada_kernel(c_ref, w_ref, b_ref, o_ref):
    cond = _silu(c_ref[...])
    o_ref[...] = _dot_hp(cond, w_ref[...]) + b_ref[...]


def _ada_modulation(c, w_ada, b_ada):
    depth, d, n = w_ada.shape
    bsz = c.shape[0]
    tn = 512
    return pl.pallas_call(
        _ada_kernel,
        grid=(depth, n // tn),
        in_specs=[
            pl.BlockSpec((bsz, d), lambda l, j: (0, 0)),
            pl.BlockSpec((None, d, tn), lambda l, j: (l, 0, j)),
            pl.BlockSpec((None, 1, tn), lambda l, j: (l, 0, j)),
        ],
        out_specs=pl.BlockSpec((None, bsz, tn), lambda l, j: (l, 0, j)),
        out_shape=jax.ShapeDtypeStruct((depth, bsz, n), F32),
        compiler_params=_cparams(("parallel", "parallel")),
    )(c, w_ada, b_ada.reshape(depth, 1, n))


def _norm_kernel(x_ref, g_ref, sc_ref, sh_ref, o_ref):
    x = x_ref[...]
    ms = jnp.mean(x * x, axis=-1, keepdims=True)
    y = x * lax.rsqrt(ms + EPS) * g_ref[...]
    o_ref[...] = (y * (1.0 + sc_ref[...]) + sh_ref[...]).astype(o_ref.dtype)


def _norm_modulate(x, g, mod):
    bsz, seq, d = x.shape
    tl = 512
    return pl.pallas_call(
        _norm_kernel,
        grid=(bsz, seq // tl),
        in_specs=[
            pl.BlockSpec((None, tl, d), lambda b, i: (b, i, 0)),
            pl.BlockSpec((1, d), lambda b, i: (0, 0)),
            pl.BlockSpec((None, 1, d), lambda b, i: (b, 0, 1)),
            pl.BlockSpec((None, 1, d), lambda b, i: (b, 0, 0)),
        ],
        out_specs=pl.BlockSpec((None, tl, d), lambda b, i: (b, i, 0)),
        out_shape=jax.ShapeDtypeStruct((bsz, seq, d), BF16),
        compiler_params=_cparams(("parallel", "parallel")),
    )(x, g, mod, mod)


def _mm_kernel(a_ref, b_ref, o_ref):
    o_ref[...] = _dg(a_ref[...], b_ref[...]).astype(o_ref.dtype)


def _matmul(a, b, out_dtype, tm, tn):
    m, k = a.shape
    n = b.shape[1]
    return pl.pallas_call(
        _mm_kernel,
        grid=(m // tm, n // tn),
        in_specs=[
            pl.BlockSpec((tm, k), lambda i, j: (i, 0)),
            pl.BlockSpec((k, tn), lambda i, j: (0, j)),
        ],
        out_specs=pl.BlockSpec((tm, tn), lambda i, j: (i, j)),
        out_shape=jax.ShapeDtypeStruct((m, n), out_dtype),
        compiler_params=_cparams(("parallel", "parallel")),
    )(a, b)


def _tri_inverse(mats, eye):
    n = mats[0].shape[0]
    levels = CHUNK.bit_length() - 1
    bs = [-a for a in mats]
    ss = [eye + b for b in bs]
    ps = [_dot_b(b, b) for b in bs]
    for _ in range(1, levels - 1):
        prods = [_dot_b(jnp.concatenate([s, p], axis=0), p) for s, p in zip(ss, ps)]
        ss = [s + pr[:n] for s, pr in zip(ss, prods)]
        ps = [pr[n:] for pr in prods]
    return [s + _dot_b(s, p) for s, p in zip(ss, ps)]


GDN_UNROLL = 8


def _gdn_kernel(hp_ref, q_ref, k_ref, v_ref, z_ref, cwq_ref, cwk_ref, cwv_ref, br_ref, ar_ref,
                gn_ref, o_ref, pad_scr, qn_scr, kn_scr, vn_scr, gc_scr, beta_scr,
                lhs_scr, cadd_scr, *, hb, seq):
    hd = GDN_HEAD_DIM
    nch = seq // CHUNK
    head0 = pl.program_id(1) * hb
    row_tile = min(seq, 256)
    unroll = math.gcd(GDN_UNROLL, nch)

    ii = lax.broadcasted_iota(I32, (CHUNK, CHUNK), 0)
    jj = lax.broadcasted_iota(I32, (CHUNK, CHUNK), 1)
    incl = ii >= jj
    strict = ii > jj
    eye = jnp.where(ii == jj, 1.0, 0.0).astype(F32)
    upper = jnp.where(ii <= jj, 1.0, 0.0).astype(BF16)

    def row_to_col(row):
        return jnp.sum(jnp.where(ii == jj, jnp.broadcast_to(row, (CHUNK, CHUNK)), 0.0),
                       axis=1, keepdims=True)

    for hh in range(hb):
        ls = slice(hh * hd, (hh + 1) * hd)
        for src, cw, dst, mode in ((q_ref, cwq_ref, qn_scr, "q"), (k_ref, cwk_ref, kn_scr, "k"),
                                   (v_ref, cwv_ref, vn_scr, "v")):
            pad_scr[0:8, :] = jnp.zeros((8, hd), F32)
            pad_scr[8:8 + seq, :] = src[:, ls].astype(F32)
            for rt in range(seq // row_tile):
                base = rt * row_tile
                acc = jnp.zeros((row_tile, hd), F32)
                for kk in range(GDN_CONV):
                    off = base + 8 - (GDN_CONV - 1) + kk
                    acc = acc + cw[kk:kk + 1, ls] * pad_scr[off:off + row_tile, :]
                y = _silu(acc)
                if mode != "v":
                    y = y * lax.rsqrt(jnp.sum(y * y, axis=-1, keepdims=True) + EPS)
                if mode == "q":
                    y = y * (hd ** -0.5)
                dst[hh, base:base + row_tile, :] = y
        a_log = hp_ref[head0 + hh, 0]
        dt_b = hp_ref[head0 + hh, 1]
        g2 = -jnp.exp(jnp.full((nch, CHUNK), a_log, F32)) * _softplus(ar_ref[hh] + dt_b)
        gc_scr[hh] = _dot_hp_exact_rhs(g2, upper)
        beta_scr[hh] = _sigmoid(br_ref[hh])

    i_hd = lax.broadcasted_iota(I32, (hd, hd), 0)
    j_hd = lax.broadcasted_iota(I32, (hd, hd), 1)
    eye_hd = jnp.where(i_hd == j_hd, 1.0, 0.0).astype(BF16)

    def load_chunk(n, hh):
        r0 = pl.multiple_of(n * CHUNK, CHUNK)
        return (qn_scr[hh, pl.ds(r0, CHUNK), :], kn_scr[hh, pl.ds(r0, CHUNK), :],
                vn_scr[hh, pl.ds(r0, CHUNK), :], gc_scr[hh, pl.ds(n, 1), :],
                beta_scr[hh, pl.ds(n, 1), :])

    def compute_chunks(loaded):
        qs, ks, vs, gc_rows, beta_rows = zip(*loaded)
        gc_cols = [row_to_col(r) for r in gc_rows]
        beta_cols = [row_to_col(r) for r in beta_rows]
        decays = [jnp.where(incl, jnp.exp(jnp.where(incl, c - r, 0.0)), 0.0)
                  for c, r in zip(gc_cols, gc_rows)]
        k16s = [k.astype(BF16) for k in ks]
        qk_kks = [_dg(jnp.concatenate([q.astype(BF16), k16], axis=0), k16, NT)
                  for q, k16 in zip(qs, k16s)]
        intras = [jnp.where(incl, m[:CHUNK] * d, 0.0).astype(BF16) for m, d in zip(qk_kks, decays)]
        t_invs = _tri_inverse([jnp.where(strict, b * m[CHUNK:] * d, 0.0)
                               for b, m, d in zip(beta_cols, qk_kks, decays)], eye)
        e_gcs = [jnp.exp(c) for c in gc_cols]
        uws = [_dot_b(t, jnp.concatenate([v * b, k * (b * e)], axis=1))
               for t, v, k, b, e in zip(t_invs, vs, ks, beta_cols, e_gcs)]
        kdts = [(k * jnp.exp(r[:, CHUNK - 1:CHUNK] - c)).T.astype(BF16)
                for k, r, c in zip(ks, gc_rows, gc_cols)]
        wu16s = [jnp.concatenate([uw[:, hd:], uw[:, :hd]], axis=1).astype(BF16) for uw in uws]
        kwcs = [_dg(kdt, wu) for kdt, wu in zip(kdts, wu16s)]
        iwus = [_dg(intra, wu) for intra, wu in zip(intras, wu16s)]
        lhs = [jnp.concatenate([kwc[:, :hd], q * e - iwu[:, :hd]], axis=0).astype(BF16)
               for kwc, q, e, iwu in zip(kwcs, qs, e_gcs, iwus)]
        return [(l, kwc[:, hd:], iwu[:, hd:]) for l, kwc, iwu in zip(lhs, kwcs, iwus)]

    def store_chunk(n, hh, lhs, c_add, o_add):
        lhs_scr[hh, n] = lhs
        cadd_scr[hh, n] = c_add
        vn_scr[hh, pl.ds(pl.multiple_of(n * CHUNK, CHUNK), CHUNK), :] = o_add

    def prepare_body(step, carry):
        jobs = [(step * unroll + cu, hh) for cu in range(unroll) for hh in range(hb)]
        results = compute_chunks([load_chunk(n, hh) for n, hh in jobs])
        for (n, hh), res in zip(jobs, results):
            store_chunk(n, hh, *res)
        return carry

    lax.fori_loop(0, nch // unroll, prepare_body, 0)

    def scan_body(n, states):
        r0 = pl.multiple_of(n * CHUNK, CHUNK)
        loaded = [(lhs_scr[hh, n], cadd_scr[hh, n], vn_scr[hh, pl.ds(r0, CHUNK), :],
                   gc_scr[hh, pl.ds(n, 1), CHUNK - 1:CHUNK]) for hh in range(hb)]
        lhss, c_adds, o_adds, g_lasts = zip(*loaded)
        prods = [_dg(l, s.astype(BF16)) for l, s in zip(lhss, states)]
        new_states = [s * jnp.exp(g) - p[:hd] + c for s, g, p, c in zip(states, g_lasts, prods, c_adds)]
        for hh in range(hb):
            qn_scr[hh, pl.ds(r0, CHUNK), :] = prods[hh][hd:] + o_adds[hh]
        return tuple(new_states)

    lax.fori_loop(0, nch, scan_body, tuple(jnp.zeros((hd, hd), F32) for _ in range(hb)))

    for hh in range(hb):
        ls = slice(hh * hd, (hh + 1) * hd)
        for rt in range(seq // row_tile):
            rs = slice(rt * row_tile, (rt + 1) * row_tile)
            o = qn_scr[hh, rs, :]
            on = o * lax.rsqrt(jnp.mean(o * o, axis=-1, keepdims=True) + EPS) * gn_ref[...]
            o_ref[rs, ls] = (on * _silu(z_ref[rs, ls].astype(F32))).astype(o_ref.dtype)


def _gdn_mixer(p3, beta_r, alpha_r, conv_w, head_params, gn, hb=2):
    bsz, seq, _ = p3.shape
    wb = hb * GDN_HEAD_DIM
    nhb = GDN_HEADS // hb
    nch = seq // CHUNK
    qb, kb_, vb, zb = (P_QKV_A // wb, (P_QKV_A + GDN_WIDTH) // wb, (P_QKV_A + 2 * GDN_WIDTH) // wb,
                       P_Z_A // wb)
    col = lambda off: pl.BlockSpec((None, seq, wb), lambda b, h, off=off: (b, 0, off + h))
    cwspec = lambda off: pl.BlockSpec((GDN_CONV, wb), lambda b, h, off=off: (0, off + h))
    rows = pl.BlockSpec((None, hb, nch, CHUNK), lambda b, h: (b, h, 0, 0))
    return pl.pallas_call(
        functools.partial(_gdn_kernel, hb=hb, seq=seq),
        grid=(bsz, nhb),
        in_specs=[
            pl.BlockSpec(memory_space=pltpu.SMEM),
            col(qb), col(kb_), col(vb), col(zb),
            cwspec(0), cwspec(nhb), cwspec(2 * nhb),
            rows, rows,
            pl.BlockSpec((1, GDN_HEAD_DIM), lambda b, h: (0, 0)),
        ],
        out_specs=pl.BlockSpec((None, seq, wb), lambda b, h: (b, 0, h)),
        out_shape=jax.ShapeDtypeStruct((bsz, seq, GDN_WIDTH), BF16),
        scratch_shapes=[
            pltpu.VMEM((seq + 8, GDN_HEAD_DIM), F32),
            pltpu.VMEM((hb, seq, GDN_HEAD_DIM), F32),
            pltpu.VMEM((hb, seq, GDN_HEAD_DIM), F32),
            pltpu.VMEM((hb, seq, GDN_HEAD_DIM), F32),
            pltpu.VMEM((hb, nch, CHUNK), F32),
            pltpu.VMEM((hb, nch, CHUNK), F32),
            pltpu.VMEM((hb, nch, GDN_HEAD_DIM + CHUNK, GDN_HEAD_DIM), BF16),
            pltpu.VMEM((hb, nch, GDN_HEAD_DIM, GDN_HEAD_DIM), F32),
        ],
        compiler_params=_cparams(("parallel", "parallel")),
    )(head_params, p3, p3, p3, p3, conv_w, conv_w, conv_w, beta_r, alpha_r, gn)


CONF_HALO = 32
CONF_ROWS = 32
CONF_COPY_ROWS = 64


def _conf_kernel(a_ref, b_ref, ap_ref, bp_ref, z_ref, w_ref, cb_ref, lg_ref, lb_ref, o_ref, glu_scr,
                 *, tl):
    i = pl.program_id(1)
    rows = CONF_HALO + tl
    for rt in range(tl // CONF_COPY_ROWS):
        rs = slice(rt * CONF_COPY_ROWS, (rt + 1) * CONF_COPY_ROWS)
        glu_scr[0, CONF_HALO + rs.start:CONF_HALO + rs.stop, :] = (
            a_ref[rs, :].astype(F32) * _sigmoid(b_ref[rs, :].astype(F32)))
    prev = ap_ref[...].astype(F32) * _sigmoid(bp_ref[...].astype(F32))
    glu_scr[0, 0:CONF_HALO, :] = jnp.where(i > 0, prev, 0.0)
    for s in range(1, SUBLANES):
        for r0 in range(0, rows - SUBLANES, CONF_COPY_ROWS):
            n = min(CONF_COPY_ROWS, rows - SUBLANES - r0)
            glu_scr[s, r0:r0 + n, :] = glu_scr[0, r0 + s:r0 + s + n, :]
    groups = CONF_ROWS // SUBLANES
    for rt in range(tl // CONF_ROWS):
        base = rt * CONF_ROWS
        acc = jnp.broadcast_to(cb_ref[...], (groups, SUBLANES, CONV_WIDTH))
        for kk in range(CONV_KERNEL):
            off = base + CONF_HALO - (CONV_KERNEL - 1) + kk
            s = off % SUBLANES
            win = glu_scr[s, off - s:off - s + CONF_ROWS, :].reshape(groups, SUBLANES, CONV_WIDTH)
            acc = acc + w_ref[kk][None] * win
        acc = acc.reshape(CONF_ROWS, CONV_WIDTH)
        mu = jnp.mean(acc, axis=-1, keepdims=True)
        cen = acc - mu
        var = jnp.mean(cen * cen, axis=-1, keepdims=True)
        y = cen * lax.rsqrt(var + EPS) * lg_ref[...] + lb_ref[...]
        z = z_ref[base:base + CONF_ROWS, :].astype(F32)
        o_ref[base:base + CONF_ROWS, :] = (_silu(y) * _silu(z)).astype(o_ref.dtype)


def _conformer_mixer(p3, conv_w, conv_b, ln_g, ln_b):
    bsz, seq, _ = p3.shape
    tl = 256
    cw = CONV_WIDTH
    ca, cb, cz = P_GLU // cw, P_GLU // cw + 1, P_Z_B // cw
    halo_per_tile = tl // CONF_HALO
    cur = lambda c: pl.BlockSpec((None, tl, cw), lambda b, i, c=c: (b, i, c))
    prev = lambda c: pl.BlockSpec(
        (None, CONF_HALO, cw), lambda b, i, c=c: (b, jnp.maximum(i * halo_per_tile - 1, 0), c))
    vec = pl.BlockSpec((1, cw), lambda b, i: (0, 0))
    return pl.pallas_call(
        functools.partial(_conf_kernel, tl=tl),
        grid=(bsz, seq // tl),
        in_specs=[cur(ca), cur(cb), prev(ca), prev(cb), cur(cz),
                  pl.BlockSpec((CONV_KERNEL, SUBLANES, cw), lambda b, i: (0, 0, 0)), vec, vec, vec],
        out_specs=pl.BlockSpec((None, tl, cw), lambda b, i: (b, i, 0)),
        out_shape=jax.ShapeDtypeStruct((bsz, seq, cw), BF16),
        scratch_shapes=[pltpu.VMEM((SUBLANES, CONF_HALO + tl, cw), F32)],
        compiler_params=_cparams(("parallel", "parallel")),
    )(p3, p3, p3, p3, p3, jnp.broadcast_to(conv_w[:, None, :], (CONV_KERNEL, SUBLANES, cw)), conv_b, ln_g, ln_b)


def _t5_bucket_int(rel):
    nb = REL_BUCKETS // 2
    max_exact = nb // 2
    n = jnp.abs(rel)
    large = jnp.full(rel.shape, max_exact, I32)
    for thr in (12, 16, 23, 32, 46, 64, 91):
        large = large + jnp.where(n >= thr, 1, 0)
    large = jnp.minimum(large, nb - 1)
    return jnp.where(rel > 0, nb, 0) + jnp.where(n < max_exact, n, large)


def _bias_kernel(rb_ref, o_ref):
    kpos = lax.broadcasted_iota(I32, (DSA_KBLOCK, DSA_BLOCK), 0)
    qpos = lax.broadcasted_iota(I32, (DSA_KBLOCK, DSA_BLOCK), 1)
    for kind in range(DSA_BIAS_KINDS):
        bucket = _t5_bucket_int(kpos - qpos + (kind - (DSA_BIAS_KINDS - 1)) * DSA_BLOCK)
        for h in range(ATT_HEADS):
            val = jnp.zeros((DSA_KBLOCK, DSA_BLOCK), F32)
            for b in range(REL_BUCKETS):
                val = jnp.where(bucket == b, rb_ref[b, h], val)
            o_ref[kind, h] = val


def _bias_tiles(rel_bias):
    return pl.pallas_call(
        _bias_kernel,
        in_specs=[pl.BlockSpec(memory_space=pltpu.SMEM)],
        out_shape=jax.ShapeDtypeStruct((DSA_BIAS_KINDS, ATT_HEADS, DSA_KBLOCK, DSA_BLOCK), F32),
    )(rel_bias)


def _for_block_pairs(n, fn, carry):
    def pair(p, c):
        return fn([2 * p, 2 * p + 1], c)

    c = lax.fori_loop(0, lax.shift_right_logical(n, 1), pair, carry)

    @pl.when((n & 1) == 1)
    def _last():
        fn([n - 1], c)


def _dsa_kernel(qi_ref, ki_ref, wt_ref, qc_ref, zc_ref, kc_ref, vc_ref, qg_ref, kg_ref, tab_ref,
                o_ref, kp_scr, kn_scr, vt_scr, qp_scr, key_scr, sel_scr, qs_scr, lg_scr, mx_scr,
                acc_scr, *, seq, topk):
    blk = DSA_BLOCK
    qrows = DSA_QROWS
    kblk = DSA_KBLOCK
    hd = ATT_HEAD_DIM
    i = pl.program_id(1)
    nkv = (i * qrows) // kblk + 1
    kf = float(topk)

    @pl.when(i == 0)
    def _prepare_keys():
        e_r = lax.broadcasted_iota(I32, (hd, hd), 0)
        e_c = lax.broadcasted_iota(I32, (hd, hd), 1)
        eye_hd = jnp.where(e_r == e_c, 1.0, 0.0).astype(BF16)
        low_k = lax.broadcasted_iota(I32, (kblk, LANES), 1) < IDX_HEAD_DIM
        for t in range(seq // kblk):
            rs = slice(t * kblk, (t + 1) * kblk)
            kt = ki_ref[rs, :]
            hi = kt.astype(BF16).astype(F32)
            lo = kt - hi
            kp_scr[rs, 0:LANES] = jnp.where(low_k, hi, pltpu.roll(hi, IDX_HEAD_DIM, 1)).astype(BF16)
            kp_scr[rs, LANES:2 * LANES] = jnp.where(low_k, lo, 0.0).astype(BF16)
            kc = kc_ref[rs, :].astype(F32)
            for g in range(ATT_KV_HEADS):
                ls = slice(g * hd, (g + 1) * hd)
                kg = kc[:, ls]
                kgn = kg * lax.rsqrt(jnp.mean(kg * kg, axis=-1, keepdims=True) + EPS) * kg_ref[...]
                kn_scr[rs, ls] = kgn.astype(BF16)
                vt_scr[t, g, 0:hd, :] = _dg(eye_hd, vc_ref[rs, ls], NT).astype(BF16)
                vt_scr[t, g, hd:, :] = jnp.ones((DSA_SUM_ROWS, kblk), BF16)

    low_q = lax.broadcasted_iota(I32, (qrows, LANES), 1) < IDX_HEAD_DIM
    for c in range(IDX_HEADS * IDX_HEAD_DIM // LANES):
        qt = qi_ref[:, c * LANES:(c + 1) * LANES]
        hi = qt.astype(BF16).astype(F32)
        lo = qt - hi
        hi_sw = pltpu.roll(hi, IDX_HEAD_DIM, 1)
        lo_sw = pltpu.roll(lo, IDX_HEAD_DIM, 1)
        for h, first, second in ((2 * c, jnp.where(low_q, hi, lo_sw), jnp.where(low_q, hi, 0.0)),
                                 (2 * c + 1, jnp.where(low_q, hi_sw, lo), jnp.where(low_q, hi_sw, 0.0))):
            qp_scr[h * qrows:(h + 1) * qrows, 0:LANES] = first.astype(BF16)
            qp_scr[h * qrows:(h + 1) * qrows, LANES:2 * LANES] = second.astype(BF16)
    wsc = wt_ref[...] * ((IDX_HEAD_DIM ** -0.5) * (IDX_HEADS ** -0.5))

    kpos = lax.broadcasted_iota(I32, (kblk, qrows), 0)
    qpos = lax.broadcasted_iota(I32, (kblk, qrows), 1)
    chunk_gap = (kpos // CHUNK) - (qpos // CHUNK)
    q_chunk0 = i * (qrows // CHUNK)

    def fold_rows(x):
        return x.reshape(kblk // SUBLANES, SUBLANES, x.shape[-1])

    def score_blocks(js, carry):
        qp = qp_scr[...]
        s_alls = [_dg(kp_scr[pl.ds(pl.multiple_of(j * kblk, kblk), kblk), :], qp, NT) for j in js]
        for j, s_all in zip(js, s_alls):
            sc = jnp.zeros((kblk, qrows), F32)
            for h in range(IDX_HEADS):
                sc = sc + wsc[h:h + 1, :] * jnp.maximum(s_all[:, h * qrows:(h + 1) * qrows], 0.0)
            sc = sc + 0.0
            bits = pltpu.bitcast(sc, I32)
            key = jnp.where(bits < 0, bits ^ jnp.int32(0x7FFFFFFF), bits)
            slack = q_chunk0 - j * (kblk // CHUNK)
            key_scr[j] = jnp.where(chunk_gap <= slack, key, jnp.int32(INT_MIN))
        return carry

    _for_block_pairs(nkv, score_blocks, 0)

    def count_where(pred, ref):
        def block(j):
            return jnp.sum(fold_rows(jnp.where(pred(key_scr[j], ref), 1.0, 0.0)), axis=0)

        acc = lax.fori_loop(0, lax.shift_right_logical(nkv, 1),
                            lambda p, a: a + (block(2 * p) + block(2 * p + 1)),
                            jnp.zeros((SUBLANES, qrows), F32))
        acc = lax.cond((nkv & 1) == 1, lambda a: a + block(nkv - 1), lambda a: a, acc)
        return jnp.sum(acc, axis=0, keepdims=True)

    ge = lambda kv, ref: kv >= ref
    cnt0 = count_where(ge, jnp.zeros((1, qrows), I32))
    cur0 = jnp.where(cnt0 >= kf, jnp.int32(0), jnp.int32(INT_MIN))

    def bit_body(t, cur):
        cand = cur | lax.shift_left(jnp.int32(1), jnp.int32(30) - t)
        return jnp.where(count_where(ge, cand) >= kf, cand, cur)

    thr = lax.fori_loop(0, 31, bit_body, cur0)
    need = kf - count_where(lambda kv, ref: kv > ref, thr)

    ku = lax.broadcasted_iota(I32, (kblk, kblk), 0)
    kv_ = lax.broadcasted_iota(I32, (kblk, kblk), 1)
    lower = jnp.where(ku >= kv_, 1.0, 0.0).astype(BF16)

    def sel_blocks(js, run):
        keys = [key_scr[j] for j in js]
        eqs = [key == thr for key in keys]
        pres = [_dg(lower, jnp.where(eq, 1.0, 0.0).astype(BF16)) for eq in eqs]
        for j, key, eq, pre in zip(js, keys, eqs, pres):
            take_tie = jnp.where(eq & (run + pre <= need), 0.0, NEG_BIG)
            take = jnp.where(key > thr, 0.0, take_tie)
            sel_scr[j] = jnp.where(key == jnp.int32(INT_MIN), NEG_BIG, take)
            run = run + pre[kblk - 1:kblk, :]
        return run

    _for_block_pairs(nkv, sel_blocks, jnp.zeros((1, qrows), F32))

    for a in range(qrows // blk):
        _dsa_attend(a, i * (qrows // blk) + a, nkv, qc_ref, zc_ref, qg_ref, tab_ref, o_ref,
                    kn_scr, vt_scr, sel_scr, qs_scr, lg_scr, mx_scr, acc_scr)


def _dsa_attend(a, qblock, nkv, qc_ref, zc_ref, qg_ref, tab_ref, o_ref,
                kn_scr, vt_scr, sel_scr, qs_scr, lg_scr, mx_scr, acc_scr):
    blk = DSA_BLOCK
    kblk = DSA_KBLOCK
    hd = ATT_HEAD_DIM
    cols4 = ATT_GROUP * blk
    rq = slice(a * blk, (a + 1) * blk)
    qc = qc_ref[rq, :].astype(F32)
    for h in range(ATT_HEADS):
        qh = qc[:, h * hd:(h + 1) * hd]
        qn = qh * lax.rsqrt(jnp.mean(qh * qh, axis=-1, keepdims=True) + EPS) * qg_ref[...]
        g, hh = divmod(h, ATT_GROUP)
        qs_scr[g, hh * blk:(hh + 1) * blk, :] = (qn * (hd ** -0.5)).astype(BF16)
    mx_scr[...] = jnp.full(mx_scr.shape, NEG_BIG, F32)

    def fold(x):
        return x.reshape(kblk // SUBLANES, SUBLANES, x.shape[-1])

    groups = range(ATT_KV_HEADS)

    def logit_blocks(js, carry):
        jg = [(n, g) for n in range(len(js)) for g in groups]
        kks = [kn_scr[pl.ds(pl.multiple_of(j * kblk, kblk), kblk), :] for j in js]
        raw = [_dg(kks[n][:, g * hd:(g + 1) * hd], qs_scr[g], NT) for n, g in jg]
        negs = [jnp.concatenate([sel_scr[j, :, rq]] * ATT_GROUP, axis=1) for j in js]
        kinds = [jnp.clip(j * (kblk // blk) - qblock + (DSA_BIAS_KINDS - 1), 0, DSA_BIAS_KINDS - 1)
                 for j in js]
        tops = [mx_scr[g] for g in groups]
        for (n, g), r in zip(jg, raw):
            bias = jnp.concatenate([tab_ref[kinds[n], g * ATT_GROUP + hh] for hh in range(ATT_GROUP)],
                                   axis=1)
            lg = r + bias + negs[n]
            lg_scr[js[n], g] = lg
            tops[g] = jnp.maximum(tops[g], jnp.max(fold(lg), axis=0))
        for g in groups:
            mx_scr[g] = tops[g]
        return carry

    _for_block_pairs(nkv, logit_blocks, 0)
    for g in groups:
        mx_scr[g] = jnp.broadcast_to(jnp.max(mx_scr[g], axis=0, keepdims=True), (SUBLANES, cols4))
    acc_scr[...] = jnp.zeros(acc_scr.shape, F32)

    def pv_blocks(js, carry):
        jg = [(j, g) for j in js for g in groups]
        ps = [jnp.exp(lg_scr[j, g] - mx_scr[g, 0:1, :]).astype(BF16) for j, g in jg]
        pvs = [_dg(vt_scr[j, g], p) for (j, g), p in zip(jg, ps)]
        for g in groups:
            acc_scr[g] += sum(pv for (_, gg), pv in zip(jg, pvs) if gg == g)
        return carry

    _for_block_pairs(nkv, pv_blocks, 0)

    for h in range(ATT_HEADS):
        g, hh = divmod(h, ATT_GROUP)
        cs = slice(hh * blk, (hh + 1) * blk)
        ls = slice(h * hd, (h + 1) * hd)
        o_t = acc_scr[g, 0:hd, cs] / acc_scr[g, hd:hd + 1, cs]
        z = zc_ref[rq, ls].astype(F32)
        o_ref[rq, ls] = (o_t.T * _silu(z)).astype(o_ref.dtype)


def _dsa_mixer(p3, s3, w_t, q_gain, k_gain, bias_tab):
    bsz, seq, _ = p3.shape
    blk = DSA_BLOCK
    kblk = DSA_KBLOCK
    qrows = DSA_QROWS
    nkb = seq // kblk
    rows4 = ATT_GROUP * blk
    topk = min(TOPK_MAX, seq // 4)
    return pl.pallas_call(
        functools.partial(_dsa_kernel, seq=seq, topk=topk),
        grid=(bsz, seq // qrows),
        in_specs=[
            pl.BlockSpec((None, qrows, IDX_HEADS * IDX_HEAD_DIM), lambda b, i: (b, i, S_Q_I // (IDX_HEADS * IDX_HEAD_DIM))),
            pl.BlockSpec((None, seq, LANES), lambda b, i: (b, 0, S_K_I // LANES)),
            pl.BlockSpec((None, IDX_HEADS, qrows), lambda b, i: (b, 0, i)),
            pl.BlockSpec((None, qrows, ATT_WIDTH), lambda b, i: (b, i, P_Q_C // ATT_WIDTH)),
            pl.BlockSpec((None, qrows, ATT_WIDTH), lambda b, i: (b, i, P_Z_C // ATT_WIDTH)),
            pl.BlockSpec((None, seq, ATT_KV_WIDTH), lambda b, i: (b, 0, P_K_C // ATT_KV_WIDTH)),
            pl.BlockSpec((None, seq, ATT_KV_WIDTH), lambda b, i: (b, 0, P_V_C // ATT_KV_WIDTH)),
            pl.BlockSpec((1, ATT_HEAD_DIM), lambda b, i: (0, 0)),
            pl.BlockSpec((1, ATT_HEAD_DIM), lambda b, i: (0, 0)),
            pl.BlockSpec((DSA_BIAS_KINDS, ATT_HEADS, kblk, blk), lambda b, i: (0, 0, 0, 0)),
        ],
        out_specs=pl.BlockSpec((None, qrows, ATT_WIDTH), lambda b, i: (b, i, 0)),
        out_shape=jax.ShapeDtypeStruct((bsz, seq, ATT_WIDTH), BF16),
        scratch_shapes=[
            pltpu.VMEM((seq, IDX_PAD), BF16),
            pltpu.VMEM((seq, ATT_KV_WIDTH), BF16),
            pltpu.VMEM((nkb, ATT_KV_HEADS, ATT_HEAD_DIM + DSA_SUM_ROWS, kblk), BF16),
            pltpu.VMEM((IDX_HEADS * qrows, IDX_PAD), BF16),
            pltpu.VMEM((nkb, kblk, qrows), I32),
            pltpu.VMEM((nkb, kblk, qrows), F32),
            pltpu.VMEM((ATT_KV_HEADS, rows4, ATT_HEAD_DIM), BF16),
            pltpu.VMEM((nkb, ATT_KV_HEADS, kblk, rows4), F32),
            pltpu.VMEM((ATT_KV_HEADS, SUBLANES, rows4), F32),
            pltpu.VMEM((ATT_KV_HEADS, ATT_HEAD_DIM + DSA_SUM_ROWS, rows4), F32),
        ],
        compiler_params=_cparams(("arbitrary", "arbitrary")),
    )(s3, s3, w_t, p3, p3, p3, p3, q_gain, k_gain, bias_tab)


def _merge_kernel(ya_ref, yb_ref, yc_ref, wa_ref, wb_ref, wc_ref, ga_ref, gb_ref, gc_ref,
                  ba_ref, bb_ref, bc_ref, o_ref):
    acc = _sigmoid(ga_ref[...].astype(F32) + ba_ref[...]) * _dg(ya_ref[...], wa_ref[...])
    acc = acc + _sigmoid(gb_ref[...].astype(F32) + bb_ref[...]) * _dg(yb_ref[...], wb_ref[...])
    acc = acc + _sigmoid(gc_ref[...].astype(F32) + bc_ref[...]) * _dg(yc_ref[...], wc_ref[...])
    o_ref[...] = acc.astype(o_ref.dtype)


def _merge(ya, yb, yc, wa, wb, wc, p2, b_gate):
    t = ya.shape[0]
    d = D_MODEL
    tm, tn = min(t, 1024), 512
    nj = d // tn
    g0 = P_GATE // tn
    yspec = pl.BlockSpec((tm, ya.shape[1]), lambda i, j: (i, 0))
    wspec = pl.BlockSpec((wa.shape[0], tn), lambda i, j: (0, j))
    gspec = lambda br: pl.BlockSpec((tm, tn), lambda i, j, br=br: (i, g0 + br * nj + j))
    bspec = lambda br: pl.BlockSpec((1, tn), lambda i, j, br=br: (0, br * nj + j))
    return pl.pallas_call(
        _merge_kernel,
        grid=(t // tm, nj),
        in_specs=[yspec, yspec, yspec, wspec, wspec, wspec, gspec(0), gspec(1), gspec(2),
                  bspec(0), bspec(1), bspec(2)],
        out_specs=pl.BlockSpec((tm, tn), lambda i, j: (i, j)),
        out_shape=jax.ShapeDtypeStruct((t, d), BF16),
        compiler_params=_cparams(("parallel", "parallel")),
    )(ya, yb, yc, wa, wb, wc, p2, p2, p2, b_gate, b_gate, b_gate)


def _out_kernel(m_ref, w_ref, x_ref, g_ref, o_ref):
    o_ref[...] = x_ref[...] + g_ref[...] * _dg(m_ref[...], w_ref[...])


def _out_proj(merged, w_out, x2, mod, seq):
    t, d = x2.shape
    tm, tn = min(seq, 2048), 512
    per_batch = seq // tm
    nj = d // tn
    return pl.pallas_call(
        _out_kernel,
        grid=(t // tm, nj),
        in_specs=[
            pl.BlockSpec((tm, d), lambda i, j: (i, 0)),
            pl.BlockSpec((d, tn), lambda i, j: (0, j)),
            pl.BlockSpec((tm, tn), lambda i, j: (i, j)),
            pl.BlockSpec((None, 1, tn), lambda i, j: (i // per_batch, 0, 2 * nj + j)),
        ],
        out_specs=pl.BlockSpec((tm, tn), lambda i, j: (i, j)),
        out_shape=jax.ShapeDtypeStruct((t, d), F32),
        compiler_params=_cparams(("parallel", "parallel")),
    )(merged, w_out, x2, mod)


def _wide_weight(w_in, layer):
    cols = [(_OFF_QKV_A, 3 * GDN_WIDTH), (_OFF_Z_A, GDN_WIDTH), (_OFF_GLU, 2 * CONV_WIDTH),
            (_OFF_Z_B, CONV_WIDTH), (_OFF_Q_C, ATT_WIDTH), (_OFF_Z_C, ATT_WIDTH),
            (_OFF_GATE, N_BRANCH * D_MODEL), (_OFF_K_C, ATT_KV_WIDTH), (_OFF_V_C, ATT_KV_WIDTH)]
    return jnp.concatenate([w_in[layer, :, o:o + n].astype(BF16) for o, n in cols], axis=1)


def _narrow_weight(w_in, layer):
    cols = [(_OFF_Q_I, IDX_HEADS * IDX_HEAD_DIM), (_OFF_K_I, IDX_HEAD_DIM), (_OFF_W_I, IDX_HEADS),
            (_OFF_BETA, GDN_HEADS), (_OFF_ALPHA, GDN_HEADS)]
    used = sum(n for _, n in cols)
    parts = [w_in[layer, :, o:o + n].astype(BF16) for o, n in cols]
    parts.append(jnp.zeros((w_in.shape[1], NS_NARROW - used), BF16))
    return jnp.concatenate(parts, axis=1)


def kernel(x, c, rel_bias, w_ada, b_ada, norm_g, w_in, b_gate, conv_a_w, a_log, dt_bias, gdn_norm_g,
           w_proj_a, conv_b_w, conv_b_b, ln_b_g, ln_b_b, w_proj_b, q_norm_g, k_norm_g, w_proj_c, w_out):
    bsz, seq, d = x.shape
    t = bsz * seq
    nch = seq // CHUNK
    mod_all = _ada_modulation(c, w_ada, b_ada)
    bias_tab = _bias_tiles(rel_bias)
    for layer in range(DEPTH):
        mod = mod_all[layer].reshape(bsz, 1, 3 * d)
        h = _norm_modulate(x, norm_g[layer].reshape(1, d), mod).reshape(t, d)
        p2 = _matmul(h, _wide_weight(w_in, layer), BF16, min(t, 2048), 512)
        s2 = _matmul(h, _narrow_weight(w_in, layer), F32, min(t, 1024), NS_NARROW)
        p3 = p2.reshape(bsz, seq, NP_WIDE)
        s3 = s2.reshape(bsz, seq, NS_NARROW)

        def head_rows(off):
            r = s3[:, :, off:off + GDN_HEADS]
            return jnp.swapaxes(r, 1, 2).reshape(bsz, GDN_HEADS, nch, CHUNK)

        head_params = jnp.stack([a_log[layer], dt_bias[layer]], axis=1)
        y_a = _gdn_mixer(p3, head_rows(S_BETA), head_rows(S_ALPHA), conv_a_w[layer], head_params,
                         gdn_norm_g[layer].reshape(1, GDN_HEAD_DIM))
        y_b = _conformer_mixer(p3, conv_b_w[layer], conv_b_b[layer].reshape(1, CONV_WIDTH),
                               ln_b_g[layer].reshape(1, CONV_WIDTH), ln_b_b[layer].reshape(1, CONV_WIDTH))
        y_c = _dsa_mixer(p3, s3, jnp.swapaxes(s3[:, :, S_W_I:S_W_I + IDX_HEADS], 1, 2),
                         q_norm_g[layer].reshape(1, ATT_HEAD_DIM), k_norm_g[layer].reshape(1, ATT_HEAD_DIM),
                         bias_tab)
        merged = _merge(y_a.reshape(t, GDN_WIDTH), y_b.reshape(t, CONV_WIDTH), y_c.reshape(t, ATT_WIDTH),
                        w_proj_a[layer].astype(BF16), w_proj_b[layer].astype(BF16),
                        w_proj_c[layer].astype(BF16), p2, b_gate[layer].reshape(1, N_BRANCH * d))
        x = _out_proj(merged, w_out[layer].astype(BF16), x.reshape(t, d), mod, seq).reshape(bsz, seq, d)
    return x
```
